```python
import math
import jax
import jax.numpy as jnp
from jax import lax
import numpy as np

D_MODEL = 2048
BATCH = 8
SEQ = 2048
DEPTH = 1
DEC_BATCH = 128
DEC_SEQ = 4
PAST_LEN = 2048
PAGE_SIZE = 128

R_HEADS = 16
R_HEAD_DIM = 64
R_WIDTH = R_HEADS * R_HEAD_DIM
DECAY_LORA = 96
ICLR_LORA = 96
RW_IN = 4 * R_WIDTH + DECAY_LORA + ICLR_LORA

A_HEADS = 8
A_HEAD_DIM = 64
A_VDIM = 2 * A_HEAD_DIM
A_WIDTH = A_HEADS * A_VDIM
AT_IN = 4 * A_WIDTH

GATE_IN = 2 * D_MODEL
TOTAL_IN = RW_IN + AT_IN + GATE_IN

Q_BLOCK = 128
NORM_EPS = 1e-6
RWKV_GN_EPS = 64e-5
NEG_INF = -1e30
ATTN_SCALE = A_HEAD_DIM ** -0.5

kernel_name = 'rwkv7_diffattn_gated_hybrid_step'


def rms_norm(x, g):
    xf = x.astype(jnp.float32)
    y = xf * lax.rsqrt(jnp.mean(xf * xf, axis=-1, keepdims=True) + NORM_EPS)
    return (y * g.astype(jnp.float32)).astype(x.dtype)


def rwkv7_mixer(p, prev_row, wkv0, mu, w0, w2, a0, a2, k_k, k_a, r_k, lnx_w, lnx_b):
    B, T, _ = p.shape
    f32 = jnp.float32
    pf = p.astype(f32)
    prev = jnp.concatenate([prev_row.astype(f32)[:, None, :], pf[:, :-1]], axis=1)
    ps = pf + (prev - pf) * mu.astype(f32)
    r, k, v, g, wd, ad = jnp.split(
        ps, [R_WIDTH, 2 * R_WIDTH, 3 * R_WIDTH, 4 * R_WIDTH, 4 * R_WIDTH + DECAY_LORA], axis=-1)
    w = -jax.nn.softplus(-(w0.astype(f32) + jnp.tanh(wd) @ w2.astype(f32))) - 0.5
    decay = jnp.exp(-jnp.exp(w))
    a = jax.nn.sigmoid(a0.astype(f32) + ad @ a2.astype(f32))
    kk = (k * k_k.astype(f32)).reshape(B, T, R_HEADS, R_HEAD_DIM)
    kk = kk / jnp.maximum(jnp.sqrt(jnp.sum(kk * kk, axis=-1, keepdims=True)), 1e-12)
    k = k * (1.0 + (a - 1.0) * k_a.astype(f32))
    heads = lambda t: t.reshape(B, T, R_HEADS, R_HEAD_DIM)
    r, k, v, a, decay = heads(r), heads(k), heads(v), heads(a), heads(decay)

    def step(S, inp):
        r_t, k_t, v_t, kk_t, b_t, d_t = inp
        s_kk = jnp.einsum('bhvk,bhk->bhv', S, kk_t)
        S = (S * d_t[:, :, None, :] - s_kk[..., None] * b_t[:, :, None, :]
             + v_t[..., None] * k_t[:, :, None, :])
        return S, jnp.einsum('bhvk,bhk->bhv', S, r_t)

    xs = tuple(jnp.swapaxes(t, 0, 1) for t in (r, k, v, kk, kk * a, decay))
    S_T, ys = lax.scan(step, wkv0.astype(f32), xs)
    y = jnp.swapaxes(ys, 0, 1)
    mean = jnp.mean(y, axis=-1, keepdims=True)
    var = jnp.mean(jnp.square(y - mean), axis=-1, keepdims=True)
    yn = ((y - mean) * lax.rsqrt(var + RWKV_GN_EPS)).reshape(B, T, R_WIDTH)
    yn = yn * lnx_w.astype(f32) + lnx_b.astype(f32)
    bonus = jnp.sum(r * k * r_k.astype(f32), axis=-1, keepdims=True) * v
    out = (yn + bonus.reshape(B, T, R_WIDTH)) * jax.nn.silu(g)
    return out.astype(p.dtype), S_T.astype(wkv0.dtype), p[:, -1]


def diff_combine(s, lam):
    pr = jax.nn.softmax(s, axis=-1)
    return pr[:, :, 0] - lam * pr[:, :, 1]


def diff_attn_prompt(q, k, v, lam):
    B, S = q.shape[:2]
    nb = S // Q_BLOCK
    qb = jnp.moveaxis(q.reshape(B, nb, Q_BLOCK, A_HEADS, 2, A_HEAD_DIM), 1, 0)
    k_pos = jnp.arange(S)
    vf = v.astype(jnp.float32)

    def block(args):
        q_blk, blk = args
        q_pos = blk * Q_BLOCK + jnp.arange(Q_BLOCK)
        s = jnp.einsum('bqhcd,bkhcd->bhcqk', q_blk, k).astype(jnp.float32) * ATTN_SCALE
        s = jnp.where(k_pos[None, :] <= q_pos[:, None], s, NEG_INF)
        return jnp.einsum('bhqk,bkhe->bqhe', diff_combine(s, lam), vf)

    o = lax.map(block, (qb, jnp.arange(nb)))
    return jnp.moveaxis(o, 0, 1).reshape(B, S, A_HEADS, A_VDIM)


def diff_attn_sample(q, k_new, v_new, cache_k, cache_v, layer, page_table, lam):
    DB, T = q.shape[:2]
    k_past = cache_k[layer, page_table].reshape(DB, -1, A_HEADS, 2, A_HEAD_DIM)
    v_past = cache_v[layer, page_table].reshape(DB, -1, A_HEADS, A_VDIM)
    P = k_past.shape[1]
    s_past = jnp.einsum('bqhcd,bkhcd->bhcqk', q, k_past).astype(jnp.float32) * ATTN_SCALE
    s_new = jnp.einsum('bqhcd,bkhcd->bhcqk', q, k_new).astype(jnp.float32) * ATTN_SCALE
    t = jnp.arange(T)
    s_new = jnp.where(t[None, :] <= t[:, None], s_new, NEG_INF)
    attn = diff_combine(jnp.concatenate([s_past, s_new], axis=-1), lam)
    return (jnp.einsum('bhqk,bkhe->bqhe', attn[..., :P], v_past.astype(jnp.float32))
            + jnp.einsum('bhqk,bkhe->bqhe', attn[..., P:], v_new.astype(jnp.float32)))


def diff_head_norm(o, subln_w, lam_init, dtype):
    B, T = o.shape[:2]
    o = o * lax.rsqrt(jnp.mean(o * o, axis=-1, keepdims=True) + NORM_EPS)
    o = o * subln_w.astype(jnp.float32) * (1.0 - lam_init)
    return o.reshape(B, T, A_WIDTH).astype(dtype)


def mixer_layer(x, shift_prev, wkv0, attend, norm_in, w_in, mu_shift, w0, w2, a0, a2, k_k, k_a,
                r_k, lnx_w, lnx_b, subln_w, lam_init, w_br_rwkv, w_br_attn, w_out):
    B, T, _ = x.shape
    h = rms_norm(x, norm_in)
    proj = h @ w_in
    p_rw, p_at, p_gate = jnp.split(proj, [RW_IN, RW_IN + AT_IN], axis=-1)
    rw_out, wkv_T, shift_T = rwkv7_mixer(p_rw, shift_prev, wkv0, mu_shift, w0, w2, a0, a2,
                                         k_k, k_a, r_k, lnx_w, lnx_b)
    q, k, v, ag = jnp.split(p_at, 4, axis=-1)
    q = q.reshape(B, T, A_HEADS, 2, A_HEAD_DIM)
    k = k.reshape(B, T, A_HEADS, 2, A_HEAD_DIM)
    v = v.reshape(B, T, A_HEADS, A_VDIM)
    at_out = diff_head_norm(attend(q, k, v), subln_w, lam_init, x.dtype) * jax.nn.silu(ag)
    g_rw, g_at = jnp.split(p_gate, 2, axis=-1)
    merged = (jax.nn.sigmoid(g_rw) * (rw_out @ w_br_rwkv)
              + jax.nn.sigmoid(g_at) * (at_out @ w_br_attn))
    return x + merged @ w_out, k, v, wkv_T, shift_T


def setup_inputs(seed: int = 0) -> dict:
    key = jax.random.key(seed)
    ks = jax.random.split(key, 32)
    n_pages = PAST_LEN // PAGE_SIZE
    n_used = DEC_BATCH * n_pages
    n_pool = n_used + n_used // 4
    nrm = lambda kk, shape, s: jax.random.normal(kk, shape, jnp.float32) * s
    page_table = jax.random.permutation(ks[6], n_pool)[:n_used].reshape(DEC_BATCH, n_pages).astype(jnp.int32)
    return {
        'x_prompt': nrm(ks[0], (BATCH, SEQ, D_MODEL), 1.0),
        'x_sample': nrm(ks[1], (DEC_BATCH, DEC_SEQ, D_MODEL), 1.0),
        'cache_k': nrm(ks[2], (DEPTH, n_pool, PAGE_SIZE, A_HEADS, 2, A_HEAD_DIM), 1.0),
        'cache_v': nrm(ks[3], (DEPTH, n_pool, PAGE_SIZE, A_HEADS, A_VDIM), 1.0),
        'state_wkv': nrm(ks[4], (DEPTH, DEC_BATCH, R_HEADS, R_HEAD_DIM, R_HEAD_DIM), 0.3),
        'state_shift': nrm(ks[5], (DEPTH, DEC_BATCH, RW_IN), 1.0),
        'page_table': page_table,
        'norm_in': 1.0 + nrm(ks[7], (DEPTH, D_MODEL), 0.02),
        'w_in': nrm(ks[8], (DEPTH, D_MODEL, TOTAL_IN), D_MODEL ** -0.5),
        'mu_shift': jax.random.uniform(ks[9], (DEPTH, RW_IN), jnp.float32),
        'w0': jax.random.uniform(ks[10], (DEPTH, R_WIDTH), jnp.float32, -6.0, 1.0),
        'w2': nrm(ks[11], (DEPTH, DECAY_LORA, R_WIDTH), 0.1),
        'a0': nrm(ks[12], (DEPTH, R_WIDTH), 0.1),
        'a2': nrm(ks[13], (DEPTH, ICLR_LORA, R_WIDTH), 0.1),
        'k_k': 0.85 + nrm(ks[14], (DEPTH, R_WIDTH), 0.02),
        'k_a': 1.0 + nrm(ks[15], (DEPTH, R_WIDTH), 0.02),
        'r_k': nrm(ks[16], (DEPTH, R_HEADS, R_HEAD_DIM), 0.1),
        'lnx_w': 1.0 + nrm(ks[17], (DEPTH, R_WIDTH), 0.02),
        'lnx_b': nrm(ks[18], (DEPTH, R_WIDTH), 0.02),
        'lambda_q1': nrm(ks[19], (DEPTH, A_HEAD_DIM), 0.1),
        'lambda_k1': nrm(ks[20], (DEPTH, A_HEAD_DIM), 0.1),
        'lambda_q2': nrm(ks[21], (DEPTH, A_HEAD_DIM), 0.1),
        'lambda_k2': nrm(ks[22], (DEPTH, A_HEAD_DIM), 0.1),
        'subln_w': 1.0 + nrm(ks[23], (DEPTH, A_VDIM), 0.02),
        'w_br_rwkv': nrm(ks[24], (DEPTH, R_WIDTH, D_MODEL), R_WIDTH ** -0.5),
        'w_br_attn': nrm(ks[25], (DEPTH, A_WIDTH, D_MODEL), A_WIDTH ** -0.5),
        'w_out': nrm(ks[26], (DEPTH, D_MODEL, D_MODEL), D_MODEL ** -0.5),
        'norm_f': 1.0 + nrm(ks[27], (D_MODEL,), 0.02),
    }


def reference(x_prompt, x_sample, cache_k, cache_v, state_wkv, state_shift, page_table,
              norm_in, w_in, mu_shift, w0, w2, a0, a2, k_k, k_a, r_k, lnx_w, lnx_b,
              lambda_q1, lambda_k1, lambda_q2, lambda_k2, subln_w, w_br_rwkv, w_br_attn,
              w_out, norm_f):
    xp, xs = x_prompt, x_sample
    Bp = xp.shape[0]
    kp_l, vp_l, ks_l, vs_l, wp_l, ws_l, sp_l, ss_l = [], [], [], [], [], [], [], []
    for l in range(DEPTH):
        lam_init = 0.8 - 0.6 * math.exp(-0.3 * l)
        lam = (jnp.exp(jnp.sum(lambda_q1[l].astype(jnp.float32) * lambda_k1[l].astype(jnp.float32)))
               - jnp.exp(jnp.sum(lambda_q2[l].astype(jnp.float32) * lambda_k2[l].astype(jnp.float32)))
               + lam_init)
        lp = (norm_in[l], w_in[l], mu_shift[l], w0[l], w2[l], a0[l], a2[l], k_k[l], k_a[l],
              r_k[l], lnx_w[l], lnx_b[l], subln_w[l], lam_init, w_br_rwkv[l], w_br_attn[l], w_out[l])
        xp, kp, vp, wkvp, shp = mixer_layer(
            xp, jnp.zeros((Bp, RW_IN), xp.dtype),
            jnp.zeros((Bp, R_HEADS, R_HEAD_DIM, R_HEAD_DIM), xp.dtype),
            lambda q, k, v, lam=lam: diff_attn_prompt(q, k, v, lam), *lp)
        xs, kn, vn, wkvs, shs = mixer_layer(
            xs, state_shift[l], state_wkv[l],
            lambda q, k, v, lam=lam, l=l: diff_attn_sample(q, k, v, cache_k, cache_v, l, page_table, lam),
            *lp)
        kp_l.append(kp); vp_l.append(vp); ks_l.append(kn); vs_l.append(vn)
        wp_l.append(wkvp); ws_l.append(wkvs); sp_l.append(shp); ss_l.append(shs)
    y_prompt = rms_norm(xp, norm_f)
    y_sample = rms_norm(xs, norm_f)
    return (y_prompt, y_sample, jnp.stack(kp_l), jnp.stack(vp_l), jnp.stack(ks_l), jnp.stack(vs_l),
            jnp.stack(wp_l), jnp.stack(ws_l), jnp.stack(sp_l), jnp.stack(ss_l))
```

```python
import functools
import math

import jax
import jax.numpy as jnp
from jax import lax
from jax.experimental import pallas as pl
from jax.experimental.pallas import tpu as pltpu

F32 = jnp.float32
BF16 = jnp.bfloat16

LANES = 128
HEAD_DIM = 64
CHUNK = 64
NORM_EPS = 1e-6
RWKV_GN_EPS = 64e-5
NEG_INF = -1e30
ATTN_SCALE = HEAD_DIM ** -0.5
VMEM_LIMIT = 56 * 1024 * 1024

HI = lax.Precision.HIGHEST


def _dot(a, b, prec=None):
    return lax.dot_general(a, b, (((1,), (0,)), ((), ())), precision=prec, preferred_element_type=F32)


def _dot_nt(a, b, prec=None):
    return lax.dot_general(a, b, (((1,), (1,)), ((), ())), precision=prec, preferred_element_type=F32)


def _dot_tn(a, b, prec=None):
    return lax.dot_general(a, b, (((0,), (0,)), ((), ())), precision=prec, preferred_element_type=F32)


def _iota2(shape, dim):
    return lax.broadcasted_iota(jnp.int32, shape, dim)


def _sigmoid(x):
    return 1.0 / (1.0 + jnp.exp(-x))


def _silu(x):
    return x * _sigmoid(x)


def _softplus(x):
    return jnp.maximum(x, 0.0) + jnp.log(1.0 + jnp.exp(-jnp.abs(x)))


def _params(semantics):
    return pltpu.CompilerParams(dimension_semantics=semantics, vmem_limit_bytes=VMEM_LIMIT)


def _inproj_kernel(x_ref, g_ref, w_ref, o_ref, xn_ref):
    @pl.when(pl.program_id(1) == 0)
    def _():
        x = x_ref[...]
        xn = x * lax.rsqrt(jnp.mean(x * x, axis=-1, keepdims=True) + NORM_EPS) * g_ref[...]
        xn_ref[...] = xn.astype(BF16)

    o_ref[...] = _dot(xn_ref[...], w_ref[...])


def _inproj(x, g, w, *, tm, tn):
    n, d = x.shape
    pw = w.shape[1]
    return pl.pallas_call(
        _inproj_kernel,
        grid=(n // tm, pw // tn),
        in_specs=[pl.BlockSpec((tm, d), lambda i, j: (i, 0)),
                  pl.BlockSpec((1, d), lambda i, j: (0, 0)),
                  pl.BlockSpec((d, tn), lambda i, j: (0, j))],
        out_specs=pl.BlockSpec((tm, tn), lambda i, j: (i, j)),
        out_shape=jax.ShapeDtypeStruct((n, pw), F32),
        scratch_shapes=[pltpu.VMEM((tm, d), BF16)],
        compiler_params=_params(("parallel", "arbitrary")),
        name="inproj",
    )(x, g, w)


def _rwkv_prepare(ps_r, ps_k, ps_v, ps_g, ps_wd, ps_ad, hp, w2, a2, gsum):
    w0, a0, k_k, k_a = hp[0:1], hp[1:2], hp[2:3], hp[3:4]
    w_pre = w0 + _dot(jnp.tanh(ps_wd).astype(BF16), w2)
    w = -_softplus(-w_pre) - 0.5
    log_decay = -jnp.exp(w)
    a = _sigmoid(a0 + _dot(ps_ad.astype(BF16), a2))
    kk = ps_k * k_k
    kk = kk / jnp.maximum(jnp.sqrt(gsum(kk * kk)), 1e-12)
    k = ps_k * (1.0 + (a - 1.0) * k_a)
    return ps_r, k, ps_v, ps_g, kk, kk * a, log_decay


def _rwkv_finish(y, r, k, v, g, hp, gsum):
    r_k, lnx_w, lnx_b = hp[4:5], hp[5:6], hp[6:7]
    mean = gsum(y) * (1.0 / HEAD_DIM)
    d = y - mean
    var = gsum(d * d) * (1.0 / HEAD_DIM)
    yn = d * lax.rsqrt(var + RWKV_GN_EPS) * lnx_w + lnx_b
    bonus = gsum(r * k * r_k) * v
    return (yn + bonus) * _silu(g)


def _head_masks(shape):
    lane = _iota2(shape, 1)
    m0 = (lane < HEAD_DIM).astype(F32)
    return m0, 1.0 - m0


def _group_ones():
    i = _iota2((LANES, LANES), 0) // HEAD_DIM
    j = _iota2((LANES, LANES), 1) // HEAD_DIM
    return (i == j).astype(F32)


def _rwkv_prompt_kernel(pr_ref, pk_ref, pv_ref, pg_ref, pwd_ref, pad_ref,
                        mr_ref, mk_ref, mv_ref, mg_ref, mwd_ref, mad_ref,
                        hp_ref, w2_ref, a2_ref,
                        out_ref, state_ref,
                        carry_ref, m_ref, r_s, k_s, v_s, g_s, kk_s, b_s, ld_s, *, tb):
    t_idx = pl.program_id(2)
    n_t = pl.num_programs(2)
    C = CHUNK

    @pl.when(t_idx == 0)
    def _():
        carry_ref[...] = jnp.zeros_like(carry_ref)
        m_ref[...] = jnp.zeros_like(m_ref)

    group_ones = _group_ones()
    gsum = lambda x: _dot(x, group_ones, HI)
    row = _iota2((tb, LANES), 0)

    def shifted(p_ref, mu_ref, idx):
        p = p_ref[0]
        prev = pltpu.roll(p, 1, axis=0)
        prev = jnp.where(row == 0, carry_ref[idx], prev)
        carry_ref[idx] = p[tb - 1:tb, :]
        return p + (prev - p) * mu_ref[...]

    hp = hp_ref[...]
    r, k, v, g, kk, b, ld = _rwkv_prepare(
        shifted(pr_ref, mr_ref, 0), shifted(pk_ref, mk_ref, 1), shifted(pv_ref, mv_ref, 2),
        shifted(pg_ref, mg_ref, 3), shifted(pwd_ref, mwd_ref, 4), shifted(pad_ref, mad_ref, 5),
        hp, w2_ref[...], a2_ref[...], gsum)
    r_s[...] = r
    k_s[...] = k
    v_s[...] = v
    g_s[...] = g
    kk_s[...] = kk
    b_s[...] = b
    ld_s[...] = ld

    m0, m1 = _head_masks((C, LANES))
    ii = _iota2((2 * C, 2 * C), 0)
    jj = _iota2((2 * C, 2 * C), 1)
    strict = ii > jj
    incl = ii >= jj
    eye = ii == jj
    tri = (_iota2((C, C), 0) >= _iota2((C, C), 1)).astype(F32)
    stack2 = lambda x: jnp.concatenate([x * m0, x * m1], axis=0)

    def chunk(ci, m):
        sl = pl.ds(pl.multiple_of(ci * C, C), C)
        r, k, v, kk, b, ld = r_s[sl, :], k_s[sl, :], v_s[sl, :], kk_s[sl, :], b_s[sl, :], ld_s[sl, :]
        c = _dot(tri, ld, HI)
        c_end = c[C - 1:C, :]
        p_inv = jnp.exp(-c)
        p_end = jnp.exp(c_end - c)
        r2 = stack2(r * jnp.exp(c))
        kk2 = stack2(kk * jnp.exp(c - ld))
        kh2 = stack2(k * p_inv)
        bh2 = stack2(b * p_inv)
        ke2 = stack2(k * p_end)
        be2 = stack2(b * p_end)
        v2 = stack2(v)
        l_b = jnp.where(strict, _dot_nt(kk2, bh2, HI), 0.0)
        l_k = jnp.where(strict, _dot_nt(kk2, kh2, HI), 0.0)
        a_rk = jnp.where(incl, _dot_nt(r2, kh2, HI), 0.0)
        a_rb = jnp.where(incl, _dot_nt(r2, bh2, HI), 0.0)
        x = _dot(kk2, m, HI) + _dot(l_k, v2, HI)
        u = x - _dot(l_b, x, HI)
        pw = l_b
        for _ in range(5):
            pw = _dot(pw, pw, HI)
            u = u + _dot(pw, u, HI)
        y2 = _dot(r2, m, HI) + _dot(a_rk, v2, HI) - _dot(a_rb, u, HI)
        y = y2[:C] + y2[C:]
        p_c = jnp.exp(c_end)
        p_c_col = jnp.sum(jnp.where(eye, jnp.broadcast_to(p_c, (2 * C, LANES)), 0.0), axis=1, keepdims=True)
        m_new = m * p_c_col + _dot_tn(ke2, v2, HI) - _dot_tn(be2, u, HI)
        out_ref[0, sl, :] = _rwkv_finish(y, r, k, v, g_s[sl, :], hp, gsum).astype(out_ref.dtype)
        return m_new

    m = lax.fori_loop(0, tb // C, chunk, m_ref[...])
    m_ref[...] = m

    @pl.when(t_idx == n_t - 1)
    def _():
        mt = m.T
        state_ref[0, 0] = mt[:HEAD_DIM, :HEAD_DIM]
        state_ref[0, 1] = mt[HEAD_DIM:, HEAD_DIM:]


def _rwkv_prompt(proj, mu_pad, hparams, w2p, a2p, *, n_heads, rw0, lora0, tb):
    B, T, _ = proj.shape
    n_hp = n_heads // 2
    col = lambda off: (lambda b, h, t: (b, t, off + h))
    colc = lambda off: (lambda b, h, t: (b, t, off))
    mcol = lambda off: (lambda b, h, t: (0, off + h))
    mcolc = lambda off: (lambda b, h, t: (0, off))
    pblk = lambda im: pl.BlockSpec((1, tb, LANES), im)
    mblk = lambda im: pl.BlockSpec((1, LANES), im)
    in_specs = [pblk(col(rw0)), pblk(col(rw0 + n_hp)), pblk(col(rw0 + 2 * n_hp)), pblk(col(rw0 + 3 * n_hp)),
                pblk(colc(lora0)), pblk(colc(lora0 + 1)),
                mblk(mcol(0)), mblk(mcol(n_hp)), mblk(mcol(2 * n_hp)), mblk(mcol(3 * n_hp)),
                mblk(mcolc(4 * n_hp)), mblk(mcolc(4 * n_hp + 1)),
                pl.BlockSpec((8, LANES), lambda b, h, t: (0, h)),
                pl.BlockSpec((LANES, LANES), lambda b, h, t: (0, h)),
                pl.BlockSpec((LANES, LANES), lambda b, h, t: (0, h))]
    out_specs = [pl.BlockSpec((1, tb, LANES), lambda b, h, t: (b, t, h)),
                 pl.BlockSpec((1, 2, HEAD_DIM, HEAD_DIM), lambda b, h, t: (b, h, 0, 0))]
    scratch = [pltpu.VMEM((6, 1, LANES), F32), pltpu.VMEM((LANES, LANES), F32)] + [pltpu.VMEM((tb, LANES), F32)] * 7
    return pl.pallas_call(
        functools.partial(_rwkv_prompt_kernel, tb=tb),
        grid=(B, n_hp, T // tb),
        in_specs=in_specs, out_specs=out_specs,
        out_shape=[jax.ShapeDtypeStruct((B, T, n_heads * HEAD_DIM), BF16),
                   jax.ShapeDtypeStruct((B, n_heads, HEAD_DIM, HEAD_DIM), F32)],
        scratch_shapes=scratch,
        compiler_params=_params(("parallel", "parallel", "arbitrary")),
        name="rwkv_prompt",
    )(proj, proj, proj, proj, proj, proj, mu_pad, mu_pad, mu_pad, mu_pad, mu_pad, mu_pad, hparams, w2p, a2p)


def _rwkv_params(mu, w0, w2, a0, a2, k_k, k_a, r_k, lnx_w, lnx_b):
    width = w0.shape[0]
    lora = w2.shape[0]
    zpad = jnp.zeros((LANES - lora,), F32)
    mu_pad = jnp.concatenate([mu[:4 * width], mu[4 * width:4 * width + lora], zpad, mu[4 * width + lora:], zpad])[None, :]
    hparams = jnp.stack([w0, a0, k_k, k_a, r_k.reshape(-1), lnx_w, lnx_b, jnp.zeros_like(w0)]).astype(F32)
    pad_rows = lambda m: jnp.concatenate([m, jnp.zeros((LANES - lora, width), m.dtype)], axis=0).astype(BF16)
    return mu_pad.astype(F32), hparams, pad_rows(w2), pad_rows(a2)


def _rwkv_sample_prep_kernel(pr_ref, pk_ref, pv_ref, pg_ref, pwd_ref, pad_ref,
                             qr_ref, qk_ref, qv_ref, qg_ref, qwd_ref, qad_ref,
                             mr_ref, mk_ref, mv_ref, mg_ref, mwd_ref, mad_ref,
                             hp_ref, w2_ref, a2_ref,
                             r_out, k_out, v_out, g_out, kk_out, b_out, ld_out):
    group_ones = _group_ones()
    gsum = lambda x: _dot(x, group_ones, HI)

    def shifted(p_ref, q_ref, mu_ref):
        p = p_ref[...]
        return p + (q_ref[...] - p) * mu_ref[...]

    outs = _rwkv_prepare(
        shifted(pr_ref, qr_ref, mr_ref), shifted(pk_ref, qk_ref, mk_ref), shifted(pv_ref, qv_ref, mv_ref),
        shifted(pg_ref, qg_ref, mg_ref), shifted(pwd_ref, qwd_ref, mwd_ref), shifted(pad_ref, qad_ref, mad_ref),
        hp_ref[...], w2_ref[...], a2_ref[...], gsum)
    for o_ref, val in zip((r_out, k_out, v_out, g_out, kk_out, b_out, ld_out), outs):
        o_ref[...] = val


def _rwkv_sample_prep(proj, prev, mu_pad, hparams, w2p, a2p, *, n_heads, rw0, lora0):
    n = proj.shape[0]
    n_hp = n_heads // 2
    blk = lambda off, fixed=False: pl.BlockSpec((n, LANES), (lambda h: (0, off)) if fixed else (lambda h: (0, off + h)))
    mblk = lambda off, fixed=False: pl.BlockSpec((1, LANES), (lambda h: (0, off)) if fixed else (lambda h: (0, off + h)))
    offs = [0, n_hp, 2 * n_hp, 3 * n_hp]
    in_specs = ([blk(rw0 + o) for o in offs] + [blk(lora0, True), blk(lora0 + 1, True)]
                + [blk(o) for o in offs] + [blk(4 * n_hp, True), blk(4 * n_hp + 1, True)]
                + [mblk(o) for o in offs] + [mblk(4 * n_hp, True), mblk(4 * n_hp + 1, True)]
                + [pl.BlockSpec((8, LANES), lambda h: (0, h)),
                   pl.BlockSpec((LANES, LANES), lambda h: (0, h)),
                   pl.BlockSpec((LANES, LANES), lambda h: (0, h))])
    width = n_heads * HEAD_DIM
    return pl.pallas_call(
        _rwkv_sample_prep_kernel,
        grid=(n_hp,),
        in_specs=in_specs,
        out_specs=[pl.BlockSpec((n, LANES), lambda h: (0, h))] * 7,
        out_shape=[jax.ShapeDtypeStruct((n, width), F32)] * 7,
        compiler_params=_params(("parallel",)),
        name="rwkv_sample_prep",
    )(*([proj] * 6 + [prev] * 6 + [mu_pad] * 6 + [hparams, w2p, a2p]))


def _rwkv_sample_kernel(r_ref, k_ref, v_ref, g_ref, kk_ref, b_ref, ld_ref, s_ref, hp_ref, out_ref, so_ref, *, steps):
    shape = (HEAD_DIM, LANES)
    lane = _iota2(shape, 1)
    sub = _iota2(shape, 0)
    first = lane < HEAD_DIM
    m0 = first.astype(F32)
    m1 = 1.0 - m0
    diag = (lane & (HEAD_DIM - 1)) == sub

    def hsum(x):
        s0 = jnp.sum(x * m0, axis=1, keepdims=True)
        s1 = jnp.sum(x * m1, axis=1, keepdims=True)
        return jnp.where(first, s0, s1)

    s2 = jnp.concatenate([s_ref[0, 0], s_ref[0, 1]], axis=1)
    ys = []
    for t in range(steps):
        row = lambda ref: ref[0, t:t + 1, :]
        r, k, v, kk, b = row(r_ref), row(k_ref), row(v_ref), row(kk_ref), row(b_ref)
        decay = jnp.exp(row(ld_ref))
        s_kk = hsum(s2 * kk)
        v_col = hsum(jnp.where(diag, v, 0.0))
        s2 = s2 * decay - s_kk * b + v_col * k
        y_col = hsum(s2 * r)
        ys.append(jnp.sum(jnp.where(diag, y_col, 0.0), axis=0, keepdims=True))
    y = jnp.concatenate(ys, axis=0)

    rm0, rm1 = _head_masks((steps, LANES))

    def gsum(x):
        s0 = jnp.sum(x * rm0, axis=1, keepdims=True)
        s1 = jnp.sum(x * rm1, axis=1, keepdims=True)
        return jnp.where(rm0 > 0.5, s0, s1)

    out_ref[0] = _rwkv_finish(y, r_ref[0], k_ref[0], v_ref[0], g_ref[0], hp_ref[...], gsum)
    so_ref[0, 0] = s2[:, :HEAD_DIM]
    so_ref[0, 1] = s2[:, HEAD_DIM:]


def _rwkv_sample(prepped, state, hparams, *, n_heads):
    db, steps, width = prepped[0].shape
    n_hp = n_heads // 2
    tok = pl.BlockSpec((1, steps, LANES), lambda b, h: (b, 0, h))
    st = pl.BlockSpec((1, 2, HEAD_DIM, HEAD_DIM), lambda b, h: (b, h, 0, 0))
    return pl.pallas_call(
        functools.partial(_rwkv_sample_kernel, steps=steps),
        grid=(db, n_hp),
        in_specs=[tok] * 7 + [st, pl.BlockSpec((8, LANES), lambda b, h: (0, h))],
        out_specs=[tok, st],
        out_shape=[jax.ShapeDtypeStruct((db, steps, width), F32),
                   jax.ShapeDtypeStruct(state.shape, F32)],
        compiler_params=_params(("parallel", "parallel")),
        name="rwkv_sample",
    )(*prepped, state, hparams)


def _diff_lambda(lam_ref, lam_init):
    lp = lam_ref[...]
    s1 = jnp.sum(lp[0:1] * lp[1:2], axis=1, keepdims=True)
    s2 = jnp.sum(lp[2:3] * lp[3:4], axis=1, keepdims=True)
    return jnp.exp(s1) - jnp.exp(s2) + lam_init


def _diff_finish(o1, o2, lam, subw, ag, lam_init):
    o = o1 - lam * o2
    o = o * lax.rsqrt(jnp.mean(o * o, axis=-1, keepdims=True) + NORM_EPS)
    return o * subw * (1.0 - lam_init) * _silu(ag)


def _stack_maps(q):
    m0, m1 = _head_masks(q.shape)
    return jnp.concatenate([q * m0, q * m1], axis=0)


def _attn_prompt_kernel(q_ref, k_ref, v_ref, ag_ref, lam_ref, subw_ref, o_ref, m_s, l_s, acc_s, *, tq, lam_init):
    qi = pl.program_id(2)
    ki = pl.program_id(3)

    @pl.when(ki == 0)
    def _():
        m_s[...] = jnp.full_like(m_s, NEG_INF)
        l_s[...] = jnp.zeros_like(l_s)
        acc_s[...] = jnp.zeros_like(acc_s)

    @pl.when(ki <= qi)
    def _():
        q2 = _stack_maps(q_ref[0] * ATTN_SCALE).astype(BF16)
        s = _dot_nt(q2, k_ref[0].astype(BF16))
        q_pos = qi * tq + (_iota2(s.shape, 0) & (tq - 1))
        k_pos = ki * tq + _iota2(s.shape, 1)
        s = jnp.where(k_pos <= q_pos, s, NEG_INF)
        m_old = m_s[...]
        m_new = jnp.maximum(m_old, jnp.max(s, axis=1, keepdims=True))
        alpha = jnp.exp(m_old - m_new)
        p = jnp.exp(s - m_new)
        l_s[...] = alpha * l_s[...] + jnp.sum(p, axis=1, keepdims=True)
        acc_s[...] = alpha * acc_s[...] + _dot(p.astype(BF16), v_ref[0].astype(BF16))
        m_s[...] = m_new

    @pl.when(ki == qi)
    def _():
        o = acc_s[...] / l_s[...]
        lam = _diff_lambda(lam_ref, lam_init)
        o_ref[0] = _diff_finish(o[:tq], o[tq:], lam, subw_ref[...], ag_ref[0], lam_init).astype(o_ref.dtype)


def _attn_prompt(proj, lam_params, subw, *, n_heads, at0, tq, lam_init):
    B, T, _ = proj.shape
    nq = T // tq
    qblk = lambda off: pl.BlockSpec((1, tq, LANES), lambda b, h, i, j: (b, i, off + h))
    kblk = lambda off: pl.BlockSpec((1, tq, LANES), lambda b, h, i, j: (b, jnp.minimum(i, j), off + h))
    return pl.pallas_call(
        functools.partial(_attn_prompt_kernel, tq=tq, lam_init=lam_init),
        grid=(B, n_heads, nq, nq),
        in_specs=[qblk(at0), kblk(at0 + n_heads), kblk(at0 + 2 * n_heads), qblk(at0 + 3 * n_heads),
                  pl.BlockSpec(lam_params.shape, lambda b, h, i, j: (0, 0)),
                  pl.BlockSpec((1, LANES), lambda b, h, i, j: (0, 0))],
        out_specs=pl.BlockSpec((1, tq, LANES), lambda b, h, i, j: (b, i, h)),
        out_shape=jax.ShapeDtypeStruct((B, T, n_heads * LANES), BF16),
        scratch_shapes=[pltpu.VMEM((2 * tq, 1), F32), pltpu.VMEM((2 * tq, 1), F32), pltpu.VMEM((2 * tq, LANES), F32)],
        compiler_params=_params(("parallel", "parallel", "parallel", "arbitrary")),
        name="attn_prompt",
    )(proj, proj, proj, proj, lam_params, subw)


def _attn_sample_kernel(pt_ref, q_ref, kn_ref, vn_ref, ag_ref, kc_ref, vc_ref, lam_ref, subw_ref, o_ref,
                        m_s, l_s, acc_s, *, n_heads, steps, lam_init):
    del pt_ref
    pi = pl.program_id(1)
    n_pages = pl.num_programs(1)

    @pl.when(pi == 0)
    def _():
        m_s[...] = jnp.full_like(m_s, NEG_INF)
        l_s[...] = jnp.zeros_like(l_s)
        acc_s[...] = jnp.zeros_like(acc_s)

    q_all = q_ref[0] * ATTN_SCALE
    hs = lambda h: slice(h * LANES, (h + 1) * LANES)

    for h in range(n_heads):
        q2 = _stack_maps(q_all[:, hs(h)])
        s = _dot_nt(q2.astype(BF16), kc_ref[0, :, hs(h)].astype(BF16))
        m_old = m_s[h]
        m_new = jnp.maximum(m_old, jnp.max(s, axis=1, keepdims=True))
        alpha = jnp.exp(m_old - m_new)
        p = jnp.exp(s - m_new)
        l_s[h] = alpha * l_s[h] + jnp.sum(p, axis=1, keepdims=True)
        acc_s[h] = alpha * acc_s[h] + _dot(p.astype(BF16), vc_ref[0, :, hs(h)].astype(BF16))
        m_s[h] = m_new

    @pl.when(pi == n_pages - 1)
    def _():
        lam = _diff_lambda(lam_ref, lam_init)
        rows = 2 * steps
        tok = _iota2((rows, LANES), 0) & (steps - 1)
        lane = _iota2((rows, LANES), 1)
        for h in range(n_heads):
            q2 = _stack_maps(q_all[:, hs(h)])
            kn = kn_ref[0][:, hs(h)]
            vn = vn_ref[0][:, hs(h)]
            s = jnp.full((rows, LANES), NEG_INF, F32)
            for j in range(steps):
                sj = jnp.sum(q2 * kn[j:j + 1, :], axis=1, keepdims=True)
                s = jnp.where((lane == j) & (tok >= j), sj, s)
            m_old = m_s[h]
            m_new = jnp.maximum(m_old, jnp.max(s, axis=1, keepdims=True))
            alpha = jnp.exp(m_old - m_new)
            p = jnp.exp(s - m_new)
            l_fin = alpha * l_s[h] + jnp.sum(p, axis=1, keepdims=True)
            acc = alpha * acc_s[h]
            for j in range(steps):
                acc = acc + p[:, j:j + 1] * vn[j:j + 1, :]
            o = acc / l_fin
            o_ref[0, :, hs(h)] = _diff_finish(o[:steps], o[steps:], lam, subw_ref[...], ag_ref[0][:, hs(h)], lam_init)


def _attn_sample(proj, cache_k, cache_v, page_table, lam_params, subw, *, n_heads, at0, lam_init):
    db, steps, _ = proj.shape
    n_pages = page_table.shape[0] // db
    page = cache_k.shape[1]
    width = n_heads * LANES
    blk0 = at0 * LANES // width
    tokblk = lambda off: pl.BlockSpec((1, steps, width), lambda b, p, pt: (b, 0, blk0 + off))
    pageblk = pl.BlockSpec((1, page, width), lambda b, p, pt: (pt[b * n_pages + p], 0, 0))
    grid_spec = pltpu.PrefetchScalarGridSpec(
        num_scalar_prefetch=1,
        grid=(db, n_pages),
        in_specs=[tokblk(0), tokblk(1), tokblk(2), tokblk(3), pageblk, pageblk,
                  pl.BlockSpec(lam_params.shape, lambda b, p, pt: (0, 0)),
                  pl.BlockSpec((1, LANES), lambda b, p, pt: (0, 0))],
        out_specs=pl.BlockSpec((1, steps, width), lambda b, p, pt: (b, 0, 0)),
        scratch_shapes=[pltpu.VMEM((n_heads, 2 * steps, 1), F32), pltpu.VMEM((n_heads, 2 * steps, 1), F32),
                        pltpu.VMEM((n_heads, 2 * steps, LANES), F32)])
    return pl.pallas_call(
        functools.partial(_attn_sample_kernel, n_heads=n_heads, steps=steps, lam_init=lam_init),
        grid_spec=grid_spec,
        out_shape=jax.ShapeDtypeStruct((db, steps, width), F32),
        compiler_params=_params(("parallel", "arbitrary")),
        name="attn_sample",
    )(page_table, proj, proj, proj, proj, cache_k, cache_v, lam_params, subw)


def _outproj_kernel(x_ref, rw_ref, at_ref, grw_ref, gat_ref, wbr_ref, wba_ref, wo_ref, nf_ref, o_ref):
    merged = (_sigmoid(grw_ref[...]) * _dot(rw_ref[...], wbr_ref[...])
              + _sigmoid(gat_ref[...]) * _dot(at_ref[...], wba_ref[...]))
    y = x_ref[...] + _dot(merged.astype(BF16), wo_ref[...])
    o_ref[...] = y * lax.rsqrt(jnp.mean(y * y, axis=-1, keepdims=True) + NORM_EPS) * nf_ref[...]


def _outproj(x, rw, at, proj, wbr, wba, wo, nf, *, tm):
    n, d = x.shape
    row = lambda w: pl.BlockSpec((tm, w), lambda i: (i, 0))
    const = lambda a: pl.BlockSpec(a.shape, lambda i: (0, 0), pipeline_mode=pl.Buffered(1))
    return pl.pallas_call(
        _outproj_kernel,
        grid=(n // tm,),
        in_specs=[row(d), row(rw.shape[1]), row(at.shape[1]), row(d),
                  pl.BlockSpec((tm, d), lambda i: (i, 1)),
                  const(wbr), const(wba), const(wo), const(nf)],
        out_specs=row(d),
        out_shape=jax.ShapeDtypeStruct((n, d), F32),
        compiler_params=_params(("parallel",)),
        name="outproj",
    )(x, rw, at, proj, proj, wbr, wba, wo, nf)


def _largest_tile(n, cap):
    t = cap
    while n % t:
        t //= 2
    return t


def kernel(x_prompt, x_sample, cache_k, cache_v, state_wkv, state_shift, page_table, norm_in, w_in, mu_shift, w0, w2, a0, a2, k_k, k_a, r_k, lnx_w, lnx_b, lambda_q1, lambda_k1, lambda_q2, lambda_k2, subln_w, w_br_rwkv, w_br_attn, w_out, norm_f):
    depth = w_in.shape[0]
    assert depth == 1, "single-layer step"
    B, T, D = x_prompt.shape
    DB, TS, _ = x_sample.shape
    r_heads = r_k.shape[1]
    r_width = r_heads * HEAD_DIM
    lora = w2.shape[1]
    a_heads = cache_k.shape[3]
    a_width = a_heads * LANES
    rw_in = 4 * r_width + 2 * lora
    lam_init = 0.8 - 0.6 * math.exp(-0.3 * 0)

    w = w_in[0]
    zcol = jnp.zeros((D, LANES - lora), w.dtype)
    w_pad = jnp.concatenate([w[:, rw_in + 4 * a_width:], w[:, :4 * r_width], w[:, rw_in:rw_in + 4 * a_width],
                             w[:, 4 * r_width:4 * r_width + lora], zcol, w[:, 4 * r_width + lora:rw_in], zcol],
                            axis=1).astype(BF16)
    rw0 = 2 * D // LANES
    at0 = rw0 + 4 * r_width // LANES
    lora0 = at0 + 4 * a_width // LANES
    pw = w_pad.shape[1]

    mu_pad, hparams, w2p, a2p = _rwkv_params(mu_shift[0], w0[0], w2[0], a0[0], a2[0], k_k[0], k_a[0], r_k[0],
                                             lnx_w[0], lnx_b[0])
    lam_params = jnp.stack([lambda_q1[0], lambda_k1[0], lambda_q2[0], lambda_k2[0]]).astype(F32)
    subw = subln_w[0][None, :].astype(F32)
    g_in = norm_in[0][None, :].astype(F32)
    wbr = w_br_rwkv[0].astype(BF16)
    wba = w_br_attn[0].astype(BF16)
    wo = w_out[0].astype(BF16)
    nf = norm_f[None, :].astype(F32)
    tn = 7 * LANES

    n_p = B * T
    xp = x_prompt.reshape(n_p, D)
    proj_p = _inproj(xp, g_in, w_pad, tm=_largest_tile(n_p, 1024), tn=tn)
    proj_p3 = proj_p.reshape(B, T, pw)
    rw_p, wkv_p = _rwkv_prompt(proj_p3, mu_pad, hparams, w2p, a2p, n_heads=r_heads, rw0=rw0, lora0=lora0,
                               tb=_largest_tile(T, 512))
    at_p = _attn_prompt(proj_p3, lam_params, subw, n_heads=a_heads, at0=at0, tq=_largest_tile(T, 512),
                        lam_init=lam_init)
    y_p = _outproj(xp, rw_p.reshape(n_p, r_width), at_p.reshape(n_p, a_width), proj_p, wbr, wba, wo, nf,
                   tm=_largest_tile(n_p, 256))

    n_s = DB * TS
    xs = x_sample.reshape(n_s, D)
    proj_s = _inproj(xs, g_in, w_pad, tm=_largest_tile(n_s, 512), tn=tn)
    proj_s3 = proj_s.reshape(DB, TS, pw)
    c0 = rw0 * LANES
    zs = jnp.zeros((DB, 1, LANES - lora), F32)
    sh = state_shift[0][:, None, :]
    shift_pad = jnp.concatenate([sh[..., :4 * r_width], sh[..., 4 * r_width:4 * r_width + lora], zs,
                                 sh[..., 4 * r_width + lora:], zs], axis=-1)
    rwkv_cols = jnp.concatenate([proj_s3[:, :, c0:c0 + 4 * r_width], proj_s3[:, :, lora0 * LANES:]], axis=-1)
    prev = jnp.concatenate([shift_pad, rwkv_cols[:, :-1]], axis=1).reshape(n_s, -1)
    prepped = _rwkv_sample_prep(proj_s, prev, mu_pad, hparams, w2p, a2p, n_heads=r_heads, rw0=rw0, lora0=lora0)
    prepped = [a.reshape(DB, TS, r_width) for a in prepped]
    rw_s, wkv_s = _rwkv_sample(prepped, state_wkv[0], hparams, n_heads=r_heads)
    n_pool, page = cache_k.shape[1], cache_k.shape[2]
    at_s = _attn_sample(proj_s3, cache_k[0].reshape(n_pool, page, a_width), cache_v[0].reshape(n_pool, page, a_width),
                        page_table.reshape(-1), lam_params, subw, n_heads=a_heads, at0=at0, lam_init=lam_init)
    y_s = _outproj(xs, rw_s.reshape(n_s, r_width).astype(BF16), at_s.reshape(n_s, a_width).astype(BF16), proj_s,
                   wbr, wba, wo, nf, tm=_largest_tile(n_s, 256))

    k0 = (at0 + a_heads) * LANES
    v0 = (at0 + 2 * a_heads) * LANES

    def shift_row(p3):
        last = p3[:, -1]
        return jnp.concatenate([last[:, c0:c0 + 4 * r_width], last[:, lora0 * LANES:lora0 * LANES + lora],
                                last[:, (lora0 + 1) * LANES:(lora0 + 1) * LANES + lora]], axis=-1)[None]

    return (y_p.reshape(B, T, D), y_s.reshape(DB, TS, D),
            proj_p3[:, :, k0:k0 + a_width].reshape(1, B, T, a_heads, 2, HEAD_DIM),
            proj_p3[:, :, v0:v0 + a_width].reshape(1, B, T, a_heads, LANES),
            proj_s3[:, :, k0:k0 + a_width].reshape(1, DB, TS, a_heads, 2, HEAD_DIM),
            proj_s3[:, :, v0:v0 + a_width].reshape(1, DB, TS, a_heads, LANES),
            wkv_p[None], wkv_s[None], shift_row(proj_p3), shift_row(proj_s3))
```

```python
import functools
import math

import jax
import jax.numpy as jnp
from jax import lax
from jax.experimental import pallas as pl
from jax.experimental.pallas import tpu as pltpu

F32 = jnp.float32
BF16 = jnp.bfloat16

LANES = 128
HEAD_DIM = 64
CHUNK = 64
NORM_EPS = 1e-6
RWKV_GN_EPS = 64e-5
NEG_INF = -1e30
ATTN_SCALE = HEAD_DIM ** -0.5
VMEM_LIMIT = 56 * 1024 * 1024

HI = lax.Precision.HIGHEST


def _dot(a, b, prec=None):
    return lax.dot_general(a, b, (((1,), (0,)), ((), ())), precision=prec, preferred_element_type=F32)


def _dot_nt(a, b, prec=None):
    return lax.dot_general(a, b, (((1,), (1,)), ((), ())), precision=prec, preferred_element_type=F32)


def _dot_tn(a, b, prec=None):
    return lax.dot_general(a, b, (((0,), (0,)), ((), ())), precision=prec, preferred_element_type=F32)


def _iota2(shape, dim):
    return lax.broadcasted_iota(jnp.int32, shape, dim)


def _sigmoid(x):
    return 1.0 / (1.0 + jnp.exp(-x))


def _silu(x):
    return x * _sigmoid(x)


def _softplus(x):
    return jnp.maximum(x, 0.0) + jnp.log(1.0 + jnp.exp(-jnp.abs(x)))


def _params(semantics):
    return pltpu.CompilerParams(dimension_semantics=semantics, vmem_limit_bytes=VMEM_LIMIT)


def _inproj_kernel(x_ref, g_ref, w_ref, o_ref, xn_ref):
    @pl.when(pl.program_id(1) == 0)
    def _():
        x = x_ref[...]
        xn = x * lax.rsqrt(jnp.mean(x * x, axis=-1, keepdims=True) + NORM_EPS) * g_ref[...]
        xn_ref[...] = xn.astype(BF16)

    o_ref[...] = _dot(xn_ref[...], w_ref[...])


def _inproj(x, g, w, *, tm, tn):
    n, d = x.shape
    pw = w.shape[1]
    return pl.pallas_call(
        _inproj_kernel,
        grid=(n // tm, pw // tn),
        in_specs=[pl.BlockSpec((tm, d), lambda i, j: (i, 0)),
                  pl.BlockSpec((1, d), lambda i, j: (0, 0)),
                  pl.BlockSpec((d, tn), lambda i, j: (0, j))],
        out_specs=pl.BlockSpec((tm, tn), lambda i, j: (i, j)),
        out_shape=jax.ShapeDtypeStruct((n, pw), F32),
        scratch_shapes=[pltpu.VMEM((tm, d), BF16)],
        compiler_params=_params(("parallel", "arbitrary")),
        name="inproj",
    )(x, g, w)


def _rwkv_prepare(ps_r, ps_k, ps_v, ps_g, ps_wd, ps_ad, hp, w2, a2, gsum):
    w0, a0, k_k, k_a = hp[0:1], hp[1:2], hp[2:3], hp[3:4]
    w_pre = w0 + _dot(jnp.tanh(ps_wd).astype(BF16), w2)
    w = -_softplus(-w_pre) - 0.5
    log_decay = -jnp.exp(w)
    a = _sigmoid(a0 + _dot(ps_ad.astype(BF16), a2))
    kk = ps_k * k_k
    kk = kk / jnp.maximum(jnp.sqrt(gsum(kk * kk)), 1e-12)
    k = ps_k * (1.0 + (a - 1.0) * k_a)
    return ps_r, k, ps_v, ps_g, kk, kk * a, log_decay


def _rwkv_finish(y, r, k, v, g, hp, gsum):
    r_k, lnx_w, lnx_b = hp[4:5], hp[5:6], hp[6:7]
    mean = gsum(y) * (1.0 / HEAD_DIM)
    d = y - mean
    var = gsum(d * d) * (1.0 / HEAD_DIM)
    yn = d * lax.rsqrt(var + RWKV_GN_EPS) * lnx_w + lnx_b
    bonus = gsum(r * k * r_k) * v
    return (yn + bonus) * _silu(g)


def _head_masks(shape):
    lane = _iota2(shape, 1)
    m0 = (lane < HEAD_DIM).astype(F32)
    return m0, 1.0 - m0


def _group_ones():
    i = _iota2((LANES, LANES), 0) // HEAD_DIM
    j = _iota2((LANES, LANES), 1) // HEAD_DIM
    return (i == j).astype(BF16)


def _gsum_wide(x):
    ones = _group_ones()
    hi = x.astype(BF16)
    lo = (x - hi.astype(F32)).astype(BF16)
    parts = []
    for j in range(x.shape[1] // LANES):
        ls = slice(j * LANES, (j + 1) * LANES)
        parts.append(_dot(hi[:, ls], ones) + _dot(lo[:, ls], ones))
    return parts[0] if len(parts) == 1 else jnp.concatenate(parts, axis=1)


def _rwkv_prompt_kernel(pr_ref, pk_ref, pv_ref, pg_ref, pwd_ref, pad_ref,
                        mr_ref, mk_ref, mv_ref, mg_ref, mwd_ref, mad_ref,
                        hp_ref, w2_ref, a2_ref,
                        out_ref, state_ref,
                        carry_ref, cw_ref, m_ref, r_s, k_s, v_s, g_s, kk_s, b_s, ld_s, *, tb, ns):
    t_idx = pl.program_id(2)
    n_t = pl.num_programs(2)
    C = CHUNK

    @pl.when(t_idx == 0)
    def _():
        carry_ref[...] = jnp.zeros_like(carry_ref)
        cw_ref[...] = jnp.zeros_like(cw_ref)
        m_ref[...] = jnp.zeros_like(m_ref)

    def shifted(p_ref, mu_ref, c_ref, idx):
        p = p_ref[0]
        row = _iota2(p.shape, 0)
        prev = pltpu.roll(p, 1, axis=0)
        prev = jnp.where(row == 0, c_ref[idx], prev)
        c_ref[idx] = p[tb - 1:tb, :]
        return p + (prev - p) * mu_ref[...]

    hp = hp_ref[...]
    r, k, v, g, kk, b, ld = _rwkv_prepare(
        shifted(pr_ref, mr_ref, carry_ref, 0), shifted(pk_ref, mk_ref, carry_ref, 1),
        shifted(pv_ref, mv_ref, carry_ref, 2), shifted(pg_ref, mg_ref, carry_ref, 3),
        shifted(pwd_ref, mwd_ref, cw_ref, 0), shifted(pad_ref, mad_ref, cw_ref, 1),
        hp, w2_ref[...], a2_ref[...], _gsum_wide)
    r_s[...] = r
    k_s[...] = k
    v_s[...] = v
    g_s[...] = g
    kk_s[...] = kk
    b_s[...] = b
    ld_s[...] = ld

    m0, m1 = _head_masks((C, LANES))
    ii = _iota2((2 * C, 2 * C), 0)
    jj = _iota2((2 * C, 2 * C), 1)
    strict = ii > jj
    incl = ii >= jj
    eye = ii == jj
    tri = (_iota2((C, C), 0) >= _iota2((C, C), 1)).astype(F32)
    stack2 = lambda x: jnp.concatenate([x * m0, x * m1], axis=0).astype(BF16)
    bf = lambda x: x.astype(BF16)

    lanes = [slice(s * LANES, (s + 1) * LANES) for s in range(ns)]
    each = lambda f, *cols: [f(*args) for args in zip(*cols)]

    def chunk(ci, ms):
        sl = pl.ds(pl.multiple_of(ci * C, C), C)
        ld = [ld_s[sl, ls] for ls in lanes]
        c = each(lambda x: _dot(tri, x, HI), ld)
        c_end = each(lambda x: x[C - 1:C, :], c)
        p_inv = each(lambda x: jnp.exp(-x), c)
        p_end = each(lambda e, x: jnp.exp(e - x), c_end, c)
        r2 = each(lambda ls, x: stack2(r_s[sl, ls] * jnp.exp(x)), lanes, c)
        kk2 = each(lambda ls, x, d: stack2(kk_s[sl, ls] * jnp.exp(x - d)), lanes, c, ld)
        kh2 = each(lambda ls, p: stack2(k_s[sl, ls] * p), lanes, p_inv)
        bh2 = each(lambda ls, p: stack2(b_s[sl, ls] * p), lanes, p_inv)
        ke2 = each(lambda ls, p: stack2(k_s[sl, ls] * p), lanes, p_end)
        be2 = each(lambda ls, p: stack2(b_s[sl, ls] * p), lanes, p_end)
        v2 = each(lambda ls: stack2(v_s[sl, ls]), lanes)
        a = each(lambda w, x, y, z: _dot_nt(jnp.concatenate([w, x], axis=0), jnp.concatenate([y, z], axis=0)),
                 kk2, r2, kh2, bh2)
        l_k = each(lambda x: bf(jnp.where(strict, x[:2 * C, :2 * C], 0.0)), a)
        l_b = each(lambda x: bf(jnp.where(strict, x[:2 * C, 2 * C:], 0.0)), a)
        a_rk = each(lambda x: bf(jnp.where(incl, x[2 * C:, :2 * C], 0.0)), a)
        a_rb = each(lambda x: bf(jnp.where(incl, -x[2 * C:, 2 * C:], 0.0)), a)
        mb = each(bf, ms)
        x = each(lambda w, lk, m, v: _dot(jnp.concatenate([w, lk], axis=1), jnp.concatenate([m, v], axis=0)),
                 kk2, l_k, mb, v2)
        pw = l_b
        u = each(lambda x_, p: x_ - _dot(p, bf(x_)), x, pw)
        for _ in range(5):
            pw = each(lambda p: bf(_dot(p, p)), pw)
            u = each(lambda u_, p: u_ + _dot(p, bf(u_)), u, pw)
        ub = each(bf, u)
        y2 = each(lambda r_, ak, ab, m, v, u_: _dot(jnp.concatenate([r_, ak, ab], axis=1),
                                                     jnp.concatenate([m, v, u_], axis=0)),
                  r2, a_rk, a_rb, mb, v2, ub)
        p_c_col = each(lambda e: jnp.sum(jnp.where(eye, jnp.broadcast_to(jnp.exp(e), (2 * C, LANES)), 0.0),
                                         axis=1, keepdims=True), c_end)
        m_new = each(lambda m, pc, ke, be, v, u_: m * pc + _dot_tn(jnp.concatenate([ke, -be], axis=0),
                                                                   jnp.concatenate([v, u_], axis=0)),
                     ms, p_c_col, ke2, be2, v2, ub)
        for ls, y in zip(lanes, y2):
            out = _rwkv_finish(y[:C] + y[C:], r_s[sl, ls], k_s[sl, ls], v_s[sl, ls], g_s[sl, ls], hp[:, ls],
                               _gsum_wide)
            out_ref[0, sl, ls] = out.astype(out_ref.dtype)
        return tuple(m_new)

    ms = lax.fori_loop(0, tb // C, chunk, tuple(m_ref[s] for s in range(ns)))
    for s in range(ns):
        m_ref[s] = ms[s]

    @pl.when(t_idx == n_t - 1)
    def _():
        for s in range(ns):
            mt = ms[s].T
            state_ref[0, 2 * s] = mt[:HEAD_DIM, :HEAD_DIM]
            state_ref[0, 2 * s + 1] = mt[HEAD_DIM:, HEAD_DIM:]


def _rwkv_prompt(proj, mu_pad, hparams, w2p, a2p, *, n_heads, rw0, lora0, tb, ns):
    B, T, _ = proj.shape
    n_hp = n_heads // 2
    assert rw0 % ns == 0 and n_hp % ns == 0
    n_grp = n_hp // ns
    wide = ns * LANES
    col = lambda off: (lambda b, h, t: (b, t, off // ns + h))
    colc = lambda off: (lambda b, h, t: (b, t, off))
    mcol = lambda off: (lambda b, h, t: (0, off // ns + h))
    mcolc = lambda off: (lambda b, h, t: (0, off))
    pblk = lambda im: pl.BlockSpec((1, tb, wide), im)
    mblk = lambda im: pl.BlockSpec((1, wide), im)
    in_specs = [pblk(col(rw0)), pblk(col(rw0 + n_hp)), pblk(col(rw0 + 2 * n_hp)), pblk(col(rw0 + 3 * n_hp)),
                pl.BlockSpec((1, tb, LANES), colc(lora0)), pl.BlockSpec((1, tb, LANES), colc(lora0 + 1)),
                mblk(mcol(0)), mblk(mcol(n_hp)), mblk(mcol(2 * n_hp)), mblk(mcol(3 * n_hp)),
                pl.BlockSpec((1, LANES), mcolc(4 * n_hp)), pl.BlockSpec((1, LANES), mcolc(4 * n_hp + 1)),
                pl.BlockSpec((8, wide), lambda b, h, t: (0, h)),
                pl.BlockSpec((LANES, wide), lambda b, h, t: (0, h)),
                pl.BlockSpec((LANES, wide), lambda b, h, t: (0, h))]
    out_specs = [pl.BlockSpec((1, tb, wide), lambda b, h, t: (b, t, h)),
                 pl.BlockSpec((1, 2 * ns, HEAD_DIM, HEAD_DIM), lambda b, h, t: (b, h, 0, 0))]
    scratch = ([pltpu.VMEM((4, 1, wide), F32), pltpu.VMEM((2, 1, LANES), F32), pltpu.VMEM((ns, LANES, LANES), F32)]
               + [pltpu.VMEM((tb, wide), F32)] * 7)
    return pl.pallas_call(
        functools.partial(_rwkv_prompt_kernel, tb=tb, ns=ns),
        grid=(B, n_grp, T // tb),
        in_specs=in_specs, out_specs=out_specs,
        out_shape=[jax.ShapeDtypeStruct((B, T, n_heads * HEAD_DIM), BF16),
                   jax.ShapeDtypeStruct((B, n_heads, HEAD_DIM, HEAD_DIM), F32)],
        scratch_shapes=scratch,
        compiler_params=_params(("parallel", "parallel", "arbitrary")),
        name="rwkv_prompt",
    )(proj, proj, proj, proj, proj, proj, mu_pad, mu_pad, mu_pad, mu_pad, mu_pad, mu_pad, hparams, w2p, a2p)


def _rwkv_params(mu, w0, w2, a0, a2, k_k, k_a, r_k, lnx_w, lnx_b):
    width = w0.shape[0]
    lora = w2.shape[0]
    zpad = jnp.zeros((LANES - lora,), F32)
    mu_pad = jnp.concatenate([mu[:4 * width], mu[4 * width:4 * width + lora], zpad, mu[4 * width + lora:], zpad])[None, :]
    hparams = jnp.stack([w0, a0, k_k, k_a, r_k.reshape(-1), lnx_w, lnx_b, jnp.zeros_like(w0)]).astype(F32)
    pad_rows = lambda m: jnp.concatenate([m, jnp.zeros((LANES - lora, width), m.dtype)], axis=0).astype(BF16)
    return mu_pad.astype(F32), hparams, pad_rows(w2), pad_rows(a2)


def _rwkv_sample_prep_kernel(pr_ref, pk_ref, pv_ref, pg_ref, pwd_ref, pad_ref,
                             qr_ref, qk_ref, qv_ref, qg_ref, qwd_ref, qad_ref,
                             mr_ref, mk_ref, mv_ref, mg_ref, mwd_ref, mad_ref,
                             hp_ref, w2_ref, a2_ref,
                             r_out, k_out, v_out, g_out, kk_out, b_out, ld_out):
    def shifted(p_ref, q_ref, mu_ref):
        p = p_ref[...]
        return p + (q_ref[...] - p) * mu_ref[...]

    outs = _rwkv_prepare(
        shifted(pr_ref, qr_ref, mr_ref), shifted(pk_ref, qk_ref, mk_ref), shifted(pv_ref, qv_ref, mv_ref),
        shifted(pg_ref, qg_ref, mg_ref), shifted(pwd_ref, qwd_ref, mwd_ref), shifted(pad_ref, qad_ref, mad_ref),
        hp_ref[...], w2_ref[...], a2_ref[...], _gsum_wide)
    for o_ref, val in zip((r_out, k_out, v_out, g_out, kk_out, b_out, ld_out), outs):
        o_ref[...] = val


def _rwkv_sample_prep(proj, prev, mu_pad, hparams, w2p, a2p, *, n_heads, rw0, lora0):
    n = proj.shape[0]
    n_hp = n_heads // 2
    blk = lambda off, fixed=False: pl.BlockSpec((n, LANES), (lambda h: (0, off)) if fixed else (lambda h: (0, off + h)))
    mblk = lambda off, fixed=False: pl.BlockSpec((1, LANES), (lambda h: (0, off)) if fixed else (lambda h: (0, off + h)))
    offs = [0, n_hp, 2 * n_hp, 3 * n_hp]
    in_specs = ([blk(rw0 + o) for o in offs] + [blk(lora0, True), blk(lora0 + 1, True)]
                + [blk(o) for o in offs] + [blk(4 * n_hp, True), blk(4 * n_hp + 1, True)]
                + [mblk(o) for o in offs] + [mblk(4 * n_hp, True), mblk(4 * n_hp + 1, True)]
                + [pl.BlockSpec((8, LANES), lambda h: (0, h)),
                   pl.BlockSpec((LANES, LANES), lambda h: (0, h)),
                   pl.BlockSpec((LANES, LANES), lambda h: (0, h))])
    width = n_heads * HEAD_DIM
    return pl.pallas_call(
        _rwkv_sample_prep_kernel,
        grid=(n_hp,),
        in_specs=in_specs,
        out_specs=[pl.BlockSpec((n, LANES), lambda h: (0, h))] * 7,
        out_shape=[jax.ShapeDtypeStruct((n, width), F32)] * 7,
        compiler_params=_params(("parallel",)),
        name="rwkv_sample_prep",
    )(*([proj] * 6 + [prev] * 6 + [mu_pad] * 6 + [hparams, w2p, a2p]))


def _rwkv_sample_kernel(r_ref, k_ref, v_ref, g_ref, kk_ref, b_ref, ld_ref, s_ref, hp_ref, out_ref, so_ref, *,
                        steps, n_pairs):
    shape = (HEAD_DIM, LANES)
    lane = _iota2(shape, 1)
    sub = _iota2(shape, 0)
    first = lane < HEAD_DIM
    m0 = first.astype(F32)
    m1 = 1.0 - m0
    diag = (lane & (HEAD_DIM - 1)) == sub
    rm0, rm1 = _head_masks((steps, LANES))

    def hsum(x):
        s0 = jnp.sum(x * m0, axis=1, keepdims=True)
        s1 = jnp.sum(x * m1, axis=1, keepdims=True)
        return jnp.where(first, s0, s1)

    def gsum(x):
        s0 = jnp.sum(x * rm0, axis=1, keepdims=True)
        s1 = jnp.sum(x * rm1, axis=1, keepdims=True)
        return jnp.where(rm0 > 0.5, s0, s1)

    for j in range(n_pairs):
        ls = slice(j * LANES, (j + 1) * LANES)
        s2 = jnp.concatenate([s_ref[0, 2 * j], s_ref[0, 2 * j + 1]], axis=1)
        ys = []
        for t in range(steps):
            row = lambda ref: ref[0, t:t + 1, ls]
            r, k, v, kk, b = row(r_ref), row(k_ref), row(v_ref), row(kk_ref), row(b_ref)
            decay = jnp.exp(row(ld_ref))
            s_kk = hsum(s2 * kk)
            v_col = hsum(jnp.where(diag, v, 0.0))
            s2 = s2 * decay - s_kk * b + v_col * k
            y_col = hsum(s2 * r)
            ys.append(jnp.sum(jnp.where(diag, y_col, 0.0), axis=0, keepdims=True))
        y = jnp.concatenate(ys, axis=0)
        out_ref[0, :, ls] = _rwkv_finish(y, r_ref[0, :, ls], k_ref[0, :, ls], v_ref[0, :, ls], g_ref[0, :, ls],
                                         hp_ref[:, ls], gsum)
        so_ref[0, 2 * j] = s2[:, :HEAD_DIM]
        so_ref[0, 2 * j + 1] = s2[:, HEAD_DIM:]


def _rwkv_sample(prepped, state, hparams, *, n_heads):
    db, steps, width = prepped[0].shape
    tok = pl.BlockSpec((1, steps, width), lambda b: (b, 0, 0))
    st = pl.BlockSpec((1, n_heads, HEAD_DIM, HEAD_DIM), lambda b: (b, 0, 0, 0))
    return pl.pallas_call(
        functools.partial(_rwkv_sample_kernel, steps=steps, n_pairs=n_heads // 2),
        grid=(db,),
        in_specs=[tok] * 7 + [st, pl.BlockSpec((8, width), lambda b: (0, 0))],
        out_specs=[tok, st],
        out_shape=[jax.ShapeDtypeStruct((db, steps, width), F32),
                   jax.ShapeDtypeStruct(state.shape, F32)],
        compiler_params=_params(("parallel",)),
        name="rwkv_sample",
    )(*prepped, state, hparams)


def _diff_lambda(lam_ref, lam_init):
    lp = lam_ref[...]
    s1 = jnp.sum(lp[0:1] * lp[1:2], axis=1, keepdims=True)
    s2 = jnp.sum(lp[2:3] * lp[3:4], axis=1, keepdims=True)
    return jnp.exp(s1) - jnp.exp(s2) + lam_init


def _diff_finish(o1, o2, lam, subw, ag, lam_init):
    o = o1 - lam * o2
    o = o * lax.rsqrt(jnp.mean(o * o, axis=-1, keepdims=True) + NORM_EPS)
    return o * subw * (1.0 - lam_init) * _silu(ag)


def _stack_maps(q):
    m0, m1 = _head_masks(q.shape)
    return jnp.concatenate([q * m0, q * m1], axis=0)


def _attn_prompt_kernel(q_ref, k_ref, v_ref, ag_ref, lam_ref, subw_ref, o_ref, m_s, l_s, acc_s, *, tq, lam_init):
    qi = pl.program_id(2)
    ki = pl.program_id(3)

    @pl.when(ki == 0)
    def _():
        m_s[...] = jnp.full_like(m_s, NEG_INF)
        l_s[...] = jnp.zeros_like(l_s)
        acc_s[...] = jnp.zeros_like(acc_s)

    def accumulate(masked):
        q2 = _stack_maps(q_ref[0] * ATTN_SCALE).astype(BF16)
        s = _dot_nt(q2, k_ref[0].astype(BF16))
        if masked:
            q_pos = _iota2(s.shape, 0) & (tq - 1)
            s = jnp.where(_iota2(s.shape, 1) <= q_pos, s, NEG_INF)
        m_old = m_s[...]
        m_new = jnp.maximum(m_old, jnp.max(s, axis=1, keepdims=True))
        alpha = jnp.exp(m_old - m_new)
        p = jnp.exp(s - m_new)
        l_s[...] = alpha * l_s[...] + jnp.sum(p, axis=1, keepdims=True)
        acc_s[...] = alpha * acc_s[...] + _dot(p.astype(BF16), v_ref[0].astype(BF16))
        m_s[...] = m_new

    @pl.when(ki < qi)
    def _():
        accumulate(False)

    @pl.when(ki == qi)
    def _():
        accumulate(True)
        o = acc_s[...] / l_s[...]
        lam = _diff_lambda(lam_ref, lam_init)
        o_ref[0] = _diff_finish(o[:tq], o[tq:], lam, subw_ref[...], ag_ref[0], lam_init).astype(o_ref.dtype)


def _attn_prompt(proj, lam_params, subw, *, n_heads, at0, tq, lam_init):
    B, T, _ = proj.shape
    nq = T // tq
    qblk = lambda off: pl.BlockSpec((1, tq, LANES), lambda b, h, i, j: (b, i, off + h))
    kblk = lambda off: pl.BlockSpec((1, tq, LANES), lambda b, h, i, j: (b, jnp.minimum(i, j), off + h))
    return pl.pallas_call(
        functools.partial(_attn_prompt_kernel, tq=tq, lam_init=lam_init),
        grid=(B, n_heads, nq, nq),
        in_specs=[qblk(at0), kblk(at0 + n_heads), kblk(at0 + 2 * n_heads), qblk(at0 + 3 * n_heads),
                  pl.BlockSpec(lam_params.shape, lambda b, h, i, j: (0, 0)),
                  pl.BlockSpec((1, LANES), lambda b, h, i, j: (0, 0))],
        out_specs=pl.BlockSpec((1, tq, LANES), lambda b, h, i, j: (b, i, h)),
        out_shape=jax.ShapeDtypeStruct((B, T, n_heads * LANES), BF16),
        scratch_shapes=[pltpu.VMEM((2 * tq, 1), F32), pltpu.VMEM((2 * tq, 1), F32), pltpu.VMEM((2 * tq, LANES), F32)],
        compiler_params=_params(("parallel", "parallel", "parallel", "arbitrary")),
        name="attn_prompt",
    )(proj, proj, proj, proj, lam_params, subw)


def _attn_sample_kernel(pt_ref, q_ref, kn_ref, vn_ref, ag_ref, *refs, n_heads, steps, pages, lam_init):
    del pt_ref
    k_refs, v_refs = refs[:pages], refs[pages:2 * pages]
    lam_ref, subw_ref, o_ref, m_s, l_s, acc_s = refs[2 * pages:]
    gi = pl.program_id(1)
    n_groups = pl.num_programs(1)
    rows_h = 2 * steps
    rows = n_heads * rows_h
    flat = k_refs[0].shape[1]
    assert n_heads & (n_heads - 1) == 0 and rows_h & (rows_h - 1) == 0

    @pl.when(gi == 0)
    def _():
        m_s[...] = jnp.full_like(m_s, NEG_INF)
        l_s[...] = jnp.zeros_like(l_s)
        acc_s[...] = jnp.zeros_like(acc_s)

    q_all = q_ref[0] * ATTN_SCALE
    hs = lambda h: slice(h * LANES, (h + 1) * LANES)
    q_blk = jnp.concatenate([_stack_maps(q_all[:, hs(h)]) for h in range(n_heads)], axis=0)
    qb = q_blk.astype(BF16)
    row_head = lax.shift_right_logical(_iota2((rows, flat), 0), int(math.log2(rows_h)))
    own = (_iota2((rows, flat), 1) & (n_heads - 1)) == row_head

    m_run, l_run, acc = m_s[...], l_s[...], acc_s[...]
    for i in range(pages):
        s = jnp.where(own, _dot_nt(qb, k_refs[i][0].astype(BF16)), NEG_INF)
        m_new = jnp.maximum(m_run, jnp.max(s, axis=1, keepdims=True))
        alpha = jnp.exp(m_run - m_new)
        p = jnp.exp(s - m_new)
        l_run = alpha * l_run + jnp.sum(p, axis=1, keepdims=True)
        acc = alpha * acc + _dot(p.astype(BF16), v_refs[i][0].astype(BF16))
        m_run = m_new
    m_s[...] = m_run
    l_s[...] = l_run
    acc_s[...] = acc

    @pl.when(gi == n_groups - 1)
    def _():
        lam = _diff_lambda(lam_ref, lam_init)
        tok = _iota2((rows_h, LANES), 0) & (steps - 1)
        lane = _iota2((rows_h, LANES), 1)
        for h in range(n_heads):
            rs = slice(h * rows_h, (h + 1) * rows_h)
            q2 = q_blk[rs]
            kn = kn_ref[0][:, hs(h)]
            vn = vn_ref[0][:, hs(h)]
            s = jnp.full((rows_h, LANES), NEG_INF, F32)
            for j in range(steps):
                sj = jnp.sum(q2 * kn[j:j + 1, :], axis=1, keepdims=True)
                s = jnp.where((lane == j) & (tok >= j), sj, s)
            m_old = m_run[rs]
            m_new = jnp.maximum(m_old, jnp.max(s, axis=1, keepdims=True))
            alpha = jnp.exp(m_old - m_new)
            p = jnp.exp(s - m_new)
            l_fin = alpha * l_run[rs] + jnp.sum(p, axis=1, keepdims=True)
            a_fin = alpha * acc[rs]
            for j in range(steps):
                a_fin = a_fin + p[:, j:j + 1] * vn[j:j + 1, :]
            o = a_fin / l_fin
            o_ref[0, :, hs(h)] = _diff_finish(o[:steps], o[steps:], lam, subw_ref[...], ag_ref[0][:, hs(h)], lam_init)


def _attn_sample(proj, cache_k, cache_v, page_table, lam_params, subw, *, n_heads, at0, pages, lam_init):
    db, steps, _ = proj.shape
    n_pages = page_table.shape[0] // db
    n_groups = n_pages // pages
    flat = cache_k.shape[1]
    width = n_heads * LANES
    blk0 = at0 * LANES // width
    rows = n_heads * 2 * steps
    tokblk = lambda off: pl.BlockSpec((1, steps, width), lambda b, g, pt: (b, 0, blk0 + off))
    pageblk = lambda i: pl.BlockSpec((1, flat, LANES), lambda b, g, pt: (pt[(b * n_groups + g) * pages + i], 0, 0))
    grid_spec = pltpu.PrefetchScalarGridSpec(
        num_scalar_prefetch=1,
        grid=(db, n_groups),
        in_specs=([tokblk(0), tokblk(1), tokblk(2), tokblk(3)] + [pageblk(i) for i in range(pages)] * 2
                  + [pl.BlockSpec(lam_params.shape, lambda b, g, pt: (0, 0)),
                     pl.BlockSpec((1, LANES), lambda b, g, pt: (0, 0))]),
        out_specs=pl.BlockSpec((1, steps, width), lambda b, g, pt: (b, 0, 0)),
        scratch_shapes=[pltpu.VMEM((rows, 1), F32), pltpu.VMEM((rows, 1), F32), pltpu.VMEM((rows, LANES), F32)])
    return pl.pallas_call(
        functools.partial(_attn_sample_kernel, n_heads=n_heads, steps=steps, pages=pages, lam_init=lam_init),
        grid_spec=grid_spec,
        out_shape=jax.ShapeDtypeStruct((db, steps, width), F32),
        compiler_params=_params(("parallel", "arbitrary")),
        name="attn_sample",
    )(page_table, proj, proj, proj, proj, *([cache_k] * pages), *([cache_v] * pages), lam_params, subw)


def _outproj_kernel(x_ref, rw_ref, at_ref, grw_ref, gat_ref, wbr_ref, wba_ref, wo_ref, nf_ref, o_ref):
    merged = (_sigmoid(grw_ref[...]) * _dot(rw_ref[...], wbr_ref[...])
              + _sigmoid(gat_ref[...]) * _dot(at_ref[...], wba_ref[...]))
    y = x_ref[...] + _dot(merged.astype(BF16), wo_ref[...])
    o_ref[...] = y * lax.rsqrt(jnp.mean(y * y, axis=-1, keepdims=True) + NORM_EPS) * nf_ref[...]


def _outproj(x, rw, at, proj, wbr, wba, wo, nf, *, tm):
    n, d = x.shape
    row = lambda w: pl.BlockSpec((tm, w), lambda i: (i, 0))
    const = lambda a: pl.BlockSpec(a.shape, lambda i: (0, 0), pipeline_mode=pl.Buffered(1))
    return pl.pallas_call(
        _outproj_kernel,
        grid=(n // tm,),
        in_specs=[row(d), row(rw.shape[1]), row(at.shape[1]), row(d),
                  pl.BlockSpec((tm, d), lambda i: (i, 1)),
                  const(wbr), const(wba), const(wo), const(nf)],
        out_specs=row(d),
        out_shape=jax.ShapeDtypeStruct((n, d), F32),
        compiler_params=_params(("parallel",)),
        name="outproj",
    )(x, rw, at, proj, proj, wbr, wba, wo, nf)


def _largest_tile(n, cap):
    t = cap
    while n % t:
        t //= 2
    return t


def kernel(x_prompt, x_sample, cache_k, cache_v, state_wkv, state_shift, page_table, norm_in, w_in, mu_shift, w0, w2, a0, a2, k_k, k_a, r_k, lnx_w, lnx_b, lambda_q1, lambda_k1, lambda_q2, lambda_k2, subln_w, w_br_rwkv, w_br_attn, w_out, norm_f):
    depth = w_in.shape[0]
    assert depth == 1, "single-layer step"
    B, T, D = x_prompt.shape
    DB, TS, _ = x_sample.shape
    r_heads = r_k.shape[1]
    r_width = r_heads * HEAD_DIM
    lora = w2.shape[1]
    a_heads = cache_k.shape[3]
    a_width = a_heads * LANES
    rw_in = 4 * r_width + 2 * lora
    lam_init = 0.8 - 0.6 * math.exp(-0.3 * 0)

    w = w_in[0]
    zcol = jnp.zeros((D, LANES - lora), w.dtype)
    w_pad = jnp.concatenate([w[:, rw_in + 4 * a_width:], w[:, :4 * r_width], w[:, rw_in:rw_in + 4 * a_width],
                             w[:, 4 * r_width:4 * r_width + lora], zcol, w[:, 4 * r_width + lora:rw_in], zcol],
                            axis=1).astype(BF16)
    rw0 = 2 * D // LANES
    at0 = rw0 + 4 * r_width // LANES
    lora0 = at0 + 4 * a_width // LANES
    pw = w_pad.shape[1]

    mu_pad, hparams, w2p, a2p = _rwkv_params(mu_shift[0], w0[0], w2[0], a0[0], a2[0], k_k[0], k_a[0], r_k[0],
                                             lnx_w[0], lnx_b[0])
    lam_params = jnp.stack([lambda_q1[0], lambda_k1[0], lambda_q2[0], lambda_k2[0]]).astype(F32)
    subw = subln_w[0][None, :].astype(F32)
    g_in = norm_in[0][None, :].astype(F32)
    wbr = w_br_rwkv[0].astype(BF16)
    wba = w_br_attn[0].astype(BF16)
    wo = w_out[0].astype(BF16)
    nf = norm_f[None, :].astype(F32)
    tn = 7 * LANES

    n_p = B * T
    xp = x_prompt.reshape(n_p, D)
    proj_p = _inproj(xp, g_in, w_pad, tm=_largest_tile(n_p, 1024), tn=tn)
    proj_p3 = proj_p.reshape(B, T, pw)
    rw_p, wkv_p = _rwkv_prompt(proj_p3, mu_pad, hparams, w2p, a2p, n_heads=r_heads, rw0=rw0, lora0=lora0,
                               tb=_largest_tile(T, 512), ns=_largest_tile(r_heads // 2, 8))
    at_p = _attn_prompt(proj_p3, lam_params, subw, n_heads=a_heads, at0=at0, tq=_largest_tile(T, 512),
                        lam_init=lam_init)
    y_p = _outproj(xp, rw_p.reshape(n_p, r_width), at_p.reshape(n_p, a_width), proj_p, wbr, wba, wo, nf,
                   tm=_largest_tile(n_p, 256))

    n_s = DB * TS
    xs = x_sample.reshape(n_s, D)
    proj_s = _inproj(xs, g_in, w_pad, tm=_largest_tile(n_s, 512), tn=tn)
    proj_s3 = proj_s.reshape(DB, TS, pw)
    c0 = rw0 * LANES
    zs = jnp.zeros((DB, 1, LANES - lora), F32)
    sh = state_shift[0][:, None, :]
    shift_pad = jnp.concatenate([sh[..., :4 * r_width], sh[..., 4 * r_width:4 * r_width + lora], zs,
                                 sh[..., 4 * r_width + lora:], zs], axis=-1)
    rwkv_cols = jnp.concatenate([proj_s3[:, :, c0:c0 + 4 * r_width], proj_s3[:, :, lora0 * LANES:]], axis=-1)
    prev = jnp.concatenate([shift_pad, rwkv_cols[:, :-1]], axis=1).reshape(n_s, -1)
    prepped = _rwkv_sample_prep(proj_s, prev, mu_pad, hparams, w2p, a2p, n_heads=r_heads, rw0=rw0, lora0=lora0)
    prepped = [a.reshape(DB, TS, r_width) for a in prepped]
    rw_s, wkv_s = _rwkv_sample(prepped, state_wkv[0], hparams, n_heads=r_heads)
    n_pool, page = cache_k.shape[1], cache_k.shape[2]
    n_pages = page_table.shape[1]
    at_s = _attn_sample(proj_s3, cache_k[0].reshape(n_pool, page * a_heads, LANES),
                        cache_v[0].reshape(n_pool, page * a_heads, LANES), page_table.reshape(-1), lam_params, subw,
                        n_heads=a_heads, at0=at0, pages=_largest_tile(n_pages, 8), lam_init=lam_init)
    y_s = _outproj(xs, rw_s.reshape(n_s, r_width).astype(BF16), at_s.reshape(n_s, a_width).astype(BF16), proj_s,
                   wbr, wba, wo, nf, tm=_largest_tile(n_s, 256))

    k0 = (at0 + a_heads) * LANES
    v0 = (at0 + 2 * a_heads) * LANES

    def shift_row(p3):
        last = p3[:, -1]
        return jnp.concatenate([last[:, c0:c0 + 4 * r_width], last[:, lora0 * LANES:lora0 * LANES + lora],
                                last[:, (lora0 + 1) * LANES:(lora0 + 1) * LANES + lora]], axis=-1)[None]

    return (y_p.reshape(B, T, D), y_s.reshape(DB, TS, D),
            proj_p3[:, :, k0:k0 + a_width].reshape(1, B, T, a_heads, 2, HEAD_DIM),
            proj_p3[:, :, v0:v0 + a_width].reshape(1, B, T, a_heads, LANES),
            proj_s3[:, :, k0:k0 + a_width].reshape(1, DB, TS, a_heads, 2, HEAD_DIM),
            proj_s3[:, :, v0:v0 + a_width].reshape(1, DB, TS, a_heads, LANES),
            wkv_p[None], wkv_s[None], shift_row(proj_p3), shift_row(proj_s3))
```

```python
import functools
import math

import jax
import jax.numpy as jnp
from jax import lax
from jax.experimental import pallas as pl
from jax.experimental.pallas import tpu as pltpu

F32 = jnp.float32
BF16 = jnp.bfloat16

LANES = 128
HEAD_DIM = 64
CHUNK = 64
NORM_EPS = 1e-6
RWKV_GN_EPS = 64e-5
NEG_INF = -1e30
ATTN_SCALE = HEAD_DIM ** -0.5
VMEM_LIMIT = 56 * 1024 * 1024

HI = lax.Precision.HIGHEST


def _dot(a, b, prec=None):
    return lax.dot_general(a, b, (((1,), (0,)), ((), ())), precision=prec, preferred_element_type=F32)


def _dot_nt(a, b, prec=None):
    return lax.dot_general(a, b, (((1,), (1,)), ((), ())), precision=prec, preferred_element_type=F32)


def _dot_tn(a, b, prec=None):
    return lax.dot_general(a, b, (((0,), (0,)), ((), ())), precision=prec, preferred_element_type=F32)


def _iota2(shape, dim):
    return lax.broadcasted_iota(jnp.int32, shape, dim)


def _sigmoid(x):
    return 1.0 / (1.0 + jnp.exp(-x))


def _silu(x):
    return x * _sigmoid(x)


def _softplus(x):
    return jnp.maximum(x, 0.0) + jnp.log(1.0 + jnp.exp(-jnp.abs(x)))


def _params(semantics):
    return pltpu.CompilerParams(dimension_semantics=semantics, vmem_limit_bytes=VMEM_LIMIT)


def _inproj_kernel(x_ref, g_ref, w_ref, o_ref, xn_ref):
    @pl.when(pl.program_id(1) == 0)
    def _():
        x = x_ref[...]
        xn = x * lax.rsqrt(jnp.mean(x * x, axis=-1, keepdims=True) + NORM_EPS) * g_ref[...]
        xn_ref[...] = xn.astype(BF16)

    o_ref[...] = _dot(xn_ref[...], w_ref[...])


def _inproj(x, g, w, *, tm, tn):
    n, d = x.shape
    pw = w.shape[1]
    return pl.pallas_call(
        _inproj_kernel,
        grid=(n // tm, pw // tn),
        in_specs=[pl.BlockSpec((tm, d), lambda i, j: (i, 0)),
                  pl.BlockSpec((1, d), lambda i, j: (0, 0)),
                  pl.BlockSpec((d, tn), lambda i, j: (0, j))],
        out_specs=pl.BlockSpec((tm, tn), lambda i, j: (i, j)),
        out_shape=jax.ShapeDtypeStruct((n, pw), F32),
        scratch_shapes=[pltpu.VMEM((tm, d), BF16)],
        compiler_params=_params(("parallel", "arbitrary")),
        name="inproj",
    )(x, g, w)


def _rwkv_prepare(ps_r, ps_k, ps_v, ps_g, ps_wd, ps_ad, hp, w2, a2, gsum):
    w0, a0, k_k, k_a = hp[0:1], hp[1:2], hp[2:3], hp[3:4]
    w_pre = w0 + _dot(jnp.tanh(ps_wd).astype(BF16), w2)
    w = -_softplus(-w_pre) - 0.5
    log_decay = -jnp.exp(w)
    a = _sigmoid(a0 + _dot(ps_ad.astype(BF16), a2))
    kk = ps_k * k_k
    kk = kk / jnp.maximum(jnp.sqrt(gsum(kk * kk)), 1e-12)
    k = ps_k * (1.0 + (a - 1.0) * k_a)
    return ps_r, k, ps_v, ps_g, kk, kk * a, log_decay


def _rwkv_finish(y, r, k, v, g, hp, gsum):
    r_k, lnx_w, lnx_b = hp[4:5], hp[5:6], hp[6:7]
    mean = gsum(y) * (1.0 / HEAD_DIM)
    d = y - mean
    var = gsum(d * d) * (1.0 / HEAD_DIM)
    yn = d * lax.rsqrt(var + RWKV_GN_EPS) * lnx_w + lnx_b
    bonus = gsum(r * k * r_k) * v
    return (yn + bonus) * _silu(g)


def _head_masks(shape):
    lane = _iota2(shape, 1)
    m0 = (lane < HEAD_DIM).astype(F32)
    return m0, 1.0 - m0


def _group_ones():
    i = _iota2((LANES, LANES), 0) // HEAD_DIM
    j = _iota2((LANES, LANES), 1) // HEAD_DIM
    return (i == j).astype(BF16)


def _gsum_wide(x):
    ones = _group_ones()
    hi = x.astype(BF16)
    lo = (x - hi.astype(F32)).astype(BF16)
    parts = []
    for j in range(x.shape[1] // LANES):
        ls = slice(j * LANES, (j + 1) * LANES)
        parts.append(_dot(hi[:, ls], ones) + _dot(lo[:, ls], ones))
    return parts[0] if len(parts) == 1 else jnp.concatenate(parts, axis=1)


def _rwkv_prompt_kernel(pr_ref, pk_ref, pv_ref, pg_ref, pwd_ref, pad_ref,
                        mr_ref, mk_ref, mv_ref, mg_ref, mwd_ref, mad_ref,
                        hp_ref, w2_ref, a2_ref,
                        out_ref, state_ref,
                        carry_ref, cw_ref, m_ref, r_s, k_s, v_s, g_s, kk_s, b_s, ld_s, *, tb, ns):
    t_idx = pl.program_id(2)
    n_t = pl.num_programs(2)
    C = CHUNK

    @pl.when(t_idx == 0)
    def _():
        carry_ref[...] = jnp.zeros_like(carry_ref)
        cw_ref[...] = jnp.zeros_like(cw_ref)
        m_ref[...] = jnp.zeros_like(m_ref)

    def shifted(p_ref, mu_ref, c_ref, idx):
        p = p_ref[0]
        row = _iota2(p.shape, 0)
        prev = pltpu.roll(p, 1, axis=0)
        prev = jnp.where(row == 0, c_ref[idx], prev)
        c_ref[idx] = p[tb - 1:tb, :]
        return p + (prev - p) * mu_ref[...]

    hp = hp_ref[...]
    r, k, v, g, kk, b, ld = _rwkv_prepare(
        shifted(pr_ref, mr_ref, carry_ref, 0), shifted(pk_ref, mk_ref, carry_ref, 1),
        shifted(pv_ref, mv_ref, carry_ref, 2), shifted(pg_ref, mg_ref, carry_ref, 3),
        shifted(pwd_ref, mwd_ref, cw_ref, 0), shifted(pad_ref, mad_ref, cw_ref, 1),
        hp, w2_ref[...], a2_ref[...], _gsum_wide)
    r_s[...] = r
    k_s[...] = k
    v_s[...] = v
    g_s[...] = g
    kk_s[...] = kk
    b_s[...] = b
    ld_s[...] = ld

    m0, m1 = _head_masks((C, LANES))
    ii = _iota2((2 * C, 2 * C), 0)
    jj = _iota2((2 * C, 2 * C), 1)
    strict = ii > jj
    incl = ii >= jj
    eye = ii == jj
    tri = (_iota2((C, C), 0) >= _iota2((C, C), 1)).astype(F32)
    stack2 = lambda x: jnp.concatenate([x * m0, x * m1], axis=0).astype(BF16)
    bf = lambda x: x.astype(BF16)

    lanes = [slice(s * LANES, (s + 1) * LANES) for s in range(ns)]
    each = lambda f, *cols: [f(*args) for args in zip(*cols)]

    def chunk(ci, ms):
        sl = pl.ds(pl.multiple_of(ci * C, C), C)
        ld = [ld_s[sl, ls] for ls in lanes]
        c = each(lambda x: _dot(tri, x, HI), ld)
        c_end = each(lambda x: x[C - 1:C, :], c)
        p_inv = each(lambda x: jnp.exp(-x), c)
        p_end = each(lambda e, x: jnp.exp(e - x), c_end, c)
        r2 = each(lambda ls, x: stack2(r_s[sl, ls] * jnp.exp(x)), lanes, c)
        kk2 = each(lambda ls, x, d: stack2(kk_s[sl, ls] * jnp.exp(x - d)), lanes, c, ld)
        kh2 = each(lambda ls, p: stack2(k_s[sl, ls] * p), lanes, p_inv)
        bh2 = each(lambda ls, p: stack2(b_s[sl, ls] * p), lanes, p_inv)
        ke2 = each(lambda ls, p: stack2(k_s[sl, ls] * p), lanes, p_end)
        be2 = each(lambda ls, p: stack2(b_s[sl, ls] * p), lanes, p_end)
        v2 = each(lambda ls: stack2(v_s[sl, ls]), lanes)
        a = each(lambda w, x, y, z: _dot_nt(jnp.concatenate([w, x], axis=0), jnp.concatenate([y, z], axis=0)),
                 kk2, r2, kh2, bh2)
        l_k = each(lambda x: bf(jnp.where(strict, x[:2 * C, :2 * C], 0.0)), a)
        l_b = each(lambda x: bf(jnp.where(strict, x[:2 * C, 2 * C:], 0.0)), a)
        a_rk = each(lambda x: bf(jnp.where(incl, x[2 * C:, :2 * C], 0.0)), a)
        a_rb = each(lambda x: bf(jnp.where(incl, -x[2 * C:, 2 * C:], 0.0)), a)
        mb = each(bf, ms)
        x = each(lambda w, lk, m, v: _dot(jnp.concatenate([w, lk], axis=1), jnp.concatenate([m, v], axis=0)),
                 kk2, l_k, mb, v2)
        pw = l_b
        u = each(lambda x_, p: x_ - _dot(p, bf(x_)), x, pw)
        for _ in range(5):
            pw = each(lambda p: bf(_dot(p, p)), pw)
            u = each(lambda u_, p: u_ + _dot(p, bf(u_)), u, pw)
        ub = each(bf, u)
        y2 = each(lambda r_, ak, ab, m, v, u_: _dot(jnp.concatenate([r_, ak, ab], axis=1),
                                                     jnp.concatenate([m, v, u_], axis=0)),
                  r2, a_rk, a_rb, mb, v2, ub)
        p_c_col = each(lambda e: jnp.sum(jnp.where(eye, jnp.broadcast_to(jnp.exp(e), (2 * C, LANES)), 0.0),
                                         axis=1, keepdims=True), c_end)
        m_new = each(lambda m, pc, ke, be, v, u_: m * pc + _dot_tn(jnp.concatenate([ke, -be], axis=0),
                                                                   jnp.concatenate([v, u_], axis=0)),
                     ms, p_c_col, ke2, be2, v2, ub)
        for ls, y in zip(lanes, y2):
            out = _rwkv_finish(y[:C] + y[C:], r_s[sl, ls], k_s[sl, ls], v_s[sl, ls], g_s[sl, ls], hp[:, ls],
                               _gsum_wide)
            out_ref[0, sl, ls] = out.astype(out_ref.dtype)
        return tuple(m_new)

    ms = lax.fori_loop(0, tb // C, chunk, tuple(m_ref[s] for s in range(ns)))
    for s in range(ns):
        m_ref[s] = ms[s]

    @pl.when(t_idx == n_t - 1)
    def _():
        for s in range(ns):
            mt = ms[s].T
            state_ref[0, 2 * s] = mt[:HEAD_DIM, :HEAD_DIM]
            state_ref[0, 2 * s + 1] = mt[HEAD_DIM:, HEAD_DIM:]


def _rwkv_prompt(proj, mu_pad, hparams, w2p, a2p, *, n_heads, rw0, lora0, tb, ns):
    B, T, _ = proj.shape
    n_hp = n_heads // 2
    assert rw0 % ns == 0 and n_hp % ns == 0
    n_grp = n_hp // ns
    wide = ns * LANES
    col = lambda off: (lambda b, h, t: (b, t, off // ns + h))
    colc = lambda off: (lambda b, h, t: (b, t, off))
    mcol = lambda off: (lambda b, h, t: (0, off // ns + h))
    mcolc = lambda off: (lambda b, h, t: (0, off))
    pblk = lambda im: pl.BlockSpec((1, tb, wide), im)
    mblk = lambda im: pl.BlockSpec((1, wide), im)
    in_specs = [pblk(col(rw0)), pblk(col(rw0 + n_hp)), pblk(col(rw0 + 2 * n_hp)), pblk(col(rw0 + 3 * n_hp)),
                pl.BlockSpec((1, tb, LANES), colc(lora0)), pl.BlockSpec((1, tb, LANES), colc(lora0 + 1)),
                mblk(mcol(0)), mblk(mcol(n_hp)), mblk(mcol(2 * n_hp)), mblk(mcol(3 * n_hp)),
                pl.BlockSpec((1, LANES), mcolc(4 * n_hp)), pl.BlockSpec((1, LANES), mcolc(4 * n_hp + 1)),
                pl.BlockSpec((8, wide), lambda b, h, t: (0, h)),
                pl.BlockSpec((LANES, wide), lambda b, h, t: (0, h)),
                pl.BlockSpec((LANES, wide), lambda b, h, t: (0, h))]
    out_specs = [pl.BlockSpec((1, tb, wide), lambda b, h, t: (b, t, h)),
                 pl.BlockSpec((1, 2 * ns, HEAD_DIM, HEAD_DIM), lambda b, h, t: (b, h, 0, 0))]
    scratch = ([pltpu.VMEM((4, 1, wide), F32), pltpu.VMEM((2, 1, LANES), F32), pltpu.VMEM((ns, LANES, LANES), F32)]
               + [pltpu.VMEM((tb, wide), F32)] * 7)
    return pl.pallas_call(
        functools.partial(_rwkv_prompt_kernel, tb=tb, ns=ns),
        grid=(B, n_grp, T // tb),
        in_specs=in_specs, out_specs=out_specs,
        out_shape=[jax.ShapeDtypeStruct((B, T, n_heads * HEAD_DIM), BF16),
                   jax.ShapeDtypeStruct((B, n_heads, HEAD_DIM, HEAD_DIM), F32)],
        scratch_shapes=scratch,
        compiler_params=_params(("parallel", "parallel", "arbitrary")),
        name="rwkv_prompt",
    )(proj, proj, proj, proj, proj, proj, mu_pad, mu_pad, mu_pad, mu_pad, mu_pad, mu_pad, hparams, w2p, a2p)


def _rwkv_params(mu, w0, w2, a0, a2, k_k, k_a, r_k, lnx_w, lnx_b):
    width = w0.shape[0]
    lora = w2.shape[0]
    zpad = jnp.zeros((LANES - lora,), F32)
    mu_pad = jnp.concatenate([mu[:4 * width], mu[4 * width:4 * width + lora], zpad, mu[4 * width + lora:], zpad])[None, :]
    hparams = jnp.stack([w0, a0, k_k, k_a, r_k.reshape(-1), lnx_w, lnx_b, jnp.zeros_like(w0)]).astype(F32)
    pad_rows = lambda m: jnp.concatenate([m, jnp.zeros((LANES - lora, width), m.dtype)], axis=0).astype(BF16)
    return mu_pad.astype(F32), hparams, pad_rows(w2), pad_rows(a2)


def _rwkv_sample_prep_kernel(pr_ref, pk_ref, pv_ref, pg_ref, pwd_ref, pad_ref,
                             qr_ref, qk_ref, qv_ref, qg_ref, qwd_ref, qad_ref,
                             mr_ref, mk_ref, mv_ref, mg_ref, mwd_ref, mad_ref,
                             hp_ref, w2_ref, a2_ref,
                             r_out, k_out, v_out, g_out, kk_out, b_out, ld_out):
    def shifted(p_ref, q_ref, mu_ref):
        p = p_ref[...]
        return p + (q_ref[...] - p) * mu_ref[...]

    outs = _rwkv_prepare(
        shifted(pr_ref, qr_ref, mr_ref), shifted(pk_ref, qk_ref, mk_ref), shifted(pv_ref, qv_ref, mv_ref),
        shifted(pg_ref, qg_ref, mg_ref), shifted(pwd_ref, qwd_ref, mwd_ref), shifted(pad_ref, qad_ref, mad_ref),
        hp_ref[...], w2_ref[...], a2_ref[...], _gsum_wide)
    for o_ref, val in zip((r_out, k_out, v_out, g_out, kk_out, b_out, ld_out), outs):
        o_ref[...] = val


def _rwkv_sample_prep(proj, prev, mu_pad, hparams, w2p, a2p, *, n_heads, rw0, lora0):
    n = proj.shape[0]
    n_hp = n_heads // 2
    blk = lambda off, fixed=False: pl.BlockSpec((n, LANES), (lambda h: (0, off)) if fixed else (lambda h: (0, off + h)))
    mblk = lambda off, fixed=False: pl.BlockSpec((1, LANES), (lambda h: (0, off)) if fixed else (lambda h: (0, off + h)))
    offs = [0, n_hp, 2 * n_hp, 3 * n_hp]
    in_specs = ([blk(rw0 + o) for o in offs] + [blk(lora0, True), blk(lora0 + 1, True)]
                + [blk(o) for o in offs] + [blk(4 * n_hp, True), blk(4 * n_hp + 1, True)]
                + [mblk(o) for o in offs] + [mblk(4 * n_hp, True), mblk(4 * n_hp + 1, True)]
                + [pl.BlockSpec((8, LANES), lambda h: (0, h)),
                   pl.BlockSpec((LANES, LANES), lambda h: (0, h)),
                   pl.BlockSpec((LANES, LANES), lambda h: (0, h))])
    width = n_heads * HEAD_DIM
    return pl.pallas_call(
        _rwkv_sample_prep_kernel,
        grid=(n_hp,),
        in_specs=in_specs,
        out_specs=[pl.BlockSpec((n, LANES), lambda h: (0, h))] * 7,
        out_shape=[jax.ShapeDtypeStruct((n, width), F32)] * 7,
        compiler_params=_params(("parallel",)),
        name="rwkv_sample_prep",
    )(*([proj] * 6 + [prev] * 6 + [mu_pad] * 6 + [hparams, w2p, a2p]))


def _rwkv_sample_kernel(r_ref, k_ref, v_ref, g_ref, kk_ref, b_ref, ld_ref, s_ref, hp_ref, out_ref, so_ref, *,
                        steps, n_pairs):
    shape = (HEAD_DIM, LANES)
    lane = _iota2(shape, 1)
    sub = _iota2(shape, 0)
    first = lane < HEAD_DIM
    m0 = first.astype(F32)
    m1 = 1.0 - m0
    diag = (lane & (HEAD_DIM - 1)) == sub
    rm0, rm1 = _head_masks((steps, LANES))

    def hsum(x):
        s0 = jnp.sum(x * m0, axis=1, keepdims=True)
        s1 = jnp.sum(x * m1, axis=1, keepdims=True)
        return jnp.where(first, s0, s1)

    def gsum(x):
        s0 = jnp.sum(x * rm0, axis=1, keepdims=True)
        s1 = jnp.sum(x * rm1, axis=1, keepdims=True)
        return jnp.where(rm0 > 0.5, s0, s1)

    for j in range(n_pairs):
        ls = slice(j * LANES, (j + 1) * LANES)
        s2 = jnp.concatenate([s_ref[0, 2 * j], s_ref[0, 2 * j + 1]], axis=1)
        ys = []
        for t in range(steps):
            row = lambda ref: ref[0, t:t + 1, ls]
            r, k, v, kk, b = row(r_ref), row(k_ref), row(v_ref), row(kk_ref), row(b_ref)
            decay = jnp.exp(row(ld_ref))
            s_kk = hsum(s2 * kk)
            v_col = hsum(jnp.where(diag, v, 0.0))
            s2 = s2 * decay - s_kk * b + v_col * k
            y_col = hsum(s2 * r)
            ys.append(jnp.sum(jnp.where(diag, y_col, 0.0), axis=0, keepdims=True))
        y = jnp.concatenate(ys, axis=0)
        out_ref[0, :, ls] = _rwkv_finish(y, r_ref[0, :, ls], k_ref[0, :, ls], v_ref[0, :, ls], g_ref[0, :, ls],
                                         hp_ref[:, ls], gsum)
        so_ref[0, 2 * j] = s2[:, :HEAD_DIM]
        so_ref[0, 2 * j + 1] = s2[:, HEAD_DIM:]


def _rwkv_sample(prepped, state, hparams, *, n_heads):
    db, steps, width = prepped[0].shape
    tok = pl.BlockSpec((1, steps, width), lambda b: (b, 0, 0))
    st = pl.BlockSpec((1, n_heads, HEAD_DIM, HEAD_DIM), lambda b: (b, 0, 0, 0))
    return pl.pallas_call(
        functools.partial(_rwkv_sample_kernel, steps=steps, n_pairs=n_heads // 2),
        grid=(db,),
        in_specs=[tok] * 7 + [st, pl.BlockSpec((8, width), lambda b: (0, 0))],
        out_specs=[tok, st],
        out_shape=[jax.ShapeDtypeStruct((db, steps, width), F32),
                   jax.ShapeDtypeStruct(state.shape, F32)],
        compiler_params=_params(("parallel",)),
        name="rwkv_sample",
    )(*prepped, state, hparams)


def _diff_lambda(lam_ref, lam_init):
    lp = lam_ref[...]
    s1 = jnp.sum(lp[0:1] * lp[1:2], axis=1, keepdims=True)
    s2 = jnp.sum(lp[2:3] * lp[3:4], axis=1, keepdims=True)
    return jnp.exp(s1) - jnp.exp(s2) + lam_init


def _diff_finish(o1, o2, lam, subw, ag, lam_init):
    o = o1 - lam * o2
    o = o * lax.rsqrt(jnp.mean(o * o, axis=-1, keepdims=True) + NORM_EPS)
    return o * subw * (1.0 - lam_init) * _silu(ag)


def _stack_maps(q):
    m0, m1 = _head_masks(q.shape)
    return jnp.concatenate([q * m0, q * m1], axis=0)


def _attn_prompt_kernel(q_ref, k_ref, v_ref, ag_ref, lam_ref, subw_ref, o_ref, kb_s, vb_s, m_s, acc_s, *,
                        tq, sub, lam_init):
    qi = pl.program_id(2)
    seq = k_ref.shape[1]

    @pl.when(qi == 0)
    def _():
        kb_s[...] = k_ref[0].astype(BF16)
        vb_s[:, :LANES] = v_ref[0].astype(BF16)
        vb_s[:, LANES:] = jnp.ones((seq, LANES), BF16)

    q2 = _stack_maps(q_ref[0] * ATTN_SCALE).astype(BF16)
    m_s[...] = jnp.full_like(m_s, NEG_INF)
    acc_s[...] = jnp.zeros_like(acc_s)
    n_sub = 2 * tq // sub

    def block(k0, masked):
        kb = kb_s[pl.ds(k0, tq), :]
        vb = vb_s[pl.ds(k0, tq), :]

        def scores(j):
            s = _dot_nt(q2[j * sub:(j + 1) * sub], kb)
            if masked:
                q_pos = (_iota2(s.shape, 0) + j * sub) & (tq - 1)
                s = jnp.where(_iota2(s.shape, 1) <= q_pos, s, NEG_INF)
            return s

        s_next = scores(0)
        for j in range(n_sub):
            s = s_next
            if j + 1 < n_sub:
                s_next = scores(j + 1)
            rows = slice(j * sub, (j + 1) * sub)
            m_old = m_s[rows]
            m_new = jnp.maximum(m_old, jnp.max(s, axis=1, keepdims=True))
            alpha = jnp.exp(m_old - m_new)
            p = jnp.exp(s - m_new).astype(BF16)
            acc_s[rows] = alpha * acc_s[rows] + _dot(p, vb)
            m_s[rows] = m_new

    def body(ki, carry):
        block(pl.multiple_of(ki * tq, tq), False)
        return carry

    lax.fori_loop(0, qi, body, 0)
    block(pl.multiple_of(qi * tq, tq), True)
    acc = acc_s[...]
    o = acc[:, :LANES] / acc[:, LANES:]
    lam = _diff_lambda(lam_ref, lam_init)
    o_ref[0] = _diff_finish(o[:tq], o[tq:], lam, subw_ref[...], ag_ref[0], lam_init).astype(o_ref.dtype)


def _attn_prompt(proj, lam_params, subw, *, n_heads, at0, tq, lam_init):
    B, T, _ = proj.shape
    qblk = lambda off: pl.BlockSpec((1, tq, LANES), lambda b, h, i: (b, i, off + h))
    kblk = lambda off: pl.BlockSpec((1, T, LANES), lambda b, h, i: (b, 0, off + h))
    return pl.pallas_call(
        functools.partial(_attn_prompt_kernel, tq=tq, sub=min(256, 2 * tq), lam_init=lam_init),
        grid=(B, n_heads, T // tq),
        in_specs=[qblk(at0), kblk(at0 + n_heads), kblk(at0 + 2 * n_heads), qblk(at0 + 3 * n_heads),
                  pl.BlockSpec(lam_params.shape, lambda b, h, i: (0, 0)),
                  pl.BlockSpec((1, LANES), lambda b, h, i: (0, 0))],
        out_specs=pl.BlockSpec((1, tq, LANES), lambda b, h, i: (b, i, h)),
        out_shape=jax.ShapeDtypeStruct((B, T, n_heads * LANES), BF16),
        scratch_shapes=[pltpu.VMEM((T, LANES), BF16), pltpu.VMEM((T, 2 * LANES), BF16),
                        pltpu.VMEM((2 * tq, 1), F32), pltpu.VMEM((2 * tq, 2 * LANES), F32)],
        compiler_params=_params(("parallel", "parallel", "arbitrary")),
        name="attn_prompt",
    )(proj, proj, proj, proj, lam_params, subw)


def _attn_sample_kernel(pt_ref, q_ref, kn_ref, vn_ref, ag_ref, *refs, n_heads, steps, pages, lam_init):
    del pt_ref
    k_refs, v_refs = refs[:pages], refs[pages:2 * pages]
    lam_ref, subw_ref, o_ref, m_s, l_s, acc_s = refs[2 * pages:]
    gi = pl.program_id(1)
    n_groups = pl.num_programs(1)
    rows_h = 2 * steps
    rows = n_heads * rows_h
    kdim, page = k_refs[0].shape[1], k_refs[0].shape[2]
    assert steps & (steps - 1) == 0

    @pl.when(gi == 0)
    def _():
        m_s[...] = jnp.full_like(m_s, NEG_INF)
        l_s[...] = jnp.zeros_like(l_s)
        acc_s[...] = jnp.zeros_like(acc_s)

    q_all = q_ref[0] * ATTN_SCALE
    hs = lambda h: slice(h * LANES, (h + 1) * LANES)
    q_rep = jnp.concatenate([q_all] * (rows // steps), axis=0)
    col_grp = lax.shift_right_logical(_iota2((rows, kdim), 1), int(math.log2(HEAD_DIM)))
    row_grp = lax.shift_right_logical(_iota2((rows, kdim), 0), int(math.log2(steps)))
    q_bd = jnp.where(col_grp == row_grp, q_rep, 0.0).astype(BF16)

    s = jnp.concatenate([_dot(q_bd, k_refs[i][0].astype(BF16)) for i in range(pages)], axis=1)
    m_old = m_s[...]
    m_run = jnp.maximum(m_old, jnp.max(s, axis=1, keepdims=True))
    alpha = jnp.exp(m_old - m_run)
    p = jnp.exp(s - m_run)
    l_run = alpha * l_s[...] + jnp.sum(p, axis=1, keepdims=True)
    pv = []
    for h in range(n_heads):
        v_h = jnp.concatenate([v_refs[i][0, pl.ds(h, page, stride=n_heads), :].astype(BF16) for i in range(pages)],
                              axis=0)
        pv.append(_dot(p[h * rows_h:(h + 1) * rows_h].astype(BF16), v_h))
    acc = alpha * acc_s[...] + jnp.concatenate(pv, axis=0)
    m_s[...] = m_run
    l_s[...] = l_run
    acc_s[...] = acc

    @pl.when(gi == n_groups - 1)
    def _():
        lam = _diff_lambda(lam_ref, lam_init)
        tok = _iota2((rows_h, LANES), 0) & (steps - 1)
        lane = _iota2((rows_h, LANES), 1)
        for h in range(n_heads):
            rs = slice(h * rows_h, (h + 1) * rows_h)
            q2 = _stack_maps(q_all[:, hs(h)])
            kn = kn_ref[0][:, hs(h)]
            vn = vn_ref[0][:, hs(h)]
            s = jnp.full((rows_h, LANES), NEG_INF, F32)
            for j in range(steps):
                sj = jnp.sum(q2 * kn[j:j + 1, :], axis=1, keepdims=True)
                s = jnp.where((lane == j) & (tok >= j), sj, s)
            m_old = m_run[rs]
            m_new = jnp.maximum(m_old, jnp.max(s, axis=1, keepdims=True))
            alpha = jnp.exp(m_old - m_new)
            p = jnp.exp(s - m_new)
            l_fin = alpha * l_run[rs] + jnp.sum(p, axis=1, keepdims=True)
            a_fin = alpha * acc[rs]
            for j in range(steps):
                a_fin = a_fin + p[:, j:j + 1] * vn[j:j + 1, :]
            o = a_fin / l_fin
            o_ref[0, :, hs(h)] = _diff_finish(o[:steps], o[steps:], lam, subw_ref[...], ag_ref[0][:, hs(h)], lam_init)


def _attn_sample(proj, cache_k, cache_v, page_table, lam_params, subw, *, n_heads, at0, pages, lam_init):
    db, steps, _ = proj.shape
    n_pages = page_table.shape[0] // db
    n_groups = n_pages // pages
    width = n_heads * LANES
    blk0 = at0 * LANES // width
    rows = n_heads * 2 * steps
    tokblk = lambda off: pl.BlockSpec((1, steps, width), lambda b, g, pt: (b, 0, blk0 + off))
    pageblk = lambda cache, i: pl.BlockSpec((1,) + cache.shape[1:],
                                            lambda b, g, pt: (pt[(b * n_groups + g) * pages + i], 0, 0))
    grid_spec = pltpu.PrefetchScalarGridSpec(
        num_scalar_prefetch=1,
        grid=(db, n_groups),
        in_specs=([tokblk(0), tokblk(1), tokblk(2), tokblk(3)]
                  + [pageblk(cache_k, i) for i in range(pages)] + [pageblk(cache_v, i) for i in range(pages)]
                  + [pl.BlockSpec(lam_params.shape, lambda b, g, pt: (0, 0)),
                     pl.BlockSpec((1, LANES), lambda b, g, pt: (0, 0))]),
        out_specs=pl.BlockSpec((1, steps, width), lambda b, g, pt: (b, 0, 0)),
        scratch_shapes=[pltpu.VMEM((rows, 1), F32), pltpu.VMEM((rows, 1), F32), pltpu.VMEM((rows, LANES), F32)])
    return pl.pallas_call(
        functools.partial(_attn_sample_kernel, n_heads=n_heads, steps=steps, pages=pages, lam_init=lam_init),
        grid_spec=grid_spec,
        out_shape=jax.ShapeDtypeStruct((db, steps, width), F32),
        compiler_params=_params(("parallel", "arbitrary")),
        name="attn_sample",
    )(page_table, proj, proj, proj, proj, *([cache_k] * pages), *([cache_v] * pages), lam_params, subw)


def _outproj_kernel(x_ref, rw_ref, at_ref, grw_ref, gat_ref, wbr_ref, wba_ref, wo_ref, nf_ref, o_ref):
    merged = (_sigmoid(grw_ref[...]) * _dot(rw_ref[...], wbr_ref[...])
              + _sigmoid(gat_ref[...]) * _dot(at_ref[...], wba_ref[...]))
    y = x_ref[...] + _dot(merged.astype(BF16), wo_ref[...])
    o_ref[...] = y * lax.rsqrt(jnp.mean(y * y, axis=-1, keepdims=True) + NORM_EPS) * nf_ref[...]


def _outproj(x, rw, at, proj, wbr, wba, wo, nf, *, tm):
    n, d = x.shape
    row = lambda w: pl.BlockSpec((tm, w), lambda i: (i, 0))
    const = lambda a: pl.BlockSpec(a.shape, lambda i: (0, 0), pipeline_mode=pl.Buffered(1))
    return pl.pallas_call(
        _outproj_kernel,
        grid=(n // tm,),
        in_specs=[row(d), row(rw.shape[1]), row(at.shape[1]), row(d),
                  pl.BlockSpec((tm, d), lambda i: (i, 1)),
                  const(wbr), const(wba), const(wo), const(nf)],
        out_specs=row(d),
        out_shape=jax.ShapeDtypeStruct((n, d), F32),
        compiler_params=_params(("parallel",)),
        name="outproj",
    )(x, rw, at, proj, proj, wbr, wba, wo, nf)


def _largest_tile(n, cap):
    t = cap
    while n % t:
        t //= 2
    return t


def kernel(x_prompt, x_sample, cache_k, cache_v, state_wkv, state_shift, page_table, norm_in, w_in, mu_shift, w0, w2, a0, a2, k_k, k_a, r_k, lnx_w, lnx_b, lambda_q1, lambda_k1, lambda_q2, lambda_k2, subln_w, w_br_rwkv, w_br_attn, w_out, norm_f):
    depth = w_in.shape[0]
    assert depth == 1, "single-layer step"
    B, T, D = x_prompt.shape
    DB, TS, _ = x_sample.shape
    r_heads = r_k.shape[1]
    r_width = r_heads * HEAD_DIM
    lora = w2.shape[1]
    a_heads = cache_k.shape[3]
    a_width = a_heads * LANES
    rw_in = 4 * r_width + 2 * lora
    lam_init = 0.8 - 0.6 * math.exp(-0.3 * 0)

    w = w_in[0]
    zcol = jnp.zeros((D, LANES - lora), w.dtype)
    w_pad = jnp.concatenate([w[:, rw_in + 4 * a_width:], w[:, :4 * r_width], w[:, rw_in:rw_in + 4 * a_width],
                             w[:, 4 * r_width:4 * r_width + lora], zcol, w[:, 4 * r_width + lora:rw_in], zcol],
                            axis=1).astype(BF16)
    rw0 = 2 * D // LANES
    at0 = rw0 + 4 * r_width // LANES
    lora0 = at0 + 4 * a_width // LANES
    pw = w_pad.shape[1]

    mu_pad, hparams, w2p, a2p = _rwkv_params(mu_shift[0], w0[0], w2[0], a0[0], a2[0], k_k[0], k_a[0], r_k[0],
                                             lnx_w[0], lnx_b[0])
    lam_params = jnp.stack([lambda_q1[0], lambda_k1[0], lambda_q2[0], lambda_k2[0]]).astype(F32)
    subw = subln_w[0][None, :].astype(F32)
    g_in = norm_in[0][None, :].astype(F32)
    wbr = w_br_rwkv[0].astype(BF16)
    wba = w_br_attn[0].astype(BF16)
    wo = w_out[0].astype(BF16)
    nf = norm_f[None, :].astype(F32)
    tn = 7 * LANES

    n_p = B * T
    xp = x_prompt.reshape(n_p, D)
    proj_p = _inproj(xp, g_in, w_pad, tm=_largest_tile(n_p, 1024), tn=tn)
    proj_p3 = proj_p.reshape(B, T, pw)
    rw_p, wkv_p = _rwkv_prompt(proj_p3, mu_pad, hparams, w2p, a2p, n_heads=r_heads, rw0=rw0, lora0=lora0,
                               tb=_largest_tile(T, 512), ns=_largest_tile(r_heads // 2, 8))
    at_p = _attn_prompt(proj_p3, lam_params, subw, n_heads=a_heads, at0=at0, tq=_largest_tile(T, 512),
                        lam_init=lam_init)
    y_p = _outproj(xp, rw_p.reshape(n_p, r_width), at_p.reshape(n_p, a_width), proj_p, wbr, wba, wo, nf,
                   tm=_largest_tile(n_p, 256))

    n_s = DB * TS
    xs = x_sample.reshape(n_s, D)
    proj_s = _inproj(xs, g_in, w_pad, tm=_largest_tile(n_s, 512), tn=tn)
    proj_s3 = proj_s.reshape(DB, TS, pw)
    c0 = rw0 * LANES
    zs = jnp.zeros((DB, 1, LANES - lora), F32)
    sh = state_shift[0][:, None, :]
    shift_pad = jnp.concatenate([sh[..., :4 * r_width], sh[..., 4 * r_width:4 * r_width + lora], zs,
                                 sh[..., 4 * r_width + lora:], zs], axis=-1)
    rwkv_cols = jnp.concatenate([proj_s3[:, :, c0:c0 + 4 * r_width], proj_s3[:, :, lora0 * LANES:]], axis=-1)
    prev = jnp.concatenate([shift_pad, rwkv_cols[:, :-1]], axis=1).reshape(n_s, -1)
    prepped = _rwkv_sample_prep(proj_s, prev, mu_pad, hparams, w2p, a2p, n_heads=r_heads, rw0=rw0, lora0=lora0)
    prepped = [a.reshape(DB, TS, r_width) for a in prepped]
    rw_s, wkv_s = _rwkv_sample(prepped, state_wkv[0], hparams, n_heads=r_heads)
    n_pool, page = cache_k.shape[1], cache_k.shape[2]
    n_pages = page_table.shape[1]
    kc = jnp.transpose(cache_k[0], (0, 2, 3, 4, 1)).reshape(n_pool, a_width, page)
    at_s = _attn_sample(proj_s3, kc, cache_v[0].reshape(n_pool, page * a_heads, LANES), page_table.reshape(-1),
                        lam_params, subw,
                        n_heads=a_heads, at0=at0, pages=_largest_tile(n_pages, 8), lam_init=lam_init)
    y_s = _outproj(xs, rw_s.reshape(n_s, r_width).astype(BF16), at_s.reshape(n_s, a_width).astype(BF16), proj_s,
                   wbr, wba, wo, nf, tm=_largest_tile(n_s, 256))

    k0 = (at0 + a_heads) * LANES
    v0 = (at0 + 2 * a_heads) * LANES

    def shift_row(p3):
        last = p3[:, -1]
        return jnp.concatenate([last[:, c0:c0 + 4 * r_width], last[:, lora0 * LANES:lora0 * LANES + lora],
                                last[:, (lora0 + 1) * LANES:(lora0 + 1) * LANES + lora]], axis=-1)[None]

    return (y_p.reshape(B, T, D), y_s.reshape(DB, TS, D),
            proj_p3[:, :, k0:k0 + a_width].reshape(1, B, T, a_heads, 2, HEAD_DIM),
            proj_p3[:, :, v0:v0 + a_width].reshape(1, B, T, a_heads, LANES),
            proj_s3[:, :, k0:k0 + a_width].reshape(1, DB, TS, a_heads, 2, HEAD_DIM),
            proj_s3[:, :, v0:v0 + a_width].reshape(1, DB, TS, a_heads, LANES),
            wkv_p[None], wkv_s[None], shift_row(proj_p3), shift_row(proj_s3))
```

```python
import functools
import math

import jax
import jax.numpy as jnp
from jax import lax
from jax.experimental import pallas as pl
from jax.experimental.pallas import tpu as pltpu

F32 = jnp.float32
BF16 = jnp.bfloat16

LANES = 128
HEAD_DIM = 64
CHUNK = 64
NORM_EPS = 1e-6
RWKV_GN_EPS = 64e-5
NEG_INF = -1e30
ATTN_SCALE = HEAD_DIM ** -0.5
VMEM_LIMIT = 56 * 1024 * 1024

HI = lax.Precision.HIGHEST


def _dot(a, b, prec=None):
    return lax.dot_general(a, b, (((1,), (0,)), ((), ())), precision=prec, preferred_element_type=F32)


def _dot_nt(a, b, prec=None):
    return lax.dot_general(a, b, (((1,), (1,)), ((), ())), precision=prec, preferred_element_type=F32)


def _dot_tn(a, b, prec=None):
    return lax.dot_general(a, b, (((0,), (0,)), ((), ())), precision=prec, preferred_element_type=F32)


def _iota2(shape, dim):
    return lax.broadcasted_iota(jnp.int32, shape, dim)


def _sigmoid(x):
    return 1.0 / (1.0 + jnp.exp(-x))


def _silu(x):
    return x * _sigmoid(x)


def _softplus(x):
    return jnp.maximum(x, 0.0) + jnp.log(1.0 + jnp.exp(-jnp.abs(x)))


def _params(semantics):
    return pltpu.CompilerParams(dimension_semantics=semantics, vmem_limit_bytes=VMEM_LIMIT)


def _inproj_kernel(x_ref, g_ref, w_ref, o_ref, xn_ref):
    @pl.when(pl.program_id(1) == 0)
    def _():
        x = x_ref[...]
        xn = x * lax.rsqrt(jnp.mean(x * x, axis=-1, keepdims=True) + NORM_EPS) * g_ref[...]
        xn_ref[...] = xn.astype(BF16)

    o_ref[...] = _dot(xn_ref[...], w_ref[...])


def _inproj(x, g, w, *, tm, tn):
    n, d = x.shape
    pw = w.shape[1]
    return pl.pallas_call(
        _inproj_kernel,
        grid=(n // tm, pw // tn),
        in_specs=[pl.BlockSpec((tm, d), lambda i, j: (i, 0)),
                  pl.BlockSpec((1, d), lambda i, j: (0, 0)),
                  pl.BlockSpec((d, tn), lambda i, j: (0, j))],
        out_specs=pl.BlockSpec((tm, tn), lambda i, j: (i, j)),
        out_shape=jax.ShapeDtypeStruct((n, pw), F32),
        scratch_shapes=[pltpu.VMEM((tm, d), BF16)],
        compiler_params=_params(("parallel", "arbitrary")),
        name="inproj",
    )(x, g, w)


def _rwkv_prepare(ps_r, ps_k, ps_v, ps_g, ps_wd, ps_ad, hp, w2, a2, gsum):
    w0, a0, k_k, k_a = hp[0:1], hp[1:2], hp[2:3], hp[3:4]
    w_pre = w0 + _dot(jnp.tanh(ps_wd).astype(BF16), w2)
    w = -_softplus(-w_pre) - 0.5
    log_decay = -jnp.exp(w)
    a = _sigmoid(a0 + _dot(ps_ad.astype(BF16), a2))
    kk = ps_k * k_k
    kk = kk / jnp.maximum(jnp.sqrt(gsum(kk * kk)), 1e-12)
    k = ps_k * (1.0 + (a - 1.0) * k_a)
    return ps_r, k, ps_v, ps_g, kk, kk * a, log_decay


def _rwkv_finish(y, r, k, v, g, hp, gsum):
    r_k, lnx_w, lnx_b = hp[4:5], hp[5:6], hp[6:7]
    mean = gsum(y) * (1.0 / HEAD_DIM)
    d = y - mean
    var = gsum(d * d) * (1.0 / HEAD_DIM)
    yn = d * lax.rsqrt(var + RWKV_GN_EPS) * lnx_w + lnx_b
    bonus = gsum(r * k * r_k) * v
    return (yn + bonus) * _silu(g)


def _head_masks(shape):
    lane = _iota2(shape, 1)
    m0 = (lane < HEAD_DIM).astype(F32)
    return m0, 1.0 - m0


def _group_ones():
    i = _iota2((LANES, LANES), 0) // HEAD_DIM
    j = _iota2((LANES, LANES), 1) // HEAD_DIM
    return (i == j).astype(BF16)


def _gsum_wide(x):
    ones = _group_ones()
    hi = x.astype(BF16)
    lo = (x - hi.astype(F32)).astype(BF16)
    parts = []
    for j in range(x.shape[1] // LANES):
        ls = slice(j * LANES, (j + 1) * LANES)
        parts.append(_dot(hi[:, ls], ones) + _dot(lo[:, ls], ones))
    return parts[0] if len(parts) == 1 else jnp.concatenate(parts, axis=1)


def _rwkv_prompt_kernel(pr_ref, pk_ref, pv_ref, pg_ref, pwd_ref, pad_ref,
                        mr_ref, mk_ref, mv_ref, mg_ref, mwd_ref, mad_ref,
                        hp_ref, w2_ref, a2_ref,
                        out_ref, state_ref,
                        carry_ref, cw_ref, m_ref, r_s, k_s, v_s, g_s, kk_s, b_s, ld_s, *, tb, ns):
    t_idx = pl.program_id(2)
    n_t = pl.num_programs(2)
    C = CHUNK

    @pl.when(t_idx == 0)
    def _():
        carry_ref[...] = jnp.zeros_like(carry_ref)
        cw_ref[...] = jnp.zeros_like(cw_ref)
        m_ref[...] = jnp.zeros_like(m_ref)

    def shifted(p_ref, mu_ref, c_ref, idx):
        p = p_ref[0]
        row = _iota2(p.shape, 0)
        prev = pltpu.roll(p, 1, axis=0)
        prev = jnp.where(row == 0, c_ref[idx], prev)
        c_ref[idx] = p[tb - 1:tb, :]
        return p + (prev - p) * mu_ref[...]

    hp = hp_ref[...]
    r, k, v, g, kk, b, ld = _rwkv_prepare(
        shifted(pr_ref, mr_ref, carry_ref, 0), shifted(pk_ref, mk_ref, carry_ref, 1),
        shifted(pv_ref, mv_ref, carry_ref, 2), shifted(pg_ref, mg_ref, carry_ref, 3),
        shifted(pwd_ref, mwd_ref, cw_ref, 0), shifted(pad_ref, mad_ref, cw_ref, 1),
        hp, w2_ref[...], a2_ref[...], _gsum_wide)
    r_s[...] = r
    k_s[...] = k
    v_s[...] = v
    g_s[...] = g
    kk_s[...] = kk
    b_s[...] = b
    ld_s[...] = ld

    m0, m1 = _head_masks((C, LANES))
    ii = _iota2((2 * C, 2 * C), 0)
    jj = _iota2((2 * C, 2 * C), 1)
    strict = ii > jj
    incl = ii >= jj
    eye = ii == jj
    tri = (_iota2((C, C), 0) >= _iota2((C, C), 1)).astype(F32)
    stack2 = lambda x: jnp.concatenate([x * m0, x * m1], axis=0).astype(BF16)
    bf = lambda x: x.astype(BF16)

    lanes = [slice(s * LANES, (s + 1) * LANES) for s in range(ns)]
    each = lambda f, *cols: [f(*args) for args in zip(*cols)]

    def chunk(ci, ms):
        sl = pl.ds(pl.multiple_of(ci * C, C), C)
        ld = [ld_s[sl, ls] for ls in lanes]
        c = each(lambda x: _dot(tri, x, HI), ld)
        c_end = each(lambda x: x[C - 1:C, :], c)
        p_inv = each(lambda x: jnp.exp(-x), c)
        p_end = each(lambda e, x: jnp.exp(e - x), c_end, c)
        r2 = each(lambda ls, x: stack2(r_s[sl, ls] * jnp.exp(x)), lanes, c)
        kk2 = each(lambda ls, x, d: stack2(kk_s[sl, ls] * jnp.exp(x - d)), lanes, c, ld)
        kh2 = each(lambda ls, p: stack2(k_s[sl, ls] * p), lanes, p_inv)
        bh2 = each(lambda ls, p: stack2(b_s[sl, ls] * p), lanes, p_inv)
        ke2 = each(lambda ls, p: stack2(k_s[sl, ls] * p), lanes, p_end)
        be2 = each(lambda ls, p: stack2(b_s[sl, ls] * p), lanes, p_end)
        v2 = each(lambda ls: stack2(v_s[sl, ls]), lanes)
        a = each(lambda w, x, y, z: _dot_nt(jnp.concatenate([w, x], axis=0), jnp.concatenate([y, z], axis=0)),
                 kk2, r2, kh2, bh2)
        l_k = each(lambda x: bf(jnp.where(strict, x[:2 * C, :2 * C], 0.0)), a)
        l_b = each(lambda x: bf(jnp.where(strict, x[:2 * C, 2 * C:], 0.0)), a)
        a_rk = each(lambda x: bf(jnp.where(incl, x[2 * C:, :2 * C], 0.0)), a)
        a_rb = each(lambda x: bf(jnp.where(incl, -x[2 * C:, 2 * C:], 0.0)), a)
        mb = each(bf, ms)
        x = each(lambda w, lk, m, v: _dot(jnp.concatenate([w, lk], axis=1), jnp.concatenate([m, v], axis=0)),
                 kk2, l_k, mb, v2)
        pw = l_b
        u = each(lambda x_, p: x_ - _dot(p, bf(x_)), x, pw)
        for _ in range(5):
            pw = each(lambda p: bf(_dot(p, p)), pw)
            u = each(lambda u_, p: u_ + _dot(p, bf(u_)), u, pw)
        ub = each(bf, u)
        y2 = each(lambda r_, ak, ab, m, v, u_: _dot(jnp.concatenate([r_, ak, ab], axis=1),
                                                     jnp.concatenate([m, v, u_], axis=0)),
                  r2, a_rk, a_rb, mb, v2, ub)
        p_c_col = each(lambda e: jnp.sum(jnp.where(eye, jnp.broadcast_to(jnp.exp(e), (2 * C, LANES)), 0.0),
                                         axis=1, keepdims=True), c_end)
        m_new = each(lambda m, pc, ke, be, v, u_: m * pc + _dot_tn(jnp.concatenate([ke, -be], axis=0),
                                                                   jnp.concatenate([v, u_], axis=0)),
                     ms, p_c_col, ke2, be2, v2, ub)
        for ls, y in zip(lanes, y2):
            out = _rwkv_finish(y[:C] + y[C:], r_s[sl, ls], k_s[sl, ls], v_s[sl, ls], g_s[sl, ls], hp[:, ls],
                               _gsum_wide)
            out_ref[0, sl, ls] = out.astype(out_ref.dtype)
        return tuple(m_new)

    ms = lax.fori_loop(0, tb // C, chunk, tuple(m_ref[s] for s in range(ns)))
    for s in range(ns):
        m_ref[s] = ms[s]

    @pl.when(t_idx == n_t - 1)
    def _():
        for s in range(ns):
            mt = ms[s].T
            state_ref[0, 2 * s] = mt[:HEAD_DIM, :HEAD_DIM]
            state_ref[0, 2 * s + 1] = mt[HEAD_DIM:, HEAD_DIM:]


def _rwkv_prompt(proj, mu_pad, hparams, w2p, a2p, *, n_heads, rw0, lora0, tb, ns):
    B, T, _ = proj.shape
    n_hp = n_heads // 2
    assert rw0 % ns == 0 and n_hp % ns == 0
    n_grp = n_hp // ns
    wide = ns * LANES
    col = lambda off: (lambda b, h, t: (b, t, off // ns + h))
    colc = lambda off: (lambda b, h, t: (b, t, off))
    mcol = lambda off: (lambda b, h, t: (0, off // ns + h))
    mcolc = lambda off: (lambda b, h, t: (0, off))
    pblk = lambda im: pl.BlockSpec((1, tb, wide), im)
    mblk = lambda im: pl.BlockSpec((1, wide), im)
    in_specs = [pblk(col(rw0)), pblk(col(rw0 + n_hp)), pblk(col(rw0 + 2 * n_hp)), pblk(col(rw0 + 3 * n_hp)),
                pl.BlockSpec((1, tb, LANES), colc(lora0)), pl.BlockSpec((1, tb, LANES), colc(lora0 + 1)),
                mblk(mcol(0)), mblk(mcol(n_hp)), mblk(mcol(2 * n_hp)), mblk(mcol(3 * n_hp)),
                pl.BlockSpec((1, LANES), mcolc(4 * n_hp)), pl.BlockSpec((1, LANES), mcolc(4 * n_hp + 1)),
                pl.BlockSpec((8, wide), lambda b, h, t: (0, h)),
                pl.BlockSpec((LANES, wide), lambda b, h, t: (0, h)),
                pl.BlockSpec((LANES, wide), lambda b, h, t: (0, h))]
    out_specs = [pl.BlockSpec((1, tb, wide), lambda b, h, t: (b, t, h)),
                 pl.BlockSpec((1, 2 * ns, HEAD_DIM, HEAD_DIM), lambda b, h, t: (b, h, 0, 0))]
    scratch = ([pltpu.VMEM((4, 1, wide), F32), pltpu.VMEM((2, 1, LANES), F32), pltpu.VMEM((ns, LANES, LANES), F32)]
               + [pltpu.VMEM((tb, wide), F32)] * 7)
    return pl.pallas_call(
        functools.partial(_rwkv_prompt_kernel, tb=tb, ns=ns),
        grid=(B, n_grp, T // tb),
        in_specs=in_specs, out_specs=out_specs,
        out_shape=[jax.ShapeDtypeStruct((B, T, n_heads * HEAD_DIM), BF16),
                   jax.ShapeDtypeStruct((B, n_heads, HEAD_DIM, HEAD_DIM), F32)],
        scratch_shapes=scratch,
        compiler_params=_params(("parallel", "parallel", "arbitrary")),
        name="rwkv_prompt",
    )(proj, proj, proj, proj, proj, proj, mu_pad, mu_pad, mu_pad, mu_pad, mu_pad, mu_pad, hparams, w2p, a2p)


def _rwkv_params(mu, w0, w2, a0, a2, k_k, k_a, r_k, lnx_w, lnx_b):
    width = w0.shape[0]
    lora = w2.shape[0]
    zpad = jnp.zeros((LANES - lora,), F32)
    mu_pad = jnp.concatenate([mu[:4 * width], mu[4 * width:4 * width + lora], zpad, mu[4 * width + lora:], zpad])[None, :]
    hparams = jnp.stack([w0, a0, k_k, k_a, r_k.reshape(-1), lnx_w, lnx_b, jnp.zeros_like(w0)]).astype(F32)
    pad_rows = lambda m: jnp.concatenate([m, jnp.zeros((LANES - lora, width), m.dtype)], axis=0).astype(BF16)
    return mu_pad.astype(F32), hparams, pad_rows(w2), pad_rows(a2)


def _rwkv_sample_prep_kernel(pr_ref, pk_ref, pv_ref, pg_ref, pwd_ref, pad_ref,
                             qr_ref, qk_ref, qv_ref, qg_ref, qwd_ref, qad_ref,
                             mr_ref, mk_ref, mv_ref, mg_ref, mwd_ref, mad_ref,
                             hp_ref, w2_ref, a2_ref,
                             r_out, k_out, v_out, g_out, kk_out, b_out, ld_out, *, steps):
    def shifted(p_ref, q_ref, mu_ref):
        p = p_ref[...]
        return p + (q_ref[...] - p) * mu_ref[...]

    outs = _rwkv_prepare(
        shifted(pr_ref, qr_ref, mr_ref), shifted(pk_ref, qk_ref, mk_ref), shifted(pv_ref, qv_ref, mv_ref),
        shifted(pg_ref, qg_ref, mg_ref), shifted(pwd_ref, qwd_ref, mwd_ref), shifted(pad_ref, qad_ref, mad_ref),
        hp_ref[...], w2_ref[...], a2_ref[...], _gsum_wide)
    db = r_out.shape[2]
    for o_ref, val in zip((r_out, k_out, v_out, g_out, kk_out, b_out, ld_out), outs):
        for t in range(steps):
            o_ref[t] = val[t * db:(t + 1) * db, :].T


def _rwkv_sample_prep(proj, prev, mu_pad, hparams, w2p, a2p, *, n_heads, rw0, lora0, steps):
    n = proj.shape[0]
    db = n // steps
    n_hp = n_heads // 2
    blk = lambda off, fixed=False: pl.BlockSpec((n, LANES), (lambda h: (0, off)) if fixed else (lambda h: (0, off + h)))
    mblk = lambda off, fixed=False: pl.BlockSpec((1, LANES), (lambda h: (0, off)) if fixed else (lambda h: (0, off + h)))
    offs = [0, n_hp, 2 * n_hp, 3 * n_hp]
    in_specs = ([blk(rw0 + o) for o in offs] + [blk(lora0, True), blk(lora0 + 1, True)]
                + [blk(o) for o in offs] + [blk(4 * n_hp, True), blk(4 * n_hp + 1, True)]
                + [mblk(o) for o in offs] + [mblk(4 * n_hp, True), mblk(4 * n_hp + 1, True)]
                + [pl.BlockSpec((8, LANES), lambda h: (0, h)),
                   pl.BlockSpec((LANES, LANES), lambda h: (0, h)),
                   pl.BlockSpec((LANES, LANES), lambda h: (0, h))])
    width = n_heads * HEAD_DIM
    return pl.pallas_call(
        functools.partial(_rwkv_sample_prep_kernel, steps=steps),
        grid=(n_hp,),
        in_specs=in_specs,
        out_specs=[pl.BlockSpec((steps, LANES, db), lambda h: (0, h, 0))] * 7,
        out_shape=[jax.ShapeDtypeStruct((steps, width, db), F32)] * 7,
        compiler_params=_params(("parallel",)),
        name="rwkv_sample_prep",
    )(*([proj] * 6 + [prev] * 6 + [mu_pad] * 6 + [hparams, w2p, a2p]))


def _rwkv_sample_kernel(r_ref, k_ref, v_ref, g_ref, kk_ref, b_ref, ld_ref, s_ref, hpt_ref, out_ref, so_ref, y_s, *,
                        steps):
    db = out_ref.shape[0] // steps

    for hh in range(2):
        r0 = hh * HEAD_DIM
        rows = slice(r0, r0 + HEAD_DIM)
        kk = [kk_ref[t, rows, :] for t in range(steps)]
        bb = [b_ref[t, rows, :] for t in range(steps)]
        kv = [k_ref[t, rows, :] for t in range(steps)]
        rv = [r_ref[t, rows, :] for t in range(steps)]
        decay = [jnp.exp(ld_ref[t, rows, :]) for t in range(steps)]

        def row_block(v, carry):
            s_v = s_ref[hh, v]
            for t in range(steps):
                s_kk = jnp.sum(s_v * kk[t], axis=0, keepdims=True)
                s_v = s_v * decay[t] - s_kk * bb[t] + v_ref[t, pl.ds(r0 + v, 1), :] * kv[t]
                y_s[t, pl.ds(r0 + v, 1), :] = jnp.sum(s_v * rv[t], axis=0, keepdims=True)
            so_ref[hh, v] = s_v
            return carry

        lax.fori_loop(0, HEAD_DIM, row_block, 0, unroll=2)

    hpt = hpt_ref[...]
    r_k, lnx_w, lnx_b = hpt[:, 4:5], hpt[:, 5:6], hpt[:, 6:7]
    for t in range(steps):
        outs = []
        for hh in range(2):
            rows = slice(hh * HEAD_DIM, (hh + 1) * HEAD_DIM)
            y = y_s[t, rows, :]
            mean = jnp.mean(y, axis=0, keepdims=True)
            d = y - mean
            var = jnp.mean(d * d, axis=0, keepdims=True)
            yn = d * lax.rsqrt(var + RWKV_GN_EPS) * lnx_w[rows] + lnx_b[rows]
            bonus = jnp.sum(r_ref[t, rows, :] * k_ref[t, rows, :] * r_k[rows], axis=0, keepdims=True) * v_ref[t, rows, :]
            outs.append((yn + bonus) * _silu(g_ref[t, rows, :]))
        out_ref[t * db:(t + 1) * db, :] = jnp.concatenate(outs, axis=0).T


def _rwkv_sample(prepped, state, hparams_t, *, n_heads):
    steps, width, db = prepped[0].shape
    tok = pl.BlockSpec((steps, LANES, db), lambda j: (0, j, 0))
    st = pl.BlockSpec((2, HEAD_DIM, HEAD_DIM, db), lambda j: (j, 0, 0, 0))
    return pl.pallas_call(
        functools.partial(_rwkv_sample_kernel, steps=steps),
        grid=(n_heads // 2,),
        in_specs=[tok] * 7 + [st, pl.BlockSpec((LANES, 8), lambda j: (j, 0))],
        out_specs=[pl.BlockSpec((steps * db, LANES), lambda j: (0, j)), st],
        out_shape=[jax.ShapeDtypeStruct((steps * db, width), F32),
                   jax.ShapeDtypeStruct(state.shape, F32)],
        scratch_shapes=[pltpu.VMEM((steps, LANES, db), F32)],
        compiler_params=_params(("parallel",)),
        name="rwkv_sample",
    )(*prepped, state, hparams_t)


def _diff_lambda(lam_ref, lam_init):
    lp = lam_ref[...]
    s1 = jnp.sum(lp[0:1] * lp[1:2], axis=1, keepdims=True)
    s2 = jnp.sum(lp[2:3] * lp[3:4], axis=1, keepdims=True)
    return jnp.exp(s1) - jnp.exp(s2) + lam_init


def _diff_finish(o1, o2, lam, subw, ag, lam_init):
    o = o1 - lam * o2
    o = o * lax.rsqrt(jnp.mean(o * o, axis=-1, keepdims=True) + NORM_EPS)
    return o * subw * (1.0 - lam_init) * _silu(ag)


def _stack_maps(q):
    m0, m1 = _head_masks(q.shape)
    return jnp.concatenate([q * m0, q * m1], axis=0)


def _attn_prompt_kernel(q_ref, k_ref, v_ref, ag_ref, lam_ref, subw_ref, o_ref, kb_s, vb_s, m_s, acc_s, *,
                        tq, tk, sub, lam_init):
    qi = pl.program_id(2)
    seq = k_ref.shape[1]

    @pl.when(qi == 0)
    def _():
        kb_s[...] = k_ref[0].astype(BF16)
        vb_s[:, :LANES] = v_ref[0].astype(BF16)
        vb_s[:, LANES:] = jnp.ones((seq, LANES), BF16)

    q2 = _stack_maps(q_ref[0] * ATTN_SCALE).astype(BF16)
    m_s[...] = jnp.full_like(m_s, NEG_INF)
    acc_s[...] = jnp.zeros_like(acc_s)
    n_sub = 2 * tq // sub

    def block(k0, diag_off):
        kb = kb_s[pl.ds(k0, tk), :]
        vb = vb_s[pl.ds(k0, tk), :]

        def scores(j):
            s = _dot_nt(q2[j * sub:(j + 1) * sub], kb)
            if diag_off is not None:
                q_pos = (_iota2(s.shape, 0) + j * sub) & (tq - 1)
                s = jnp.where(_iota2(s.shape, 1) + diag_off <= q_pos, s, NEG_INF)
            return s

        s_next = scores(0)
        for j in range(n_sub):
            s = s_next
            if j + 1 < n_sub:
                s_next = scores(j + 1)
            rows = slice(j * sub, (j + 1) * sub)
            m_old = m_s[rows]
            m_new = jnp.maximum(m_old, jnp.max(s, axis=1, keepdims=True))
            alpha = jnp.exp(m_old - m_new)
            p = jnp.exp(s - m_new).astype(BF16)
            acc_s[rows] = alpha * acc_s[rows] + _dot(p, vb)
            m_s[rows] = m_new

    def body(ki, carry):
        block(pl.multiple_of(ki * tk, tk), None)
        return carry

    lax.fori_loop(0, qi * (tq // tk), body, 0)
    for d in range(tq // tk):
        block(pl.multiple_of(qi * tq + d * tk, tk), d * tk)
    acc = acc_s[...]
    o = acc[:, :LANES] / acc[:, LANES:]
    lam = _diff_lambda(lam_ref, lam_init)
    o_ref[0] = _diff_finish(o[:tq], o[tq:], lam, subw_ref[...], ag_ref[0], lam_init).astype(o_ref.dtype)


def _attn_prompt(proj, lam_params, subw, *, n_heads, at0, tq, lam_init):
    B, T, _ = proj.shape
    qblk = lambda off: pl.BlockSpec((1, tq, LANES), lambda b, h, i: (b, i, off + h))
    kblk = lambda off: pl.BlockSpec((1, T, LANES), lambda b, h, i: (b, 0, off + h))
    return pl.pallas_call(
        functools.partial(_attn_prompt_kernel, tq=tq, tk=tq, sub=min(256, 2 * tq), lam_init=lam_init),
        grid=(B, n_heads, T // tq),
        in_specs=[qblk(at0), kblk(at0 + n_heads), kblk(at0 + 2 * n_heads), qblk(at0 + 3 * n_heads),
                  pl.BlockSpec(lam_params.shape, lambda b, h, i: (0, 0)),
                  pl.BlockSpec((1, LANES), lambda b, h, i: (0, 0))],
        out_specs=pl.BlockSpec((1, tq, LANES), lambda b, h, i: (b, i, h)),
        out_shape=jax.ShapeDtypeStruct((B, T, n_heads * LANES), BF16),
        scratch_shapes=[pltpu.VMEM((T, LANES), BF16), pltpu.VMEM((T, 2 * LANES), BF16),
                        pltpu.VMEM((2 * tq, 1), F32), pltpu.VMEM((2 * tq, 2 * LANES), F32)],
        compiler_params=_params(("parallel", "parallel", "arbitrary")),
        name="attn_prompt",
    )(proj, proj, proj, proj, lam_params, subw)


def _attn_sample_kernel(pt_ref, q_ref, kn_ref, vn_ref, ag_ref, *refs, n_heads, steps, pages, lam_init):
    del pt_ref
    k_refs, v_refs = refs[:pages], refs[pages:2 * pages]
    lam_ref, subw_ref, o_ref, m_s, l_s, acc_s = refs[2 * pages:]
    gi = pl.program_id(1)
    n_groups = pl.num_programs(1)
    rows_h = 2 * steps
    rows = n_heads * rows_h
    kdim, page = k_refs[0].shape[1], k_refs[0].shape[2]
    assert steps & (steps - 1) == 0

    @pl.when(gi == 0)
    def _():
        m_s[...] = jnp.full_like(m_s, NEG_INF)
        l_s[...] = jnp.zeros_like(l_s)
        acc_s[...] = jnp.zeros_like(acc_s)

    q_all = q_ref[0] * ATTN_SCALE
    hs = lambda h: slice(h * LANES, (h + 1) * LANES)
    q_rep = jnp.concatenate([q_all] * (rows // steps), axis=0)
    col_grp = lax.shift_right_logical(_iota2((rows, kdim), 1), int(math.log2(HEAD_DIM)))
    row_grp = lax.shift_right_logical(_iota2((rows, kdim), 0), int(math.log2(steps)))
    q_bd = jnp.where(col_grp == row_grp, q_rep, 0.0).astype(BF16)

    s = jnp.concatenate([_dot(q_bd, k_refs[i][0].astype(BF16)) for i in range(pages)], axis=1)
    m_old = m_s[...]
    m_run = jnp.maximum(m_old, jnp.max(s, axis=1, keepdims=True))
    alpha = jnp.exp(m_old - m_run)
    p = jnp.exp(s - m_run)
    l_run = alpha * l_s[...] + jnp.sum(p, axis=1, keepdims=True)
    pv = []
    for h in range(n_heads):
        v_h = jnp.concatenate([v_refs[i][0, pl.ds(h, page, stride=n_heads), :].astype(BF16) for i in range(pages)],
                              axis=0)
        pv.append(_dot(p[h * rows_h:(h + 1) * rows_h].astype(BF16), v_h))
    acc = alpha * acc_s[...] + jnp.concatenate(pv, axis=0)
    m_s[...] = m_run
    l_s[...] = l_run
    acc_s[...] = acc

    @pl.when(gi == n_groups - 1)
    def _():
        lam = _diff_lambda(lam_ref, lam_init)
        tok = _iota2((rows_h, LANES), 0) & (steps - 1)
        lane = _iota2((rows_h, LANES), 1)
        for h in range(n_heads):
            rs = slice(h * rows_h, (h + 1) * rows_h)
            q2 = _stack_maps(q_all[:, hs(h)])
            kn = kn_ref[0][:, hs(h)]
            vn = vn_ref[0][:, hs(h)]
            s = jnp.full((rows_h, LANES), NEG_INF, F32)
            for j in range(steps):
                sj = jnp.sum(q2 * kn[j:j + 1, :], axis=1, keepdims=True)
                s = jnp.where((lane == j) & (tok >= j), sj, s)
            m_old = m_run[rs]
            m_new = jnp.maximum(m_old, jnp.max(s, axis=1, keepdims=True))
            alpha = jnp.exp(m_old - m_new)
            p = jnp.exp(s - m_new)
            l_fin = alpha * l_run[rs] + jnp.sum(p, axis=1, keepdims=True)
            a_fin = alpha * acc[rs]
            for j in range(steps):
                a_fin = a_fin + p[:, j:j + 1] * vn[j:j + 1, :]
            o = a_fin / l_fin
            o_ref[0, :, hs(h)] = _diff_finish(o[:steps], o[steps:], lam, subw_ref[...], ag_ref[0][:, hs(h)], lam_init)


def _attn_sample(qkvg, cache_k, cache_v, page_table, lam_params, subw, *, n_heads, pages, lam_init):
    db, steps, _ = qkvg.shape
    n_pages = page_table.shape[0] // db
    n_groups = n_pages // pages
    width = n_heads * LANES
    rows = n_heads * 2 * steps
    tokblk = lambda off: pl.BlockSpec((1, steps, width), lambda b, g, pt: (b, 0, off))
    pageblk = lambda cache, i: pl.BlockSpec((1,) + cache.shape[1:],
                                            lambda b, g, pt: (pt[(b * n_groups + g) * pages + i], 0, 0))
    grid_spec = pltpu.PrefetchScalarGridSpec(
        num_scalar_prefetch=1,
        grid=(db, n_groups),
        in_specs=([tokblk(0), tokblk(1), tokblk(2), tokblk(3)]
                  + [pageblk(cache_k, i) for i in range(pages)] + [pageblk(cache_v, i) for i in range(pages)]
                  + [pl.BlockSpec(lam_params.shape, lambda b, g, pt: (0, 0)),
                     pl.BlockSpec((1, LANES), lambda b, g, pt: (0, 0))]),
        out_specs=pl.BlockSpec((1, steps, width), lambda b, g, pt: (b, 0, 0)),
        scratch_shapes=[pltpu.VMEM((rows, 1), F32), pltpu.VMEM((rows, 1), F32), pltpu.VMEM((rows, LANES), F32)])
    return pl.pallas_call(
        functools.partial(_attn_sample_kernel, n_heads=n_heads, steps=steps, pages=pages, lam_init=lam_init),
        grid_spec=grid_spec,
        out_shape=jax.ShapeDtypeStruct((db, steps, width), F32),
        compiler_params=_params(("parallel", "arbitrary")),
        name="attn_sample",
    )(page_table, qkvg, qkvg, qkvg, qkvg, *([cache_k] * pages), *([cache_v] * pages), lam_params, subw)


def _outproj_kernel(x_ref, rw_ref, at_ref, grw_ref, gat_ref, wbr_ref, wba_ref, wo_ref, nf_ref, o_ref):
    merged = (_sigmoid(grw_ref[...]) * _dot(rw_ref[...], wbr_ref[...])
              + _sigmoid(gat_ref[...]) * _dot(at_ref[...], wba_ref[...]))
    y = x_ref[...] + _dot(merged.astype(BF16), wo_ref[...])
    o_ref[...] = y * lax.rsqrt(jnp.mean(y * y, axis=-1, keepdims=True) + NORM_EPS) * nf_ref[...]


def _outproj(x, rw, at, proj, wbr, wba, wo, nf, *, tm):
    n, d = x.shape
    row = lambda w: pl.BlockSpec((tm, w), lambda i: (i, 0))
    const = lambda a: pl.BlockSpec(a.shape, lambda i: (0, 0), pipeline_mode=pl.Buffered(1))
    return pl.pallas_call(
        _outproj_kernel,
        grid=(n // tm,),
        in_specs=[row(d), row(rw.shape[1]), row(at.shape[1]), row(d),
                  pl.BlockSpec((tm, d), lambda i: (i, 1)),
                  const(wbr), const(wba), const(wo), const(nf)],
        out_specs=row(d),
        out_shape=jax.ShapeDtypeStruct((n, d), F32),
        compiler_params=_params(("parallel",)),
        name="outproj",
    )(x, rw, at, proj, proj, wbr, wba, wo, nf)


def _largest_tile(n, cap):
    t = cap
    while n % t:
        t //= 2
    return t


def kernel(x_prompt, x_sample, cache_k, cache_v, state_wkv, state_shift, page_table, norm_in, w_in, mu_shift, w0, w2, a0, a2, k_k, k_a, r_k, lnx_w, lnx_b, lambda_q1, lambda_k1, lambda_q2, lambda_k2, subln_w, w_br_rwkv, w_br_attn, w_out, norm_f):
    depth = w_in.shape[0]
    assert depth == 1, "single-layer step"
    B, T, D = x_prompt.shape
    DB, TS, _ = x_sample.shape
    r_heads = r_k.shape[1]
    r_width = r_heads * HEAD_DIM
    lora = w2.shape[1]
    a_heads = cache_k.shape[3]
    a_width = a_heads * LANES
    rw_in = 4 * r_width + 2 * lora
    lam_init = 0.8 - 0.6 * math.exp(-0.3 * 0)

    w = w_in[0]
    zcol = jnp.zeros((D, LANES - lora), w.dtype)
    w_pad = jnp.concatenate([w[:, rw_in + 4 * a_width:], w[:, :4 * r_width], w[:, rw_in:rw_in + 4 * a_width],
                             w[:, 4 * r_width:4 * r_width + lora], zcol, w[:, 4 * r_width + lora:rw_in], zcol],
                            axis=1).astype(BF16)
    rw0 = 2 * D // LANES
    at0 = rw0 + 4 * r_width // LANES
    lora0 = at0 + 4 * a_width // LANES
    pw = w_pad.shape[1]

    mu_pad, hparams, w2p, a2p = _rwkv_params(mu_shift[0], w0[0], w2[0], a0[0], a2[0], k_k[0], k_a[0], r_k[0],
                                             lnx_w[0], lnx_b[0])
    lam_params = jnp.stack([lambda_q1[0], lambda_k1[0], lambda_q2[0], lambda_k2[0]]).astype(F32)
    subw = subln_w[0][None, :].astype(F32)
    g_in = norm_in[0][None, :].astype(F32)
    wbr = w_br_rwkv[0].astype(BF16)
    wba = w_br_attn[0].astype(BF16)
    wo = w_out[0].astype(BF16)
    nf = norm_f[None, :].astype(F32)
    tn = 7 * LANES

    n_p = B * T
    xp = x_prompt.reshape(n_p, D)
    proj_p = _inproj(xp, g_in, w_pad, tm=_largest_tile(n_p, 1024), tn=tn)
    proj_p3 = proj_p.reshape(B, T, pw)
    rw_p, wkv_p = _rwkv_prompt(proj_p3, mu_pad, hparams, w2p, a2p, n_heads=r_heads, rw0=rw0, lora0=lora0,
                               tb=_largest_tile(T, 512), ns=_largest_tile(r_heads // 2, 8))
    at_p = _attn_prompt(proj_p3, lam_params, subw, n_heads=a_heads, at0=at0, tq=_largest_tile(T, 512),
                        lam_init=lam_init)
    y_p = _outproj(xp, rw_p.reshape(n_p, r_width), at_p.reshape(n_p, a_width), proj_p, wbr, wba, wo, nf,
                   tm=_largest_tile(n_p, 256))

    n_s = DB * TS
    xs = jnp.transpose(x_sample, (1, 0, 2)).reshape(n_s, D)
    proj_s = _inproj(xs, g_in, w_pad, tm=_largest_tile(n_s, 512), tn=tn)
    c0 = rw0 * LANES
    zs = jnp.zeros((DB, LANES - lora), F32)
    sh = state_shift[0]
    shift_pad = jnp.concatenate([sh[:, :4 * r_width], sh[:, 4 * r_width:4 * r_width + lora], zs,
                                 sh[:, 4 * r_width + lora:], zs], axis=-1)
    rwkv_cols = jnp.concatenate([proj_s[:, c0:c0 + 4 * r_width], proj_s[:, lora0 * LANES:]], axis=-1)
    prev = jnp.concatenate([shift_pad, rwkv_cols[:-DB]], axis=0)
    prepped = _rwkv_sample_prep(proj_s, prev, mu_pad, hparams, w2p, a2p, n_heads=r_heads, rw0=rw0, lora0=lora0,
                                steps=TS)
    rw_s, wkv_s = _rwkv_sample(prepped, jnp.transpose(state_wkv[0], (1, 2, 3, 0)), hparams.T, n_heads=r_heads)
    wkv_s = jnp.transpose(wkv_s, (3, 0, 1, 2))
    n_pool, page = cache_k.shape[1], cache_k.shape[2]
    n_pages = page_table.shape[1]
    q0 = at0 * LANES
    qkvg_s = jnp.transpose(proj_s[:, q0:q0 + 4 * a_width].reshape(TS, DB, 4 * a_width), (1, 0, 2))
    kc = jnp.transpose(cache_k[0], (0, 2, 3, 4, 1)).reshape(n_pool, a_width, page)
    at_s = _attn_sample(qkvg_s, kc, cache_v[0].reshape(n_pool, page * a_heads, LANES), page_table.reshape(-1),
                        lam_params, subw, n_heads=a_heads, pages=_largest_tile(n_pages, 8), lam_init=lam_init)
    at_s = jnp.transpose(at_s, (1, 0, 2)).reshape(n_s, a_width)
    y_s = _outproj(xs, rw_s.astype(BF16), at_s.astype(BF16), proj_s, wbr, wba, wo, nf, tm=_largest_tile(n_s, 256))
    y_s = jnp.transpose(y_s.reshape(TS, DB, D), (1, 0, 2))

    k0 = (at0 + a_heads) * LANES
    v0 = (at0 + 2 * a_heads) * LANES

    def shift_row(last):
        return jnp.concatenate([last[:, c0:c0 + 4 * r_width], last[:, lora0 * LANES:lora0 * LANES + lora],
                                last[:, (lora0 + 1) * LANES:(lora0 + 1) * LANES + lora]], axis=-1)[None]

    return (y_p.reshape(B, T, D), y_s,
            proj_p3[:, :, k0:k0 + a_width].reshape(1, B, T, a_heads, 2, HEAD_DIM),
            proj_p3[:, :, v0:v0 + a_width].reshape(1, B, T, a_heads, LANES),
            qkvg_s[:, :, a_width:2 * a_width].reshape(1, DB, TS, a_heads, 2, HEAD_DIM),
            qkvg_s[:, :, 2 * a_width:3 * a_width].reshape(1, DB, TS, a_heads, LANES),
            wkv_p[None], wkv_s[None], shift_row(proj_p3[:, -1]), shift_row(proj_s[-DB:]))
```

```python
import functools
import math

import jax
import jax.numpy as jnp
from jax import lax
from jax.experimental import pallas as pl
from jax.experimental.pallas import tpu as pltpu

F32 = jnp.float32
BF16 = jnp.bfloat16

LANES = 128
HEAD_DIM = 64
CHUNK = 64
NORM_EPS = 1e-6
RWKV_GN_EPS = 64e-5
NEG_INF = -1e30
ATTN_SCALE = HEAD_DIM ** -0.5
VMEM_LIMIT = 56 * 1024 * 1024

HI = lax.Precision.HIGHEST


def _dot(a, b, prec=None):
    return lax.dot_general(a, b, (((1,), (0,)), ((), ())), precision=prec, preferred_element_type=F32)


def _dot_nt(a, b, prec=None):
    return lax.dot_general(a, b, (((1,), (1,)), ((), ())), precision=prec, preferred_element_type=F32)


def _dot_tn(a, b, prec=None):
    return lax.dot_general(a, b, (((0,), (0,)), ((), ())), precision=prec, preferred_element_type=F32)


def _iota2(shape, dim):
    return lax.broadcasted_iota(jnp.int32, shape, dim)


def _sigmoid(x):
    return 1.0 / (1.0 + jnp.exp(-x))


def _silu(x):
    return x * _sigmoid(x)


def _softplus(x):
    return jnp.maximum(x, 0.0) + jnp.log(1.0 + jnp.exp(-jnp.abs(x)))


def _params(semantics):
    return pltpu.CompilerParams(dimension_semantics=semantics, vmem_limit_bytes=VMEM_LIMIT)


def _inproj_kernel(x_ref, g_ref, w_ref, o_ref, xn_ref):
    @pl.when(pl.program_id(1) == 0)
    def _():
        x = x_ref[...]
        xn = x * lax.rsqrt(jnp.mean(x * x, axis=-1, keepdims=True) + NORM_EPS) * g_ref[...]
        xn_ref[...] = xn.astype(BF16)

    o_ref[...] = _dot(xn_ref[...], w_ref[...])


def _inproj(x, g, w, *, tm, tn):
    n, d = x.shape
    pw = w.shape[1]
    return pl.pallas_call(
        _inproj_kernel,
        grid=(n // tm, pw // tn),
        in_specs=[pl.BlockSpec((tm, d), lambda i, j: (i, 0)),
                  pl.BlockSpec((1, d), lambda i, j: (0, 0)),
                  pl.BlockSpec((d, tn), lambda i, j: (0, j))],
        out_specs=pl.BlockSpec((tm, tn), lambda i, j: (i, j)),
        out_shape=jax.ShapeDtypeStruct((n, pw), F32),
        scratch_shapes=[pltpu.VMEM((tm, d), BF16)],
        compiler_params=_params(("parallel", "arbitrary")),
        name="inproj",
    )(x, g, w)


def _rwkv_prepare(ps_r, ps_k, ps_v, ps_g, ps_wd, ps_ad, hp, w2, a2, gsum):
    w0, a0, k_k, k_a = hp[0:1], hp[1:2], hp[2:3], hp[3:4]
    w_pre = w0 + _dot(jnp.tanh(ps_wd).astype(BF16), w2)
    w = -_softplus(-w_pre) - 0.5
    log_decay = -jnp.exp(w)
    a = _sigmoid(a0 + _dot(ps_ad.astype(BF16), a2))
    kk = ps_k * k_k
    kk = kk * lax.rsqrt(jnp.maximum(gsum(kk * kk), 1e-24))
    k = ps_k * (1.0 + (a - 1.0) * k_a)
    return ps_r, k, ps_v, ps_g, kk, kk * a, log_decay


def _rwkv_finish(y, r, k, v, g, hp, gsum):
    r_k, lnx_w, lnx_b = hp[4:5], hp[5:6], hp[6:7]
    mean = gsum(y) * (1.0 / HEAD_DIM)
    d = y - mean
    var = gsum(d * d) * (1.0 / HEAD_DIM)
    yn = d * lax.rsqrt(var + RWKV_GN_EPS) * lnx_w + lnx_b
    bonus = gsum(r * k * r_k) * v
    return (yn + bonus) * _silu(g)


def _head_masks(shape):
    lane = _iota2(shape, 1)
    m0 = (lane < HEAD_DIM).astype(F32)
    return m0, 1.0 - m0


def _group_ones():
    i = _iota2((LANES, LANES), 0) // HEAD_DIM
    j = _iota2((LANES, LANES), 1) // HEAD_DIM
    return (i == j).astype(BF16)


def _gsum_wide(x):
    ones = _group_ones()
    hi = x.astype(BF16)
    lo = (x - hi.astype(F32)).astype(BF16)
    parts = []
    for j in range(x.shape[1] // LANES):
        ls = slice(j * LANES, (j + 1) * LANES)
        parts.append(_dot(hi[:, ls], ones) + _dot(lo[:, ls], ones))
    return parts[0] if len(parts) == 1 else jnp.concatenate(parts, axis=1)


def _rwkv_prompt_kernel(pr_ref, pk_ref, pv_ref, pg_ref, pwd_ref, pad_ref,
                        mr_ref, mk_ref, mv_ref, mg_ref, mwd_ref, mad_ref,
                        hp_ref, w2_ref, a2_ref,
                        out_ref, state_ref,
                        carry_ref, cw_ref, m_ref, r_s, k_s, v_s, g_s, kk_s, b_s, ld_s, *, tb, ns):
    t_idx = pl.program_id(2)
    n_t = pl.num_programs(2)
    C = CHUNK

    @pl.when(t_idx == 0)
    def _():
        carry_ref[...] = jnp.zeros_like(carry_ref)
        cw_ref[...] = jnp.zeros_like(cw_ref)
        m_ref[...] = jnp.zeros_like(m_ref)

    def shifted(p_ref, mu_ref, c_ref, idx):
        p = p_ref[0]
        row = _iota2(p.shape, 0)
        prev = pltpu.roll(p, 1, axis=0)
        prev = jnp.where(row == 0, c_ref[idx], prev)
        c_ref[idx] = p[tb - 1:tb, :]
        return p + (prev - p) * mu_ref[...]

    hp = hp_ref[...]
    r, k, v, g, kk, b, ld = _rwkv_prepare(
        shifted(pr_ref, mr_ref, carry_ref, 0), shifted(pk_ref, mk_ref, carry_ref, 1),
        shifted(pv_ref, mv_ref, carry_ref, 2), shifted(pg_ref, mg_ref, carry_ref, 3),
        shifted(pwd_ref, mwd_ref, cw_ref, 0), shifted(pad_ref, mad_ref, cw_ref, 1),
        hp, w2_ref[...], a2_ref[...], _gsum_wide)
    r_s[...] = r
    k_s[...] = k
    v_s[...] = v
    g_s[...] = g
    kk_s[...] = kk
    b_s[...] = b
    ld_s[...] = ld

    m0, m1 = _head_masks((C, LANES))
    ii = _iota2((2 * C, 2 * C), 0)
    jj = _iota2((2 * C, 2 * C), 1)
    strict = ii > jj
    incl = ii >= jj
    eye = ii == jj
    row_c = _iota2((C, LANES), 0)
    first = m0 > 0.5

    def cumsum_rows(x):
        shift = 1
        while shift < C:
            x = x + jnp.where(row_c >= shift, pltpu.roll(x, shift, axis=0), 0.0)
            shift *= 2
        return x

    def head_sums(x):
        s0 = jnp.sum(x * m0, axis=1, keepdims=True)
        s1 = jnp.sum(x * m1, axis=1, keepdims=True)
        return jnp.where(first, s0, s1)

    stack2 = lambda x: jnp.concatenate([x * m0, x * m1], axis=0).astype(BF16)
    bf = lambda x: x.astype(BF16)

    lanes = [slice(s * LANES, (s + 1) * LANES) for s in range(ns)]
    each = lambda f, *cols: [f(*args) for args in zip(*cols)]

    def chunk(ci, ms):
        sl = pl.ds(pl.multiple_of(ci * C, C), C)
        ld = [ld_s[sl, ls] for ls in lanes]
        c = each(cumsum_rows, ld)
        c_end = each(lambda x: x[C - 1:C, :], c)
        p_inv = each(lambda x: jnp.exp(-x), c)
        p_end = each(lambda e, x: jnp.exp(e - x), c_end, c)
        r2 = each(lambda ls, x: stack2(r_s[sl, ls] * jnp.exp(x)), lanes, c)
        kk2 = each(lambda ls, x, d: stack2(kk_s[sl, ls] * jnp.exp(x - d)), lanes, c, ld)
        kh2 = each(lambda ls, p: stack2(k_s[sl, ls] * p), lanes, p_inv)
        bh2 = each(lambda ls, p: stack2(b_s[sl, ls] * p), lanes, p_inv)
        ke2 = each(lambda ls, p: stack2(k_s[sl, ls] * p), lanes, p_end)
        be2 = each(lambda ls, p: stack2(b_s[sl, ls] * p), lanes, p_end)
        v2 = each(lambda ls: stack2(v_s[sl, ls]), lanes)
        a = each(lambda w, x, y, z: _dot_nt(jnp.concatenate([w, x], axis=0), jnp.concatenate([y, z], axis=0)),
                 kk2, r2, kh2, bh2)
        l_k = each(lambda x: bf(jnp.where(strict, x[:2 * C, :2 * C], 0.0)), a)
        l_b = each(lambda x: bf(jnp.where(strict, x[:2 * C, 2 * C:], 0.0)), a)
        a_rk = each(lambda x: bf(jnp.where(incl, x[2 * C:, :2 * C], 0.0)), a)
        a_rb = each(lambda x: bf(jnp.where(incl, -x[2 * C:, 2 * C:], 0.0)), a)
        mb = each(bf, ms)
        x = each(lambda w, lk, m, v: _dot(jnp.concatenate([w, lk], axis=1), jnp.concatenate([m, v], axis=0)),
                 kk2, l_k, mb, v2)
        pw = l_b
        u = each(lambda x_, p: x_ - _dot(p, bf(x_)), x, pw)
        for _ in range(5):
            pw = each(lambda p: bf(_dot(p, p)), pw)
            u = each(lambda u_, p: u_ + _dot(p, bf(u_)), u, pw)
        ub = each(bf, u)
        y2 = each(lambda r_, ak, ab, m, v, u_: _dot(jnp.concatenate([r_, ak, ab], axis=1),
                                                     jnp.concatenate([m, v, u_], axis=0)),
                  r2, a_rk, a_rb, mb, v2, ub)
        p_c_col = each(lambda e: jnp.sum(jnp.where(eye, jnp.broadcast_to(jnp.exp(e), (2 * C, LANES)), 0.0),
                                         axis=1, keepdims=True), c_end)
        m_new = each(lambda m, pc, ke, be, v, u_: m * pc + _dot_tn(jnp.concatenate([ke, -be], axis=0),
                                                                   jnp.concatenate([v, u_], axis=0)),
                     ms, p_c_col, ke2, be2, v2, ub)
        for ls, y in zip(lanes, y2):
            out = _rwkv_finish(y[:C] + y[C:], r_s[sl, ls], k_s[sl, ls], v_s[sl, ls], g_s[sl, ls], hp[:, ls],
                               head_sums)
            out_ref[0, sl, ls] = out.astype(out_ref.dtype)
        return tuple(m_new)

    ms = lax.fori_loop(0, tb // C, chunk, tuple(m_ref[s] for s in range(ns)))
    for s in range(ns):
        m_ref[s] = ms[s]

    @pl.when(t_idx == n_t - 1)
    def _():
        for s in range(ns):
            mt = ms[s].T
            state_ref[0, 2 * s] = mt[:HEAD_DIM, :HEAD_DIM]
            state_ref[0, 2 * s + 1] = mt[HEAD_DIM:, HEAD_DIM:]


def _rwkv_prompt(proj, mu_pad, hparams, w2p, a2p, *, n_heads, rw0, lora0, tb, ns):
    B, T, _ = proj.shape
    n_hp = n_heads // 2
    assert rw0 % ns == 0 and n_hp % ns == 0
    n_grp = n_hp // ns
    wide = ns * LANES
    col = lambda off: (lambda b, h, t: (b, t, off // ns + h))
    colc = lambda off: (lambda b, h, t: (b, t, off))
    mcol = lambda off: (lambda b, h, t: (0, off // ns + h))
    mcolc = lambda off: (lambda b, h, t: (0, off))
    pblk = lambda im: pl.BlockSpec((1, tb, wide), im)
    mblk = lambda im: pl.BlockSpec((1, wide), im)
    in_specs = [pblk(col(rw0)), pblk(col(rw0 + n_hp)), pblk(col(rw0 + 2 * n_hp)), pblk(col(rw0 + 3 * n_hp)),
                pl.BlockSpec((1, tb, LANES), colc(lora0)), pl.BlockSpec((1, tb, LANES), colc(lora0 + 1)),
                mblk(mcol(0)), mblk(mcol(n_hp)), mblk(mcol(2 * n_hp)), mblk(mcol(3 * n_hp)),
                pl.BlockSpec((1, LANES), mcolc(4 * n_hp)), pl.BlockSpec((1, LANES), mcolc(4 * n_hp + 1)),
                pl.BlockSpec((8, wide), lambda b, h, t: (0, h)),
                pl.BlockSpec((LANES, wide), lambda b, h, t: (0, h)),
                pl.BlockSpec((LANES, wide), lambda b, h, t: (0, h))]
    out_specs = [pl.BlockSpec((1, tb, wide), lambda b, h, t: (b, t, h)),
                 pl.BlockSpec((1, 2 * ns, HEAD_DIM, HEAD_DIM), lambda b, h, t: (b, h, 0, 0))]
    scratch = ([pltpu.VMEM((4, 1, wide), F32), pltpu.VMEM((2, 1, LANES), F32), pltpu.VMEM((ns, LANES, LANES), F32)]
               + [pltpu.VMEM((tb, wide), F32)] * 7)
    return pl.pallas_call(
        functools.partial(_rwkv_prompt_kernel, tb=tb, ns=ns),
        grid=(B, n_grp, T // tb),
        in_specs=in_specs, out_specs=out_specs,
        out_shape=[jax.ShapeDtypeStruct((B, T, n_heads * HEAD_DIM), BF16),
                   jax.ShapeDtypeStruct((B, n_heads, HEAD_DIM, HEAD_DIM), F32)],
        scratch_shapes=scratch,
        compiler_params=_params(("parallel", "parallel", "arbitrary")),
        name="rwkv_prompt",
    )(proj, proj, proj, proj, proj, proj, mu_pad, mu_pad, mu_pad, mu_pad, mu_pad, mu_pad, hparams, w2p, a2p)


def _rwkv_params(mu, w0, w2, a0, a2, k_k, k_a, r_k, lnx_w, lnx_b):
    width = w0.shape[0]
    lora = w2.shape[0]
    zpad = jnp.zeros((LANES - lora,), F32)
    mu_pad = jnp.concatenate([mu[:4 * width], mu[4 * width:4 * width + lora], zpad, mu[4 * width + lora:], zpad])[None, :]
    hparams = jnp.stack([w0, a0, k_k, k_a, r_k.reshape(-1), lnx_w, lnx_b, jnp.zeros_like(w0)]).astype(F32)
    pad_rows = lambda m: jnp.concatenate([m, jnp.zeros((LANES - lora, width), m.dtype)], axis=0).astype(BF16)
    return mu_pad.astype(F32), hparams, pad_rows(w2), pad_rows(a2)


def _rwkv_sample_prep_kernel(pr_ref, pk_ref, pv_ref, pg_ref, pwd_ref, pad_ref,
                             qr_ref, qk_ref, qv_ref, qg_ref, qwd_ref, qad_ref,
                             mr_ref, mk_ref, mv_ref, mg_ref, mwd_ref, mad_ref,
                             hp_ref, w2_ref, a2_ref,
                             r_out, k_out, v_out, g_out, kk_out, b_out, ld_out, *, steps):
    def shifted(p_ref, q_ref, mu_ref):
        p = p_ref[...]
        return p + (q_ref[...] - p) * mu_ref[...]

    outs = _rwkv_prepare(
        shifted(pr_ref, qr_ref, mr_ref), shifted(pk_ref, qk_ref, mk_ref), shifted(pv_ref, qv_ref, mv_ref),
        shifted(pg_ref, qg_ref, mg_ref), shifted(pwd_ref, qwd_ref, mwd_ref), shifted(pad_ref, qad_ref, mad_ref),
        hp_ref[...], w2_ref[...], a2_ref[...], _gsum_wide)
    db = r_out.shape[2]
    for o_ref, val in zip((r_out, k_out, v_out, g_out, kk_out, b_out, ld_out), outs):
        for t in range(steps):
            o_ref[t] = val[t * db:(t + 1) * db, :].T


def _rwkv_sample_prep(proj, prev, mu_pad, hparams, w2p, a2p, *, n_heads, rw0, lora0, steps):
    n = proj.shape[0]
    db = n // steps
    n_hp = n_heads // 2
    blk = lambda off, fixed=False: pl.BlockSpec((n, LANES), (lambda h: (0, off)) if fixed else (lambda h: (0, off + h)))
    mblk = lambda off, fixed=False: pl.BlockSpec((1, LANES), (lambda h: (0, off)) if fixed else (lambda h: (0, off + h)))
    offs = [0, n_hp, 2 * n_hp, 3 * n_hp]
    in_specs = ([blk(rw0 + o) for o in offs] + [blk(lora0, True), blk(lora0 + 1, True)]
                + [blk(o) for o in offs] + [blk(4 * n_hp, True), blk(4 * n_hp + 1, True)]
                + [mblk(o) for o in offs] + [mblk(4 * n_hp, True), mblk(4 * n_hp + 1, True)]
                + [pl.BlockSpec((8, LANES), lambda h: (0, h)),
                   pl.BlockSpec((LANES, LANES), lambda h: (0, h)),
                   pl.BlockSpec((LANES, LANES), lambda h: (0, h))])
    width = n_heads * HEAD_DIM
    return pl.pallas_call(
        functools.partial(_rwkv_sample_prep_kernel, steps=steps),
        grid=(n_hp,),
        in_specs=in_specs,
        out_specs=[pl.BlockSpec((steps, LANES, db), lambda h: (0, h, 0))] * 7,
        out_shape=[jax.ShapeDtypeStruct((steps, width, db), F32)] * 7,
        compiler_params=_params(("parallel",)),
        name="rwkv_sample_prep",
    )(*([proj] * 6 + [prev] * 6 + [mu_pad] * 6 + [hparams, w2p, a2p]))


def _rwkv_sample_kernel(r_ref, k_ref, v_ref, g_ref, kk_ref, b_ref, ld_ref, s_ref, hpt_ref, out_ref, so_ref, y_s, *,
                        steps):
    db = out_ref.shape[0] // steps

    for hh in range(2):
        r0 = hh * HEAD_DIM
        rows = slice(r0, r0 + HEAD_DIM)
        kk = [kk_ref[t, rows, :] for t in range(steps)]
        bb = [b_ref[t, rows, :] for t in range(steps)]
        kv = [k_ref[t, rows, :] for t in range(steps)]
        rv = [r_ref[t, rows, :] for t in range(steps)]
        decay = [jnp.exp(ld_ref[t, rows, :]) for t in range(steps)]

        def row_block(v, carry):
            s_v = s_ref[hh, v]
            for t in range(steps):
                s_kk = jnp.sum(s_v * kk[t], axis=0, keepdims=True)
                s_v = s_v * decay[t] - s_kk * bb[t] + v_ref[t, pl.ds(r0 + v, 1), :] * kv[t]
                y_s[t, pl.ds(r0 + v, 1), :] = jnp.sum(s_v * rv[t], axis=0, keepdims=True)
            so_ref[hh, v] = s_v
            return carry

        lax.fori_loop(0, HEAD_DIM, row_block, 0, unroll=2)

    hpt = hpt_ref[...]
    r_k, lnx_w, lnx_b = hpt[:, 4:5], hpt[:, 5:6], hpt[:, 6:7]
    for t in range(steps):
        outs = []
        for hh in range(2):
            rows = slice(hh * HEAD_DIM, (hh + 1) * HEAD_DIM)
            y = y_s[t, rows, :]
            mean = jnp.mean(y, axis=0, keepdims=True)
            d = y - mean
            var = jnp.mean(d * d, axis=0, keepdims=True)
            yn = d * lax.rsqrt(var + RWKV_GN_EPS) * lnx_w[rows] + lnx_b[rows]
            bonus = jnp.sum(r_ref[t, rows, :] * k_ref[t, rows, :] * r_k[rows], axis=0, keepdims=True) * v_ref[t, rows, :]
            outs.append((yn + bonus) * _silu(g_ref[t, rows, :]))
        out_ref[t * db:(t + 1) * db, :] = jnp.concatenate(outs, axis=0).T


def _rwkv_sample(prepped, state, hparams_t, *, n_heads):
    steps, width, db = prepped[0].shape
    tok = pl.BlockSpec((steps, LANES, db), lambda j: (0, j, 0))
    st = pl.BlockSpec((2, HEAD_DIM, HEAD_DIM, db), lambda j: (j, 0, 0, 0))
    return pl.pallas_call(
        functools.partial(_rwkv_sample_kernel, steps=steps),
        grid=(n_heads // 2,),
        in_specs=[tok] * 7 + [st, pl.BlockSpec((LANES, 8), lambda j: (j, 0))],
        out_specs=[pl.BlockSpec((steps * db, LANES), lambda j: (0, j)), st],
        out_shape=[jax.ShapeDtypeStruct((steps * db, width), F32),
                   jax.ShapeDtypeStruct(state.shape, F32)],
        scratch_shapes=[pltpu.VMEM((steps, LANES, db), F32)],
        compiler_params=_params(("parallel",)),
        name="rwkv_sample",
    )(*prepped, state, hparams_t)


def _diff_lambda(lam_ref, lam_init):
    lp = lam_ref[...]
    s1 = jnp.sum(lp[0:1] * lp[1:2], axis=1, keepdims=True)
    s2 = jnp.sum(lp[2:3] * lp[3:4], axis=1, keepdims=True)
    return jnp.exp(s1) - jnp.exp(s2) + lam_init


def _diff_finish(o1, o2, lam, subw, ag, lam_init):
    o = o1 - lam * o2
    o = o * lax.rsqrt(jnp.mean(o * o, axis=-1, keepdims=True) + NORM_EPS)
    return o * subw * (1.0 - lam_init) * _silu(ag)


def _stack_maps(q):
    m0, m1 = _head_masks(q.shape)
    return jnp.concatenate([q * m0, q * m1], axis=0)


def _attn_prompt_kernel(q_ref, k_ref, v_ref, ag_ref, lam_ref, subw_ref, o_ref, kb_s, vb_s, m_s, acc_s, *,
                        tq, tk, sub, lam_init):
    qi = pl.program_id(2)
    seq = k_ref.shape[1]

    @pl.when(qi == 0)
    def _():
        kb_s[...] = k_ref[0].astype(BF16)
        vb_s[:, :LANES] = v_ref[0].astype(BF16)
        vb_s[:, LANES:] = jnp.ones((seq, LANES), BF16)

    q2 = _stack_maps(q_ref[0] * ATTN_SCALE).astype(BF16)
    m_s[...] = jnp.full_like(m_s, NEG_INF)
    acc_s[...] = jnp.zeros_like(acc_s)
    n_sub = 2 * tq // sub

    def block(k0, diag_off):
        kb = kb_s[pl.ds(k0, tk), :]
        vb = vb_s[pl.ds(k0, tk), :]

        def scores(j):
            s = _dot_nt(q2[j * sub:(j + 1) * sub], kb)
            if diag_off is not None:
                q_pos = (_iota2(s.shape, 0) + j * sub) & (tq - 1)
                s = jnp.where(_iota2(s.shape, 1) + diag_off <= q_pos, s, NEG_INF)
            return s

        s_next = scores(0)
        for j in range(n_sub):
            s = s_next
            if j + 1 < n_sub:
                s_next = scores(j + 1)
            rows = slice(j * sub, (j + 1) * sub)
            m_old = m_s[rows]
            m_new = jnp.maximum(m_old, jnp.max(s, axis=1, keepdims=True))
            alpha = jnp.exp(m_old - m_new)
            p = jnp.exp(s - m_new).astype(BF16)
            acc_s[rows] = alpha * acc_s[rows] + _dot(p, vb)
            m_s[rows] = m_new

    def body(ki, carry):
        block(pl.multiple_of(ki * tk, tk), None)
        return carry

    lax.fori_loop(0, qi * (tq // tk), body, 0)
    for d in range(tq // tk):
        block(pl.multiple_of(qi * tq + d * tk, tk), d * tk)
    acc = acc_s[...]
    o = acc[:, :LANES] / acc[:, LANES:]
    lam = _diff_lambda(lam_ref, lam_init)
    o_ref[0] = _diff_finish(o[:tq], o[tq:], lam, subw_ref[...], ag_ref[0], lam_init).astype(o_ref.dtype)


def _attn_prompt(proj, lam_params, subw, *, n_heads, at0, tq, lam_init):
    B, T, _ = proj.shape
    qblk = lambda off: pl.BlockSpec((1, tq, LANES), lambda b, h, i: (b, i, off + h))
    kblk = lambda off: pl.BlockSpec((1, T, LANES), lambda b, h, i: (b, 0, off + h))
    return pl.pallas_call(
        functools.partial(_attn_prompt_kernel, tq=tq, tk=tq, sub=min(256, 2 * tq), lam_init=lam_init),
        grid=(B, n_heads, T // tq),
        in_specs=[qblk(at0), kblk(at0 + n_heads), kblk(at0 + 2 * n_heads), qblk(at0 + 3 * n_heads),
                  pl.BlockSpec(lam_params.shape, lambda b, h, i: (0, 0)),
                  pl.BlockSpec((1, LANES), lambda b, h, i: (0, 0))],
        out_specs=pl.BlockSpec((1, tq, LANES), lambda b, h, i: (b, i, h)),
        out_shape=jax.ShapeDtypeStruct((B, T, n_heads * LANES), BF16),
        scratch_shapes=[pltpu.VMEM((T, LANES), BF16), pltpu.VMEM((T, 2 * LANES), BF16),
                        pltpu.VMEM((2 * tq, 1), F32), pltpu.VMEM((2 * tq, 2 * LANES), F32)],
        compiler_params=_params(("parallel", "parallel", "arbitrary")),
        name="attn_prompt",
    )(proj, proj, proj, proj, lam_params, subw)


def _attn_sample_kernel(pt_ref, q_ref, kn_ref, vn_ref, ag_ref, *refs, n_heads, steps, pages, lam_init):
    del pt_ref
    k_refs, v_refs = refs[:pages], refs[pages:2 * pages]
    lam_ref, subw_ref, o_ref, m_s, l_s, acc_s = refs[2 * pages:]
    gi = pl.program_id(1)
    n_groups = pl.num_programs(1)
    rows_h = 2 * steps
    rows = n_heads * rows_h
    kdim, page = k_refs[0].shape[1], k_refs[0].shape[2]
    assert steps & (steps - 1) == 0

    @pl.when(gi == 0)
    def _():
        m_s[...] = jnp.full_like(m_s, NEG_INF)
        l_s[...] = jnp.zeros_like(l_s)
        acc_s[...] = jnp.zeros_like(acc_s)

    q_all = q_ref[0] * ATTN_SCALE
    hs = lambda h: slice(h * LANES, (h + 1) * LANES)
    q_rep = jnp.concatenate([q_all] * (rows // steps), axis=0)
    col_grp = lax.shift_right_logical(_iota2((rows, kdim), 1), int(math.log2(HEAD_DIM)))
    row_grp = lax.shift_right_logical(_iota2((rows, kdim), 0), int(math.log2(steps)))
    q_bd = jnp.where(col_grp == row_grp, q_rep, 0.0).astype(BF16)

    s = jnp.concatenate([_dot(q_bd, k_refs[i][0].astype(BF16)) for i in range(pages)], axis=1)
    m_old = m_s[...]
    m_run = jnp.maximum(m_old, jnp.max(s, axis=1, keepdims=True))
    alpha = jnp.exp(m_old - m_run)
    p = jnp.exp(s - m_run)
    l_run = alpha * l_s[...] + jnp.sum(p, axis=1, keepdims=True)
    pv = []
    for h in range(n_heads):
        v_h = jnp.concatenate([v_refs[i][0, pl.ds(h, page, stride=n_heads), :].astype(BF16) for i in range(pages)],
                              axis=0)
        pv.append(_dot(p[h * rows_h:(h + 1) * rows_h].astype(BF16), v_h))
    acc = alpha * acc_s[...] + jnp.concatenate(pv, axis=0)
    m_s[...] = m_run
    l_s[...] = l_run
    acc_s[...] = acc

    @pl.when(gi == n_groups - 1)
    def _():
        lam = _diff_lambda(lam_ref, lam_init)
        tok = _iota2((rows_h, LANES), 0) & (steps - 1)
        lane = _iota2((rows_h, LANES), 1)
        for h in range(n_heads):
            rs = slice(h * rows_h, (h + 1) * rows_h)
            q2 = _stack_maps(q_all[:, hs(h)])
            kn = kn_ref[0][:, hs(h)]
            vn = vn_ref[0][:, hs(h)]
            s = jnp.full((rows_h, LANES), NEG_INF, F32)
            for j in range(steps):
                sj = jnp.sum(q2 * kn[j:j + 1, :], axis=1, keepdims=True)
                s = jnp.where((lane == j) & (tok >= j), sj, s)
            m_old = m_run[rs]
            m_new = jnp.maximum(m_old, jnp.max(s, axis=1, keepdims=True))
            alpha = jnp.exp(m_old - m_new)
            p = jnp.exp(s - m_new)
            l_fin = alpha * l_run[rs] + jnp.sum(p, axis=1, keepdims=True)
            a_fin = alpha * acc[rs]
            for j in range(steps):
                a_fin = a_fin + p[:, j:j + 1] * vn[j:j + 1, :]
            o = a_fin / l_fin
            o_ref[0, :, hs(h)] = _diff_finish(o[:steps], o[steps:], lam, subw_ref[...], ag_ref[0][:, hs(h)], lam_init)


def _attn_sample(qkvg, cache_k, cache_v, page_table, lam_params, subw, *, n_heads, pages, lam_init):
    db, steps, _ = qkvg.shape
    n_pages = page_table.shape[0] // db
    n_groups = n_pages // pages
    width = n_heads * LANES
    rows = n_heads * 2 * steps
    tokblk = lambda off: pl.BlockSpec((1, steps, width), lambda b, g, pt: (b, 0, off))
    pageblk = lambda cache, i: pl.BlockSpec((1,) + cache.shape[1:],
                                            lambda b, g, pt: (pt[(b * n_groups + g) * pages + i], 0, 0))
    grid_spec = pltpu.PrefetchScalarGridSpec(
        num_scalar_prefetch=1,
        grid=(db, n_groups),
        in_specs=([tokblk(0), tokblk(1), tokblk(2), tokblk(3)]
                  + [pageblk(cache_k, i) for i in range(pages)] + [pageblk(cache_v, i) for i in range(pages)]
                  + [pl.BlockSpec(lam_params.shape, lambda b, g, pt: (0, 0)),
                     pl.BlockSpec((1, LANES), lambda b, g, pt: (0, 0))]),
        out_specs=pl.BlockSpec((1, steps, width), lambda b, g, pt: (b, 0, 0)),
        scratch_shapes=[pltpu.VMEM((rows, 1), F32), pltpu.VMEM((rows, 1), F32), pltpu.VMEM((rows, LANES), F32)])
    return pl.pallas_call(
        functools.partial(_attn_sample_kernel, n_heads=n_heads, steps=steps, pages=pages, lam_init=lam_init),
        grid_spec=grid_spec,
        out_shape=jax.ShapeDtypeStruct((db, steps, width), F32),
        compiler_params=_params(("parallel", "arbitrary")),
        name="attn_sample",
    )(page_table, qkvg, qkvg, qkvg, qkvg, *([cache_k] * pages), *([cache_v] * pages), lam_params, subw)


def _outproj_kernel(x_ref, rw_ref, at_ref, grw_ref, gat_ref, wbr_ref, wba_ref, wo_ref, nf_ref, o_ref):
    merged = (_sigmoid(grw_ref[...]) * _dot(rw_ref[...], wbr_ref[...])
              + _sigmoid(gat_ref[...]) * _dot(at_ref[...], wba_ref[...]))
    y = x_ref[...] + _dot(merged.astype(BF16), wo_ref[...])
    o_ref[...] = y * lax.rsqrt(jnp.mean(y * y, axis=-1, keepdims=True) + NORM_EPS) * nf_ref[...]


def _outproj(x, rw, at, proj, wbr, wba, wo, nf, *, tm):
    n, d = x.shape
    row = lambda w: pl.BlockSpec((tm, w), lambda i: (i, 0))
    const = lambda a: pl.BlockSpec(a.shape, lambda i: (0, 0), pipeline_mode=pl.Buffered(1))
    return pl.pallas_call(
        _outproj_kernel,
        grid=(n // tm,),
        in_specs=[row(d), row(rw.shape[1]), row(at.shape[1]), row(d),
                  pl.BlockSpec((tm, d), lambda i: (i, 1)),
                  const(wbr), const(wba), const(wo), const(nf)],
        out_specs=row(d),
        out_shape=jax.ShapeDtypeStruct((n, d), F32),
        compiler_params=_params(("parallel",)),
        name="outproj",
    )(x, rw, at, proj, proj, wbr, wba, wo, nf)


def _largest_tile(n, cap):
    t = cap
    while n % t:
        t //= 2
    return t


def kernel(x_prompt, x_sample, cache_k, cache_v, state_wkv, state_shift, page_table, norm_in, w_in, mu_shift, w0, w2, a0, a2, k_k, k_a, r_k, lnx_w, lnx_b, lambda_q1, lambda_k1, lambda_q2, lambda_k2, subln_w, w_br_rwkv, w_br_attn, w_out, norm_f):
    depth = w_in.shape[0]
    assert depth == 1, "single-layer step"
    B, T, D = x_prompt.shape
    DB, TS, _ = x_sample.shape
    r_heads = r_k.shape[1]
    r_width = r_heads * HEAD_DIM
    lora = w2.shape[1]
    a_heads = cache_k.shape[3]
    a_width = a_heads * LANES
    rw_in = 4 * r_width + 2 * lora
    lam_init = 0.8 - 0.6 * math.exp(-0.3 * 0)

    w = w_in[0]
    zcol = jnp.zeros((D, LANES - lora), w.dtype)
    w_pad = jnp.concatenate([w[:, rw_in + 4 * a_width:], w[:, :4 * r_width], w[:, rw_in:rw_in + 4 * a_width],
                             w[:, 4 * r_width:4 * r_width + lora], zcol, w[:, 4 * r_width + lora:rw_in], zcol],
                            axis=1).astype(BF16)
    rw0 = 2 * D // LANES
    at0 = rw0 + 4 * r_width // LANES
    lora0 = at0 + 4 * a_width // LANES
    pw = w_pad.shape[1]

    mu_pad, hparams, w2p, a2p = _rwkv_params(mu_shift[0], w0[0], w2[0], a0[0], a2[0], k_k[0], k_a[0], r_k[0],
                                             lnx_w[0], lnx_b[0])
    lam_params = jnp.stack([lambda_q1[0], lambda_k1[0], lambda_q2[0], lambda_k2[0]]).astype(F32)
    subw = subln_w[0][None, :].astype(F32)
    g_in = norm_in[0][None, :].astype(F32)
    wbr = w_br_rwkv[0].astype(BF16)
    wba = w_br_attn[0].astype(BF16)
    wo = w_out[0].astype(BF16)
    nf = norm_f[None, :].astype(F32)
    tn = 14 * LANES
    assert pw % tn == 0

    n_p = B * T
    xp = x_prompt.reshape(n_p, D)
    proj_p = _inproj(xp, g_in, w_pad, tm=_largest_tile(n_p, 1024), tn=tn)
    proj_p3 = proj_p.reshape(B, T, pw)
    rw_p, wkv_p = _rwkv_prompt(proj_p3, mu_pad, hparams, w2p, a2p, n_heads=r_heads, rw0=rw0, lora0=lora0,
                               tb=_largest_tile(T, 512), ns=_largest_tile(r_heads // 2, 8))
    at_p = _attn_prompt(proj_p3, lam_params, subw, n_heads=a_heads, at0=at0, tq=_largest_tile(T, 512),
                        lam_init=lam_init)
    y_p = _outproj(xp, rw_p.reshape(n_p, r_width), at_p.reshape(n_p, a_width), proj_p, wbr, wba, wo, nf,
                   tm=_largest_tile(n_p, 256))

    n_s = DB * TS
    xs = jnp.transpose(x_sample, (1, 0, 2)).reshape(n_s, D)
    proj_s = _inproj(xs, g_in, w_pad, tm=_largest_tile(n_s, 512), tn=tn)
    c0 = rw0 * LANES
    zs = jnp.zeros((DB, LANES - lora), F32)
    sh = state_shift[0]
    shift_pad = jnp.concatenate([sh[:, :4 * r_width], sh[:, 4 * r_width:4 * r_width + lora], zs,
                                 sh[:, 4 * r_width + lora:], zs], axis=-1)
    rwkv_cols = jnp.concatenate([proj_s[:, c0:c0 + 4 * r_width], proj_s[:, lora0 * LANES:]], axis=-1)
    prev = jnp.concatenate([shift_pad, rwkv_cols[:-DB]], axis=0)
    prepped = _rwkv_sample_prep(proj_s, prev, mu_pad, hparams, w2p, a2p, n_heads=r_heads, rw0=rw0, lora0=lora0,
                                steps=TS)
    rw_s, wkv_s = _rwkv_sample(prepped, jnp.transpose(state_wkv[0], (1, 2, 3, 0)), hparams.T, n_heads=r_heads)
    wkv_s = jnp.transpose(wkv_s, (3, 0, 1, 2))
    n_pool, page = cache_k.shape[1], cache_k.shape[2]
    n_pages = page_table.shape[1]
    q0 = at0 * LANES
    qkvg_s = jnp.transpose(proj_s[:, q0:q0 + 4 * a_width].reshape(TS, DB, 4 * a_width), (1, 0, 2))
    kc = jnp.transpose(cache_k[0], (0, 2, 3, 4, 1)).reshape(n_pool, a_width, page)
    at_s = _attn_sample(qkvg_s, kc, cache_v[0].reshape(n_pool, page * a_heads, LANES), page_table.reshape(-1),
                        lam_params, subw, n_heads=a_heads, pages=_largest_tile(n_pages, 16), lam_init=lam_init)
    at_s = jnp.transpose(at_s, (1, 0, 2)).reshape(n_s, a_width)
    y_s = _outproj(xs, rw_s.astype(BF16), at_s.astype(BF16), proj_s, wbr, wba, wo, nf, tm=_largest_tile(n_s, 256))
    y_s = jnp.transpose(y_s.reshape(TS, DB, D), (1, 0, 2))

    k0 = (at0 + a_heads) * LANES
    v0 = (at0 + 2 * a_heads) * LANES

    def shift_row(last):
        return jnp.concatenate([last[:, c0:c0 + 4 * r_width], last[:, lora0 * LANES:lora0 * LANES + lora],
                                last[:, (lora0 + 1) * LANES:(lora0 + 1) * LANES + lora]], axis=-1)[None]

    return (y_p.reshape(B, T, D), y_s,
            proj_p3[:, :, k0:k0 + a_width].reshape(1, B, T, a_heads, 2, HEAD_DIM),
            proj_p3[:, :, v0:v0 + a_width].reshape(1, B, T, a_heads, LANES),
            qkvg_s[:, :, a_width:2 * a_width].reshape(1, DB, TS, a_heads, 2, HEAD_DIM),
            qkvg_s[:, :, 2 * a_width:3 * a_width].reshape(1, DB, TS, a_heads, LANES),
            wkv_p[None], wkv_s[None], shift_row(proj_p3[:, -1]), shift_row(proj_s[-DB:]))
```

```python
import functools
import math

import jax
import jax.numpy as jnp
from jax import lax
from jax.experimental import pallas as pl
from jax.experimental.pallas import tpu as pltpu

F32 = jnp.float32
BF16 = jnp.bfloat16

LANES = 128
HEAD_DIM = 64
CHUNK = 64
NORM_EPS = 1e-6
RWKV_GN_EPS = 64e-5
NEG_INF = -1e30
ATTN_SCALE = HEAD_DIM ** -0.5
VMEM_LIMIT = 56 * 1024 * 1024

HI = lax.Precision.HIGHEST


def _dot(a, b, prec=None):
    return lax.dot_general(a, b, (((1,), (0,)), ((), ())), precision=prec, preferred_element_type=F32)


def _dot_nt(a, b, prec=None):
    return lax.dot_general(a, b, (((1,), (1,)), ((), ())), precision=prec, preferred_element_type=F32)


def _dot_tn(a, b, prec=None):
    return lax.dot_general(a, b, (((0,), (0,)), ((), ())), precision=prec, preferred_element_type=F32)


def _iota2(shape, dim):
    return lax.broadcasted_iota(jnp.int32, shape, dim)


def _sigmoid(x):
    return 1.0 / (1.0 + jnp.exp(-x))


def _silu(x):
    return x * _sigmoid(x)


def _params(semantics):
    return pltpu.CompilerParams(dimension_semantics=semantics, vmem_limit_bytes=VMEM_LIMIT)


def _regroup_kernel(w_ref, o_ref, *, segments):
    col = 0
    for start, width, pad in segments:
        o_ref[:, col:col + width] = w_ref[:, start:start + width].astype(o_ref.dtype)
        col += width
        if pad:
            o_ref[:, col:col + pad] = jnp.zeros((o_ref.shape[0], pad), o_ref.dtype)
            col += pad


def _regroup_weights(w, segments, *, tr):
    d, cols = w.shape
    out_cols = sum(width + pad for _, width, pad in segments)
    return pl.pallas_call(
        functools.partial(_regroup_kernel, segments=segments),
        grid=(d // tr,),
        in_specs=[pl.BlockSpec((tr, cols), lambda i: (i, 0))],
        out_specs=pl.BlockSpec((tr, out_cols), lambda i: (i, 0)),
        out_shape=jax.ShapeDtypeStruct((d, out_cols), BF16),
        compiler_params=_params(("parallel",)),
        name="regroup_weights",
    )(w)


def _inproj_kernel(x_ref, g_ref, w_ref, o_ref, xn_ref):
    @pl.when(pl.program_id(1) == 0)
    def _():
        x = x_ref[...]
        xn = x * lax.rsqrt(jnp.mean(x * x, axis=-1, keepdims=True) + NORM_EPS) * g_ref[...]
        xn_ref[...] = xn.astype(BF16)

    o_ref[...] = _dot(xn_ref[...], w_ref[...])


def _inproj(x, g, w, *, tm, tn):
    n, d = x.shape
    pw = w.shape[1]
    return pl.pallas_call(
        _inproj_kernel,
        grid=(n // tm, pw // tn),
        in_specs=[pl.BlockSpec((tm, d), lambda i, j: (i, 0)),
                  pl.BlockSpec((1, d), lambda i, j: (0, 0)),
                  pl.BlockSpec((d, tn), lambda i, j: (0, j))],
        out_specs=pl.BlockSpec((tm, tn), lambda i, j: (i, j)),
        out_shape=jax.ShapeDtypeStruct((n, pw), F32),
        scratch_shapes=[pltpu.VMEM((tm, d), BF16)],
        compiler_params=_params(("parallel", "arbitrary")),
        name="inproj",
    )(x, g, w)


def _rwkv_prepare(ps_r, ps_k, ps_v, ps_g, ps_wd, ps_ad, hp, w2, a2, gsum):
    w0, a0, k_k, k_a = hp[0:1], hp[1:2], hp[2:3], hp[3:4]
    w_pre = w0 + _dot(jnp.tanh(ps_wd).astype(BF16), w2)
    log_decay = -math.exp(-0.5) * _sigmoid(w_pre)
    a = _sigmoid(a0 + _dot(ps_ad.astype(BF16), a2))
    kk = ps_k * k_k
    kk = kk * lax.rsqrt(jnp.maximum(gsum(kk * kk), 1e-24))
    k = ps_k * (1.0 + (a - 1.0) * k_a)
    return ps_r, k, ps_v, ps_g, kk, kk * a, log_decay


def _rwkv_finish(y, r, k, v, g, hp, gsum):
    r_k, lnx_w, lnx_b = hp[4:5], hp[5:6], hp[6:7]
    mean = gsum(y) * (1.0 / HEAD_DIM)
    d = y - mean
    var = gsum(d * d) * (1.0 / HEAD_DIM)
    yn = d * lax.rsqrt(var + RWKV_GN_EPS) * lnx_w + lnx_b
    bonus = gsum(r * k * r_k) * v
    return (yn + bonus) * _silu(g)


def _head_masks(shape):
    lane = _iota2(shape, 1)
    m0 = (lane < HEAD_DIM).astype(F32)
    return m0, 1.0 - m0


def _group_ones():
    i = _iota2((LANES, LANES), 0) // HEAD_DIM
    j = _iota2((LANES, LANES), 1) // HEAD_DIM
    return (i == j).astype(BF16)


def _gsum_wide(x):
    ones = _group_ones()
    hi = x.astype(BF16)
    lo = (x - hi.astype(F32)).astype(BF16)
    parts = []
    for j in range(x.shape[1] // LANES):
        ls = slice(j * LANES, (j + 1) * LANES)
        parts.append(_dot(hi[:, ls], ones) + _dot(lo[:, ls], ones))
    return parts[0] if len(parts) == 1 else jnp.concatenate(parts, axis=1)


def _rwkv_prompt_kernel(pr_ref, pk_ref, pv_ref, pg_ref, pwd_ref, pad_ref,
                        mr_ref, mk_ref, mv_ref, mg_ref, mwd_ref, mad_ref,
                        hp_ref, w2_ref, a2_ref,
                        out_ref, state_ref,
                        carry_ref, cw_ref, m_ref, r_s, k_s, v_s, g_s, kk_s, b_s, ld_s, *, tb, ns):
    t_idx = pl.program_id(2)
    n_t = pl.num_programs(2)
    C = CHUNK

    @pl.when(t_idx == 0)
    def _():
        carry_ref[...] = jnp.zeros_like(carry_ref)
        cw_ref[...] = jnp.zeros_like(cw_ref)
        m_ref[...] = jnp.zeros_like(m_ref)

    def shifted(p_ref, mu_ref, c_ref, idx):
        p = p_ref[0]
        row = _iota2(p.shape, 0)
        prev = pltpu.roll(p, 1, axis=0)
        prev = jnp.where(row == 0, c_ref[idx], prev)
        c_ref[idx] = p[tb - 1:tb, :]
        return p + (prev - p) * mu_ref[...]

    hp = hp_ref[...]
    r, k, v, g, kk, b, ld = _rwkv_prepare(
        shifted(pr_ref, mr_ref, carry_ref, 0), shifted(pk_ref, mk_ref, carry_ref, 1),
        shifted(pv_ref, mv_ref, carry_ref, 2), shifted(pg_ref, mg_ref, carry_ref, 3),
        shifted(pwd_ref, mwd_ref, cw_ref, 0), shifted(pad_ref, mad_ref, cw_ref, 1),
        hp, w2_ref[...], a2_ref[...], _gsum_wide)
    r_s[...] = r
    k_s[...] = k
    v_s[...] = v
    g_s[...] = g
    kk_s[...] = kk
    b_s[...] = b
    ld_s[...] = ld

    m0, m1 = _head_masks((C, LANES))
    ii = _iota2((2 * C, 2 * C), 0)
    jj = _iota2((2 * C, 2 * C), 1)
    strict = ii > jj
    incl = ii >= jj
    eye = ii == jj
    row_c = _iota2((C, LANES), 0)
    first = m0 > 0.5

    def cumsum_rows(x):
        shift = 1
        while shift < C:
            x = x + jnp.where(row_c >= shift, pltpu.roll(x, shift, axis=0), 0.0)
            shift *= 2
        return x

    def head_sums(x):
        s0 = jnp.sum(x * m0, axis=1, keepdims=True)
        s1 = jnp.sum(x * m1, axis=1, keepdims=True)
        return jnp.where(first, s0, s1)

    stack2 = lambda x: jnp.concatenate([x * m0, x * m1], axis=0).astype(BF16)
    bf = lambda x: x.astype(BF16)

    lanes = [slice(s * LANES, (s + 1) * LANES) for s in range(ns)]
    each = lambda f, *cols: [f(*args) for args in zip(*cols)]

    def chunk(ci, ms):
        sl = pl.ds(pl.multiple_of(ci * C, C), C)
        ld = [ld_s[sl, ls] for ls in lanes]
        c = each(cumsum_rows, ld)
        c_end = each(lambda x: x[C - 1:C, :], c)
        p_inv = each(lambda x: jnp.exp(-x), c)
        p_end = each(lambda e, x: jnp.exp(e - x), c_end, c)
        r2 = each(lambda ls, x: stack2(r_s[sl, ls] * jnp.exp(x)), lanes, c)
        kk2 = each(lambda ls, x, d: stack2(kk_s[sl, ls] * jnp.exp(x - d)), lanes, c, ld)
        kh2 = each(lambda ls, p: stack2(k_s[sl, ls] * p), lanes, p_inv)
        bh2 = each(lambda ls, p: stack2(b_s[sl, ls] * p), lanes, p_inv)
        ke2 = each(lambda ls, p: stack2(k_s[sl, ls] * p), lanes, p_end)
        be2 = each(lambda ls, p: stack2(b_s[sl, ls] * p), lanes, p_end)
        v2 = each(lambda ls: stack2(v_s[sl, ls]), lanes)
        a = each(lambda w, x, y, z: _dot_nt(jnp.concatenate([w, x], axis=0), jnp.concatenate([y, z], axis=0)),
                 kk2, r2, kh2, bh2)
        l_k = each(lambda x: bf(jnp.where(strict, x[:2 * C, :2 * C], 0.0)), a)
        l_b = each(lambda x: bf(jnp.where(strict, x[:2 * C, 2 * C:], 0.0)), a)
        a_rk = each(lambda x: bf(jnp.where(incl, x[2 * C:, :2 * C], 0.0)), a)
        a_rb = each(lambda x: bf(jnp.where(incl, -x[2 * C:, 2 * C:], 0.0)), a)
        mb = each(bf, ms)
        x = each(lambda w, lk, m, v: _dot(jnp.concatenate([w, lk], axis=1), jnp.concatenate([m, v], axis=0)),
                 kk2, l_k, mb, v2)
        pw = l_b
        u = each(lambda x_, p: x_ - _dot(p, bf(x_)), x, pw)
        for _ in range(5):
            pw = each(lambda p: bf(_dot(p, p)), pw)
            u = each(lambda u_, p: u_ + _dot(p, bf(u_)), u, pw)
        ub = each(bf, u)
        y2 = each(lambda r_, ak, ab, m, v, u_: _dot(jnp.concatenate([r_, ak, ab], axis=1),
                                                     jnp.concatenate([m, v, u_], axis=0)),
                  r2, a_rk, a_rb, mb, v2, ub)
        p_c_col = each(lambda e: jnp.sum(jnp.where(eye, jnp.broadcast_to(jnp.exp(e), (2 * C, LANES)), 0.0),
                                         axis=1, keepdims=True), c_end)
        m_new = each(lambda m, pc, ke, be, v, u_: m * pc + _dot_tn(jnp.concatenate([ke, -be], axis=0),
                                                                   jnp.concatenate([v, u_], axis=0)),
                     ms, p_c_col, ke2, be2, v2, ub)
        for ls, y in zip(lanes, y2):
            out = _rwkv_finish(y[:C] + y[C:], r_s[sl, ls], k_s[sl, ls], v_s[sl, ls], g_s[sl, ls], hp[:, ls],
                               head_sums)
            out_ref[0, sl, ls] = out.astype(out_ref.dtype)
        return tuple(m_new)

    ms = lax.fori_loop(0, tb // C, chunk, tuple(m_ref[s] for s in range(ns)))
    for s in range(ns):
        m_ref[s] = ms[s]

    @pl.when(t_idx == n_t - 1)
    def _():
        for s in range(ns):
            mt = ms[s].T
            state_ref[0, 2 * s] = mt[:HEAD_DIM, :HEAD_DIM]
            state_ref[0, 2 * s + 1] = mt[HEAD_DIM:, HEAD_DIM:]


def _rwkv_prompt(proj, mu_pad, hparams, w2p, a2p, *, n_heads, rw0, lora0, tb, ns):
    B, T, _ = proj.shape
    n_hp = n_heads // 2
    assert rw0 % ns == 0 and n_hp % ns == 0
    n_grp = n_hp // ns
    wide = ns * LANES
    col = lambda off: (lambda b, h, t: (b, t, off // ns + h))
    colc = lambda off: (lambda b, h, t: (b, t, off))
    mcol = lambda off: (lambda b, h, t: (0, off // ns + h))
    mcolc = lambda off: (lambda b, h, t: (0, off))
    pblk = lambda im: pl.BlockSpec((1, tb, wide), im)
    mblk = lambda im: pl.BlockSpec((1, wide), im)
    in_specs = [pblk(col(rw0)), pblk(col(rw0 + n_hp)), pblk(col(rw0 + 2 * n_hp)), pblk(col(rw0 + 3 * n_hp)),
                pl.BlockSpec((1, tb, LANES), colc(lora0)), pl.BlockSpec((1, tb, LANES), colc(lora0 + 1)),
                mblk(mcol(0)), mblk(mcol(n_hp)), mblk(mcol(2 * n_hp)), mblk(mcol(3 * n_hp)),
                pl.BlockSpec((1, LANES), mcolc(4 * n_hp)), pl.BlockSpec((1, LANES), mcolc(4 * n_hp + 1)),
                pl.BlockSpec((8, wide), lambda b, h, t: (0, h)),
                pl.BlockSpec((LANES, wide), lambda b, h, t: (0, h)),
                pl.BlockSpec((LANES, wide), lambda b, h, t: (0, h))]
    out_specs = [pl.BlockSpec((1, tb, wide), lambda b, h, t: (b, t, h)),
                 pl.BlockSpec((1, 2 * ns, HEAD_DIM, HEAD_DIM), lambda b, h, t: (b, h, 0, 0))]
    scratch = ([pltpu.VMEM((4, 1, wide), F32), pltpu.VMEM((2, 1, LANES), F32), pltpu.VMEM((ns, LANES, LANES), F32)]
               + [pltpu.VMEM((tb, wide), F32)] * 7)
    return pl.pallas_call(
        functools.partial(_rwkv_prompt_kernel, tb=tb, ns=ns),
        grid=(B, n_grp, T // tb),
        in_specs=in_specs, out_specs=out_specs,
        out_shape=[jax.ShapeDtypeStruct((B, T, n_heads * HEAD_DIM), BF16),
                   jax.ShapeDtypeStruct((B, n_heads, HEAD_DIM, HEAD_DIM), F32)],
        scratch_shapes=scratch,
        compiler_params=_params(("parallel", "parallel", "arbitrary")),
        name="rwkv_prompt",
    )(proj, proj, proj, proj, proj, proj, mu_pad, mu_pad, mu_pad, mu_pad, mu_pad, mu_pad, hparams, w2p, a2p)


def _rwkv_params(mu, w0, w2, a0, a2, k_k, k_a, r_k, lnx_w, lnx_b):
    width = w0.shape[0]
    lora = w2.shape[0]
    zpad = jnp.zeros((LANES - lora,), F32)
    mu_pad = jnp.concatenate([mu[:4 * width], mu[4 * width:4 * width + lora], zpad, mu[4 * width + lora:], zpad])[None, :]
    hparams = jnp.stack([w0, a0, k_k, k_a, r_k.reshape(-1), lnx_w, lnx_b, jnp.zeros_like(w0)]).astype(F32)
    pad_rows = lambda m: jnp.concatenate([m, jnp.zeros((LANES - lora, width), m.dtype)], axis=0).astype(BF16)
    return mu_pad.astype(F32), hparams, pad_rows(w2), pad_rows(a2)


def _rwkv_sample_prep_kernel(pr_ref, pk_ref, pv_ref, pg_ref, pwd_ref, pad_ref,
                             qr_ref, qk_ref, qv_ref, qg_ref, qwd_ref, qad_ref,
                             mr_ref, mk_ref, mv_ref, mg_ref, mwd_ref, mad_ref,
                             hp_ref, w2_ref, a2_ref,
                             r_out, k_out, v_out, g_out, kk_out, b_out, ld_out, *, steps):
    def shifted(p_ref, q_ref, mu_ref):
        p = p_ref[...]
        return p + (q_ref[...] - p) * mu_ref[...]

    outs = _rwkv_prepare(
        shifted(pr_ref, qr_ref, mr_ref), shifted(pk_ref, qk_ref, mk_ref), shifted(pv_ref, qv_ref, mv_ref),
        shifted(pg_ref, qg_ref, mg_ref), shifted(pwd_ref, qwd_ref, mwd_ref), shifted(pad_ref, qad_ref, mad_ref),
        hp_ref[...], w2_ref[...], a2_ref[...], _gsum_wide)
    db = r_out.shape[2]
    for o_ref, val in zip((r_out, k_out, v_out, g_out, kk_out, b_out, ld_out), outs):
        for t in range(steps):
            o_ref[t] = val[t * db:(t + 1) * db, :].T


def _rwkv_sample_prep(proj, prev, mu_pad, hparams, w2p, a2p, *, n_heads, rw0, lora0, steps):
    n = proj.shape[0]
    db = n // steps
    n_hp = n_heads // 2
    blk = lambda off, fixed=False: pl.BlockSpec((n, LANES), (lambda h: (0, off)) if fixed else (lambda h: (0, off + h)))
    mblk = lambda off, fixed=False: pl.BlockSpec((1, LANES), (lambda h: (0, off)) if fixed else (lambda h: (0, off + h)))
    offs = [0, n_hp, 2 * n_hp, 3 * n_hp]
    in_specs = ([blk(rw0 + o) for o in offs] + [blk(lora0, True), blk(lora0 + 1, True)]
                + [blk(o) for o in offs] + [blk(4 * n_hp, True), blk(4 * n_hp + 1, True)]
                + [mblk(o) for o in offs] + [mblk(4 * n_hp, True), mblk(4 * n_hp + 1, True)]
                + [pl.BlockSpec((8, LANES), lambda h: (0, h)),
                   pl.BlockSpec((LANES, LANES), lambda h: (0, h)),
                   pl.BlockSpec((LANES, LANES), lambda h: (0, h))])
    width = n_heads * HEAD_DIM
    return pl.pallas_call(
        functools.partial(_rwkv_sample_prep_kernel, steps=steps),
        grid=(n_hp,),
        in_specs=in_specs,
        out_specs=[pl.BlockSpec((steps, LANES, db), lambda h: (0, h, 0))] * 7,
        out_shape=[jax.ShapeDtypeStruct((steps, width, db), F32)] * 7,
        compiler_params=_params(("parallel",)),
        name="rwkv_sample_prep",
    )(*([proj] * 6 + [prev] * 6 + [mu_pad] * 6 + [hparams, w2p, a2p]))


def _rwkv_sample_kernel(r_ref, k_ref, v_ref, g_ref, kk_ref, b_ref, ld_ref, s_ref, hpt_ref, out_ref, so_ref, y_s, *,
                        steps):
    db = out_ref.shape[0] // steps

    for hh in range(2):
        r0 = hh * HEAD_DIM
        rows = slice(r0, r0 + HEAD_DIM)
        kk = [kk_ref[t, rows, :] for t in range(steps)]
        bb = [b_ref[t, rows, :] for t in range(steps)]
        kv = [k_ref[t, rows, :] for t in range(steps)]
        rv = [r_ref[t, rows, :] for t in range(steps)]
        decay = [jnp.exp(ld_ref[t, rows, :]) for t in range(steps)]

        def row_block(v, carry):
            s_v = s_ref[hh, v]
            for t in range(steps):
                s_kk = jnp.sum(s_v * kk[t], axis=0, keepdims=True)
                s_v = s_v * decay[t] - s_kk * bb[t] + v_ref[t, pl.ds(r0 + v, 1), :] * kv[t]
                y_s[t, pl.ds(r0 + v, 1), :] = jnp.sum(s_v * rv[t], axis=0, keepdims=True)
            so_ref[hh, v] = s_v
            return carry

        lax.fori_loop(0, HEAD_DIM, row_block, 0, unroll=2)

    hpt = hpt_ref[...]
    r_k, lnx_w, lnx_b = hpt[:, 4:5], hpt[:, 5:6], hpt[:, 6:7]
    for t in range(steps):
        outs = []
        for hh in range(2):
            rows = slice(hh * HEAD_DIM, (hh + 1) * HEAD_DIM)
            y = y_s[t, rows, :]
            mean = jnp.mean(y, axis=0, keepdims=True)
            d = y - mean
            var = jnp.mean(d * d, axis=0, keepdims=True)
            yn = d * lax.rsqrt(var + RWKV_GN_EPS) * lnx_w[rows] + lnx_b[rows]
            bonus = jnp.sum(r_ref[t, rows, :] * k_ref[t, rows, :] * r_k[rows], axis=0, keepdims=True) * v_ref[t, rows, :]
            outs.append((yn + bonus) * _silu(g_ref[t, rows, :]))
        out_ref[t * db:(t + 1) * db, :] = jnp.concatenate(outs, axis=0).T


def _rwkv_sample(prepped, state, hparams_t, *, n_heads):
    steps, width, db = prepped[0].shape
    tok = pl.BlockSpec((steps, LANES, db), lambda j: (0, j, 0))
    st = pl.BlockSpec((2, HEAD_DIM, HEAD_DIM, db), lambda j: (j, 0, 0, 0))
    return pl.pallas_call(
        functools.partial(_rwkv_sample_kernel, steps=steps),
        grid=(n_heads // 2,),
        in_specs=[tok] * 7 + [st, pl.BlockSpec((LANES, 8), lambda j: (j, 0))],
        out_specs=[pl.BlockSpec((steps * db, LANES), lambda j: (0, j)), st],
        out_shape=[jax.ShapeDtypeStruct((steps * db, width), F32),
                   jax.ShapeDtypeStruct(state.shape, F32)],
        scratch_shapes=[pltpu.VMEM((steps, LANES, db), F32)],
        compiler_params=_params(("parallel",)),
        name="rwkv_sample",
    )(*prepped, state, hparams_t)


def _diff_lambda(lam_ref, lam_init):
    lp = lam_ref[...]
    s1 = jnp.sum(lp[0:1] * lp[1:2], axis=1, keepdims=True)
    s2 = jnp.sum(lp[2:3] * lp[3:4], axis=1, keepdims=True)
    return jnp.exp(s1) - jnp.exp(s2) + lam_init


def _diff_finish(o1, o2, lam, subw, ag, lam_init):
    o = o1 - lam * o2
    o = o * lax.rsqrt(jnp.mean(o * o, axis=-1, keepdims=True) + NORM_EPS)
    return o * subw * (1.0 - lam_init) * _silu(ag)


def _stack_maps(q):
    m0, m1 = _head_masks(q.shape)
    return jnp.concatenate([q * m0, q * m1], axis=0)


def _attn_prompt_kernel(q_ref, k_ref, v_ref, ag_ref, lam_ref, subw_ref, o_ref, ko_ref, vo_ref,
                        kb_s, vb_s, m_s, acc_s, *,
                        tq, tk, sub, lam_init):
    qi = pl.program_id(2)
    seq = k_ref.shape[1]

    @pl.when(qi == 0)
    def _():
        ko_ref[...] = k_ref[...]
        vo_ref[...] = v_ref[...]
        kb_s[...] = k_ref[0].astype(BF16)
        vb_s[:, :LANES] = v_ref[0].astype(BF16)
        vb_s[:, LANES:] = jnp.ones((seq, LANES), BF16)

    q2 = _stack_maps(q_ref[0] * ATTN_SCALE).astype(BF16)
    m_s[...] = jnp.full_like(m_s, NEG_INF)
    acc_s[...] = jnp.zeros_like(acc_s)
    n_sub = 2 * tq // sub

    def block(k0, diag_off):
        def width(j):
            if diag_off is None:
                return tk
            return max(0, min(tk, (j * sub) % tq + sub - diag_off))

        def scores(j):
            kw = width(j)
            if kw == 0:
                return None
            s = _dot_nt(q2[j * sub:(j + 1) * sub], kb_s[pl.ds(k0, kw), :])
            if diag_off is not None:
                q_pos = (_iota2(s.shape, 0) + j * sub) & (tq - 1)
                s = jnp.where(_iota2(s.shape, 1) + diag_off <= q_pos, s, NEG_INF)
            return s

        s_next = scores(0)
        for j in range(n_sub):
            s = s_next
            if j + 1 < n_sub:
                s_next = scores(j + 1)
            if s is None:
                continue
            rows = slice(j * sub, (j + 1) * sub)
            m_old = m_s[rows]
            m_new = jnp.maximum(m_old, jnp.max(s, axis=1, keepdims=True))
            alpha = jnp.exp(m_old - m_new)
            p = jnp.exp(s - m_new).astype(BF16)
            acc_s[rows] = alpha * acc_s[rows] + _dot(p, vb_s[pl.ds(k0, width(j)), :])
            m_s[rows] = m_new

    def body(ki, carry):
        block(pl.multiple_of(ki * tk, tk), None)
        return carry

    lax.fori_loop(0, qi * (tq // tk), body, 0)
    for d in range(tq // tk):
        block(pl.multiple_of(qi * tq + d * tk, tk), d * tk)
    acc = acc_s[...]
    o = acc[:, :LANES] / acc[:, LANES:]
    lam = _diff_lambda(lam_ref, lam_init)
    o_ref[0] = _diff_finish(o[:tq], o[tq:], lam, subw_ref[...], ag_ref[0], lam_init).astype(o_ref.dtype)


def _attn_prompt(proj, lam_params, subw, *, n_heads, at0, tq, lam_init):
    B, T, _ = proj.shape
    qblk = lambda off: pl.BlockSpec((1, tq, LANES), lambda b, h, i: (b, i, off + h))
    kblk = lambda off: pl.BlockSpec((1, T, LANES), lambda b, h, i: (b, 0, off + h))
    return pl.pallas_call(
        functools.partial(_attn_prompt_kernel, tq=tq, tk=tq, sub=min(256, 2 * tq), lam_init=lam_init),
        grid=(B, n_heads, T // tq),
        in_specs=[qblk(at0), kblk(at0 + n_heads), kblk(at0 + 2 * n_heads), qblk(at0 + 3 * n_heads),
                  pl.BlockSpec(lam_params.shape, lambda b, h, i: (0, 0)),
                  pl.BlockSpec((1, LANES), lambda b, h, i: (0, 0))],
        out_specs=[pl.BlockSpec((1, tq, LANES), lambda b, h, i: (b, i, h)),
                   pl.BlockSpec((1, T, LANES), lambda b, h, i: (b, 0, h)),
                   pl.BlockSpec((1, T, LANES), lambda b, h, i: (b, 0, h))],
        out_shape=[jax.ShapeDtypeStruct((B, T, n_heads * LANES), BF16),
                   jax.ShapeDtypeStruct((B, T, n_heads * LANES), F32),
                   jax.ShapeDtypeStruct((B, T, n_heads * LANES), F32)],
        scratch_shapes=[pltpu.VMEM((T, LANES), BF16), pltpu.VMEM((T, 2 * LANES), BF16),
                        pltpu.VMEM((2 * tq, 1), F32), pltpu.VMEM((2 * tq, 2 * LANES), F32)],
        compiler_params=_params(("parallel", "parallel", "arbitrary")),
        name="attn_prompt",
    )(proj, proj, proj, proj, lam_params, subw)


def _attn_sample_kernel(pt_ref, q_ref, kn_ref, vn_ref, ag_ref, *refs, n_heads, steps, pages, lam_init):
    del pt_ref
    k_refs, v_refs = refs[:pages], refs[pages:2 * pages]
    lam_ref, subw_ref, o_ref, m_s, l_s, acc_s = refs[2 * pages:]
    gi = pl.program_id(1)
    n_groups = pl.num_programs(1)
    rows_h = 2 * steps
    rows = n_heads * rows_h
    kdim, page = k_refs[0].shape[1], k_refs[0].shape[2]
    assert steps & (steps - 1) == 0

    @pl.when(gi == 0)
    def _():
        m_s[...] = jnp.full_like(m_s, NEG_INF)
        l_s[...] = jnp.zeros_like(l_s)
        acc_s[...] = jnp.zeros_like(acc_s)

    q_all = q_ref[0] * ATTN_SCALE
    hs = lambda h: slice(h * LANES, (h + 1) * LANES)
    q_rep = jnp.concatenate([q_all] * (rows // steps), axis=0)
    col_grp = lax.shift_right_logical(_iota2((rows, kdim), 1), int(math.log2(HEAD_DIM)))
    row_grp = lax.shift_right_logical(_iota2((rows, kdim), 0), int(math.log2(steps)))
    q_bd = jnp.where(col_grp == row_grp, q_rep, 0.0).astype(BF16)

    s = jnp.concatenate([_dot(q_bd, k_refs[i][0].astype(BF16)) for i in range(pages)], axis=1)
    m_old = m_s[...]
    m_run = jnp.maximum(m_old, jnp.max(s, axis=1, keepdims=True))
    alpha = jnp.exp(m_old - m_run)
    p = jnp.exp(s - m_run)
    l_run = alpha * l_s[...] + jnp.sum(p, axis=1, keepdims=True)
    pv = []
    for h in range(n_heads):
        v_h = jnp.concatenate([v_refs[i][0, pl.ds(h, page, stride=n_heads), :].astype(BF16) for i in range(pages)],
                              axis=0)
        pv.append(_dot(p[h * rows_h:(h + 1) * rows_h].astype(BF16), v_h))
    acc = alpha * acc_s[...] + jnp.concatenate(pv, axis=0)
    m_s[...] = m_run
    l_s[...] = l_run
    acc_s[...] = acc

    @pl.when(gi == n_groups - 1)
    def _():
        lam = _diff_lambda(lam_ref, lam_init)
        tok = _iota2((rows_h, LANES), 0) & (steps - 1)
        lane = _iota2((rows_h, LANES), 1)
        for h in range(n_heads):
            rs = slice(h * rows_h, (h + 1) * rows_h)
            q2 = _stack_maps(q_all[:, hs(h)])
            kn = kn_ref[0][:, hs(h)]
            vn = vn_ref[0][:, hs(h)]
            s = jnp.full((rows_h, LANES), NEG_INF, F32)
            for j in range(steps):
                sj = jnp.sum(q2 * kn[j:j + 1, :], axis=1, keepdims=True)
                s = jnp.where((lane == j) & (tok >= j), sj, s)
            m_old = m_run[rs]
            m_new = jnp.maximum(m_old, jnp.max(s, axis=1, keepdims=True))
            alpha = jnp.exp(m_old - m_new)
            p = jnp.exp(s - m_new)
            l_fin = alpha * l_run[rs] + jnp.sum(p, axis=1, keepdims=True)
            a_fin = alpha * acc[rs]
            for j in range(steps):
                a_fin = a_fin + p[:, j:j + 1] * vn[j:j + 1, :]
            o = a_fin / l_fin
            o_ref[0, :, hs(h)] = _diff_finish(o[:steps], o[steps:], lam, subw_ref[...], ag_ref[0][:, hs(h)], lam_init)


def _attn_sample(qkvg, cache_k, cache_v, page_table, lam_params, subw, *, n_heads, pages, lam_init):
    db, steps, _ = qkvg.shape
    n_pages = page_table.shape[0] // db
    n_groups = n_pages // pages
    width = n_heads * LANES
    rows = n_heads * 2 * steps
    tokblk = lambda off: pl.BlockSpec((1, steps, width), lambda b, g, pt: (b, 0, off))
    pageblk = lambda cache, i: pl.BlockSpec((1,) + cache.shape[1:],
                                            lambda b, g, pt: (pt[(b * n_groups + g) * pages + i], 0, 0))
    grid_spec = pltpu.PrefetchScalarGridSpec(
        num_scalar_prefetch=1,
        grid=(db, n_groups),
        in_specs=([tokblk(0), tokblk(1), tokblk(2), tokblk(3)]
                  + [pageblk(cache_k, i) for i in range(pages)] + [pageblk(cache_v, i) for i in range(pages)]
                  + [pl.BlockSpec(lam_params.shape, lambda b, g, pt: (0, 0)),
                     pl.BlockSpec((1, LANES), lambda b, g, pt: (0, 0))]),
        out_specs=pl.BlockSpec((1, steps, width), lambda b, g, pt: (b, 0, 0)),
        scratch_shapes=[pltpu.VMEM((rows, 1), F32), pltpu.VMEM((rows, 1), F32), pltpu.VMEM((rows, LANES), F32)])
    return pl.pallas_call(
        functools.partial(_attn_sample_kernel, n_heads=n_heads, steps=steps, pages=pages, lam_init=lam_init),
        grid_spec=grid_spec,
        out_shape=jax.ShapeDtypeStruct((db, steps, width), F32),
        compiler_params=_params(("parallel", "arbitrary")),
        name="attn_sample",
    )(page_table, qkvg, qkvg, qkvg, qkvg, *([cache_k] * pages), *([cache_v] * pages), lam_params, subw)


def _outproj_kernel(x_ref, rw_ref, at_ref, grw_ref, gat_ref, wbr_ref, wba_ref, wo_ref, nf_ref, o_ref):
    merged = (_sigmoid(grw_ref[...]) * _dot(rw_ref[...], wbr_ref[...])
              + _sigmoid(gat_ref[...]) * _dot(at_ref[...], wba_ref[...]))
    y = x_ref[...] + _dot(merged.astype(BF16), wo_ref[...])
    o_ref[...] = y * lax.rsqrt(jnp.mean(y * y, axis=-1, keepdims=True) + NORM_EPS) * nf_ref[...]


def _outproj(x, rw, at, proj, wbr, wba, wo, nf, *, tm):
    n, d = x.shape
    row = lambda w: pl.BlockSpec((tm, w), lambda i: (i, 0))
    const = lambda a: pl.BlockSpec(a.shape, lambda i: (0, 0), pipeline_mode=pl.Buffered(1))
    return pl.pallas_call(
        _outproj_kernel,
        grid=(n // tm,),
        in_specs=[row(d), row(rw.shape[1]), row(at.shape[1]), row(d),
                  pl.BlockSpec((tm, d), lambda i: (i, 1)),
                  const(wbr), const(wba), const(wo), const(nf)],
        out_specs=row(d),
        out_shape=jax.ShapeDtypeStruct((n, d), F32),
        compiler_params=_params(("parallel",)),
        name="outproj",
    )(x, rw, at, proj, proj, wbr, wba, wo, nf)


def _largest_tile(n, cap):
    t = cap
    while n % t:
        t //= 2
    return t


def kernel(x_prompt, x_sample, cache_k, cache_v, state_wkv, state_shift, page_table, norm_in, w_in, mu_shift, w0, w2, a0, a2, k_k, k_a, r_k, lnx_w, lnx_b, lambda_q1, lambda_k1, lambda_q2, lambda_k2, subln_w, w_br_rwkv, w_br_attn, w_out, norm_f):
    depth = w_in.shape[0]
    assert depth == 1, "single-layer step"
    B, T, D = x_prompt.shape
    DB, TS, _ = x_sample.shape
    r_heads = r_k.shape[1]
    r_width = r_heads * HEAD_DIM
    lora = w2.shape[1]
    a_heads = cache_k.shape[3]
    a_width = a_heads * LANES
    rw_in = 4 * r_width + 2 * lora
    lam_init = 0.8 - 0.6 * math.exp(-0.3 * 0)

    segments = ((rw_in + 4 * a_width, 2 * D, 0), (0, 4 * r_width, 0), (rw_in, 4 * a_width, 0),
                (4 * r_width, lora, LANES - lora), (4 * r_width + lora, lora, LANES - lora))
    w_pad = _regroup_weights(w_in[0], segments, tr=_largest_tile(D, 128))
    rw0 = 2 * D // LANES
    at0 = rw0 + 4 * r_width // LANES
    lora0 = at0 + 4 * a_width // LANES
    pw = w_pad.shape[1]

    mu_pad, hparams, w2p, a2p = _rwkv_params(mu_shift[0], w0[0], w2[0], a0[0], a2[0], k_k[0], k_a[0], r_k[0],
                                             lnx_w[0], lnx_b[0])
    lam_params = jnp.stack([lambda_q1[0], lambda_k1[0], lambda_q2[0], lambda_k2[0]]).astype(F32)
    subw = subln_w[0][None, :].astype(F32)
    g_in = norm_in[0][None, :].astype(F32)
    wbr = w_br_rwkv[0].astype(BF16)
    wba = w_br_attn[0].astype(BF16)
    wo = w_out[0].astype(BF16)
    nf = norm_f[None, :].astype(F32)
    tn = 14 * LANES
    assert pw % tn == 0

    n_p = B * T
    xp = x_prompt.reshape(n_p, D)
    proj_p = _inproj(xp, g_in, w_pad, tm=_largest_tile(n_p, 1024), tn=tn)
    proj_p3 = proj_p.reshape(B, T, pw)
    rw_p, wkv_p = _rwkv_prompt(proj_p3, mu_pad, hparams, w2p, a2p, n_heads=r_heads, rw0=rw0, lora0=lora0,
                               tb=_largest_tile(T, 512), ns=_largest_tile(r_heads // 2, 8))
    at_p, k_p, v_p = _attn_prompt(proj_p3, lam_params, subw, n_heads=a_heads, at0=at0, tq=_largest_tile(T, 512),
                                  lam_init=lam_init)
    y_p = _outproj(xp, rw_p.reshape(n_p, r_width), at_p.reshape(n_p, a_width), proj_p, wbr, wba, wo, nf,
                   tm=_largest_tile(n_p, 256))

    n_s = DB * TS
    xs = jnp.transpose(x_sample, (1, 0, 2)).reshape(n_s, D)
    proj_s = _inproj(xs, g_in, w_pad, tm=_largest_tile(n_s, 512), tn=tn)
    c0 = rw0 * LANES
    zs = jnp.zeros((DB, LANES - lora), F32)
    sh = state_shift[0]
    shift_pad = jnp.concatenate([sh[:, :4 * r_width], sh[:, 4 * r_width:4 * r_width + lora], zs,
                                 sh[:, 4 * r_width + lora:], zs], axis=-1)
    rwkv_cols = jnp.concatenate([proj_s[:, c0:c0 + 4 * r_width], proj_s[:, lora0 * LANES:]], axis=-1)
    prev = jnp.concatenate([shift_pad, rwkv_cols[:-DB]], axis=0)
    prepped = _rwkv_sample_prep(proj_s, prev, mu_pad, hparams, w2p, a2p, n_heads=r_heads, rw0=rw0, lora0=lora0,
                                steps=TS)
    rw_s, wkv_s = _rwkv_sample(prepped, jnp.transpose(state_wkv[0], (1, 2, 3, 0)), hparams.T, n_heads=r_heads)
    wkv_s = jnp.transpose(wkv_s, (3, 0, 1, 2))
    n_pool, page = cache_k.shape[1], cache_k.shape[2]
    n_pages = page_table.shape[1]
    q0 = at0 * LANES
    qkvg_s = jnp.transpose(proj_s[:, q0:q0 + 4 * a_width].reshape(TS, DB, 4 * a_width), (1, 0, 2))
    kc = jnp.transpose(cache_k[0], (0, 2, 3, 4, 1)).reshape(n_pool, a_width, page)
    at_s = _attn_sample(qkvg_s, kc, cache_v[0].reshape(n_pool, page * a_heads, LANES), page_table.reshape(-1),
                        lam_params, subw, n_heads=a_heads, pages=_largest_tile(n_pages, 16), lam_init=lam_init)
    at_s = jnp.transpose(at_s, (1, 0, 2)).reshape(n_s, a_width)
    y_s = _outproj(xs, rw_s.astype(BF16), at_s.astype(BF16), proj_s, wbr, wba, wo, nf, tm=_largest_tile(n_s, 256))
    y_s = jnp.transpose(y_s.reshape(TS, DB, D), (1, 0, 2))

    def shift_row(last):
        return jnp.concatenate([last[:, c0:c0 + 4 * r_width], last[:, lora0 * LANES:lora0 * LANES + lora],
                                last[:, (lora0 + 1) * LANES:(lora0 + 1) * LANES + lora]], axis=-1)[None]

    return (y_p.reshape(B, T, D), y_s,
            k_p.reshape(1, B, T, a_heads, 2, HEAD_DIM), v_p.reshape(1, B, T, a_heads, LANES),
            qkvg_s[:, :, a_width:2 * a_width].reshape(1, DB, TS, a_heads, 2, HEAD_DIM),
            qkvg_s[:, :, 2 * a_width:3 * a_width].reshape(1, DB, TS, a_heads, LANES),
            wkv_p[None], wkv_s[None], shift_row(proj_p3[:, -1]), shift_row(proj_s[-DB:]))
```

```python
import functools
import math

import jax
import jax.numpy as jnp
from jax import lax
from jax.experimental import pallas as pl
from jax.experimental.pallas import tpu as pltpu

F32 = jnp.float32
BF16 = jnp.bfloat16

LANES = 128
HEAD_DIM = 64
CHUNK = 64
NORM_EPS = 1e-6
RWKV_GN_EPS = 64e-5
NEG_INF = -1e30
ATTN_SCALE = HEAD_DIM ** -0.5
VMEM_LIMIT = 56 * 1024 * 1024

HI = lax.Precision.HIGHEST


def _dot(a, b, prec=None):
    return lax.dot_general(a, b, (((1,), (0,)), ((), ())), precision=prec, preferred_element_type=F32)


def _dot_nt(a, b, prec=None):
    return lax.dot_general(a, b, (((1,), (1,)), ((), ())), precision=prec, preferred_element_type=F32)


def _dot_tn(a, b, prec=None):
    return lax.dot_general(a, b, (((0,), (0,)), ((), ())), precision=prec, preferred_element_type=F32)


def _iota2(shape, dim):
    return lax.broadcasted_iota(jnp.int32, shape, dim)


def _sigmoid(x):
    return 1.0 / (1.0 + jnp.exp(-x))


def _silu(x):
    return x * _sigmoid(x)


def _params(semantics):
    return pltpu.CompilerParams(dimension_semantics=semantics, vmem_limit_bytes=VMEM_LIMIT)


def _regroup_kernel(w_ref, o_ref, *, segments):
    row = 0
    for start, width, pad in segments:
        o_ref[row:row + width, :] = w_ref[start:start + width, :].astype(o_ref.dtype)
        row += width
        if pad:
            o_ref[row:row + pad, :] = jnp.zeros((pad, o_ref.shape[1]), o_ref.dtype)
            row += pad


def _regroup_weights(w_t, segments, *, tc):
    cols, d = w_t.shape
    out_cols = sum(width + pad for _, width, pad in segments)
    return pl.pallas_call(
        functools.partial(_regroup_kernel, segments=segments),
        grid=(d // tc,),
        in_specs=[pl.BlockSpec((cols, tc), lambda i: (0, i))],
        out_specs=pl.BlockSpec((out_cols, tc), lambda i: (0, i)),
        out_shape=jax.ShapeDtypeStruct((out_cols, d), BF16),
        compiler_params=_params(("parallel",)),
        name="regroup_weights",
    )(w_t)


def _inproj_kernel(x_ref, g_ref, w_ref, o_ref, xn_ref):
    @pl.when(pl.program_id(1) == 0)
    def _():
        x = x_ref[...]
        xn = x * lax.rsqrt(jnp.mean(x * x, axis=-1, keepdims=True) + NORM_EPS) * g_ref[...]
        xn_ref[...] = xn.astype(BF16)

    o_ref[...] = _dot_nt(xn_ref[...], w_ref[...])


def _inproj(x, g, w_t, *, tm, tn):
    n, d = x.shape
    pw = w_t.shape[0]
    return pl.pallas_call(
        _inproj_kernel,
        grid=(n // tm, pw // tn),
        in_specs=[pl.BlockSpec((tm, d), lambda i, j: (i, 0)),
                  pl.BlockSpec((1, d), lambda i, j: (0, 0)),
                  pl.BlockSpec((tn, d), lambda i, j: (j, 0))],
        out_specs=pl.BlockSpec((tm, tn), lambda i, j: (i, j)),
        out_shape=jax.ShapeDtypeStruct((n, pw), F32),
        scratch_shapes=[pltpu.VMEM((tm, d), BF16)],
        compiler_params=_params(("parallel", "arbitrary")),
        name="inproj",
    )(x, g, w_t)


def _rwkv_prepare(ps_r, ps_k, ps_v, ps_g, ps_wd, ps_ad, hp, w2, a2, gsum):
    w0, a0, k_k, k_a = hp[0:1], hp[1:2], hp[2:3], hp[3:4]
    w_pre = w0 + _dot(jnp.tanh(ps_wd).astype(BF16), w2)
    log_decay = -math.exp(-0.5) * _sigmoid(w_pre)
    a = _sigmoid(a0 + _dot(ps_ad.astype(BF16), a2))
    kk = ps_k * k_k
    kk = kk * lax.rsqrt(jnp.maximum(gsum(kk * kk), 1e-24))
    k = ps_k * (1.0 + (a - 1.0) * k_a)
    return ps_r, k, ps_v, ps_g, kk, kk * a, log_decay


def _rwkv_finish(y, r, k, v, g, hp, gsum):
    r_k, lnx_w, lnx_b = hp[4:5], hp[5:6], hp[6:7]
    mean = gsum(y) * (1.0 / HEAD_DIM)
    d = y - mean
    var = gsum(d * d) * (1.0 / HEAD_DIM)
    yn = d * lax.rsqrt(var + RWKV_GN_EPS) * lnx_w + lnx_b
    bonus = gsum(r * k * r_k) * v
    return (yn + bonus) * _silu(g)


def _head_masks(shape):
    lane = _iota2(shape, 1)
    m0 = (lane < HEAD_DIM).astype(F32)
    return m0, 1.0 - m0


def _group_ones():
    i = _iota2((LANES, LANES), 0) // HEAD_DIM
    j = _iota2((LANES, LANES), 1) // HEAD_DIM
    return (i == j).astype(BF16)


def _gsum_wide(x):
    ones = _group_ones()
    hi = x.astype(BF16)
    lo = (x - hi.astype(F32)).astype(BF16)
    parts = []
    for j in range(x.shape[1] // LANES):
        ls = slice(j * LANES, (j + 1) * LANES)
        parts.append(_dot(hi[:, ls], ones) + _dot(lo[:, ls], ones))
    return parts[0] if len(parts) == 1 else jnp.concatenate(parts, axis=1)


def _rwkv_prompt_kernel(pr_ref, pk_ref, pv_ref, pg_ref, pwd_ref, pad_ref,
                        mr_ref, mk_ref, mv_ref, mg_ref, mwd_ref, mad_ref,
                        hp_ref, w2_ref, a2_ref,
                        out_ref, state_ref,
                        carry_ref, cw_ref, m_ref, r_s, k_s, v_s, g_s, kk_s, b_s, ld_s, *, tb, ns):
    t_idx = pl.program_id(2)
    n_t = pl.num_programs(2)
    C = CHUNK

    @pl.when(t_idx == 0)
    def _():
        carry_ref[...] = jnp.zeros_like(carry_ref)
        cw_ref[...] = jnp.zeros_like(cw_ref)
        m_ref[...] = jnp.zeros_like(m_ref)

    def shifted(p_ref, mu_ref, c_ref, idx):
        p = p_ref[0]
        row = _iota2(p.shape, 0)
        prev = pltpu.roll(p, 1, axis=0)
        prev = jnp.where(row == 0, c_ref[idx], prev)
        c_ref[idx] = p[tb - 1:tb, :]
        return p + (prev - p) * mu_ref[...]

    hp = hp_ref[...]
    r, k, v, g, kk, b, ld = _rwkv_prepare(
        shifted(pr_ref, mr_ref, carry_ref, 0), shifted(pk_ref, mk_ref, carry_ref, 1),
        shifted(pv_ref, mv_ref, carry_ref, 2), shifted(pg_ref, mg_ref, carry_ref, 3),
        shifted(pwd_ref, mwd_ref, cw_ref, 0), shifted(pad_ref, mad_ref, cw_ref, 1),
        hp, w2_ref[...], a2_ref[...], _gsum_wide)
    r_s[...] = r
    k_s[...] = k
    v_s[...] = v
    g_s[...] = g
    kk_s[...] = kk
    b_s[...] = b
    ld_s[...] = ld

    m0, m1 = _head_masks((C, LANES))
    ii = _iota2((2 * C, 2 * C), 0)
    jj = _iota2((2 * C, 2 * C), 1)
    strict = ii > jj
    incl = ii >= jj
    eye = ii == jj
    row_c = _iota2((C, LANES), 0)
    first = m0 > 0.5

    def cumsum_rows(x):
        shift = 1
        while shift < C:
            x = x + jnp.where(row_c >= shift, pltpu.roll(x, shift, axis=0), 0.0)
            shift *= 2
        return x

    def head_sums(x):
        s0 = jnp.sum(x * m0, axis=1, keepdims=True)
        s1 = jnp.sum(x * m1, axis=1, keepdims=True)
        return jnp.where(first, s0, s1)

    stack2 = lambda x: jnp.concatenate([x * m0, x * m1], axis=0).astype(BF16)
    bf = lambda x: x.astype(BF16)

    lanes = [slice(s * LANES, (s + 1) * LANES) for s in range(ns)]
    each = lambda f, *cols: [f(*args) for args in zip(*cols)]

    def chunk(ci, ms):
        sl = pl.ds(pl.multiple_of(ci * C, C), C)
        ld = [ld_s[sl, ls] for ls in lanes]
        c = each(cumsum_rows, ld)
        c_end = each(lambda x: x[C - 1:C, :], c)
        p_inv = each(lambda x: jnp.exp(-x), c)
        p_end = each(lambda e, x: jnp.exp(e - x), c_end, c)
        r2 = each(lambda ls, x: stack2(r_s[sl, ls] * jnp.exp(x)), lanes, c)
        kk2 = each(lambda ls, x, d: stack2(kk_s[sl, ls] * jnp.exp(x - d)), lanes, c, ld)
        kh2 = each(lambda ls, p: stack2(k_s[sl, ls] * p), lanes, p_inv)
        bh2 = each(lambda ls, p: stack2(b_s[sl, ls] * p), lanes, p_inv)
        ke2 = each(lambda ls, p: stack2(k_s[sl, ls] * p), lanes, p_end)
        be2 = each(lambda ls, p: stack2(b_s[sl, ls] * p), lanes, p_end)
        v2 = each(lambda ls: stack2(v_s[sl, ls]), lanes)
        a = each(lambda w, x, y, z: _dot_nt(jnp.concatenate([w, x], axis=0), jnp.concatenate([y, z], axis=0)),
                 kk2, r2, kh2, bh2)
        l_k = each(lambda x: bf(jnp.where(strict, x[:2 * C, :2 * C], 0.0)), a)
        l_b = each(lambda x: bf(jnp.where(strict, x[:2 * C, 2 * C:], 0.0)), a)
        a_rk = each(lambda x: bf(jnp.where(incl, x[2 * C:, :2 * C], 0.0)), a)
        a_rb = each(lambda x: bf(jnp.where(incl, -x[2 * C:, 2 * C:], 0.0)), a)
        mb = each(bf, ms)
        x = each(lambda w, lk, m, v: _dot(jnp.concatenate([w, lk], axis=1), jnp.concatenate([m, v], axis=0)),
                 kk2, l_k, mb, v2)
        pw = l_b
        u = each(lambda x_, p: x_ - _dot(p, bf(x_)), x, pw)
        for _ in range(5):
            pw = each(lambda p: bf(_dot(p, p)), pw)
            u = each(lambda u_, p: u_ + _dot(p, bf(u_)), u, pw)
        ub = each(bf, u)
        y2 = each(lambda r_, ak, ab, m, v, u_: _dot(jnp.concatenate([r_, ak, ab], axis=1),
                                                     jnp.concatenate([m, v, u_], axis=0)),
                  r2, a_rk, a_rb, mb, v2, ub)
        p_c_col = each(lambda e: jnp.sum(jnp.where(eye, jnp.broadcast_to(jnp.exp(e), (2 * C, LANES)), 0.0),
                                         axis=1, keepdims=True), c_end)
        m_new = each(lambda m, pc, ke, be, v, u_: m * pc + _dot_tn(jnp.concatenate([ke, -be], axis=0),
                                                                   jnp.concatenate([v, u_], axis=0)),
                     ms, p_c_col, ke2, be2, v2, ub)
        for ls, y in zip(lanes, y2):
            out = _rwkv_finish(y[:C] + y[C:], r_s[sl, ls], k_s[sl, ls], v_s[sl, ls], g_s[sl, ls], hp[:, ls],
                               head_sums)
            out_ref[0, sl, ls] = out.astype(out_ref.dtype)
        return tuple(m_new)

    ms = lax.fori_loop(0, tb // C, chunk, tuple(m_ref[s] for s in range(ns)))
    for s in range(ns):
        m_ref[s] = ms[s]

    @pl.when(t_idx == n_t - 1)
    def _():
        for s in range(ns):
            mt = ms[s].T
            state_ref[0, 2 * s] = mt[:HEAD_DIM, :HEAD_DIM]
            state_ref[0, 2 * s + 1] = mt[HEAD_DIM:, HEAD_DIM:]


def _rwkv_prompt(proj, mu_pad, hparams, w2p, a2p, *, n_heads, rw0, lora0, tb, ns):
    B, T, _ = proj.shape
    n_hp = n_heads // 2
    assert rw0 % ns == 0 and n_hp % ns == 0
    n_grp = n_hp // ns
    wide = ns * LANES
    col = lambda off: (lambda b, h, t: (b, t, off // ns + h))
    colc = lambda off: (lambda b, h, t: (b, t, off))
    mcol = lambda off: (lambda b, h, t: (0, off // ns + h))
    mcolc = lambda off: (lambda b, h, t: (0, off))
    pblk = lambda im: pl.BlockSpec((1, tb, wide), im)
    mblk = lambda im: pl.BlockSpec((1, wide), im)
    in_specs = [pblk(col(rw0)), pblk(col(rw0 + n_hp)), pblk(col(rw0 + 2 * n_hp)), pblk(col(rw0 + 3 * n_hp)),
                pl.BlockSpec((1, tb, LANES), colc(lora0)), pl.BlockSpec((1, tb, LANES), colc(lora0 + 1)),
                mblk(mcol(0)), mblk(mcol(n_hp)), mblk(mcol(2 * n_hp)), mblk(mcol(3 * n_hp)),
                pl.BlockSpec((1, LANES), mcolc(4 * n_hp)), pl.BlockSpec((1, LANES), mcolc(4 * n_hp + 1)),
                pl.BlockSpec((8, wide), lambda b, h, t: (0, h)),
                pl.BlockSpec((LANES, wide), lambda b, h, t: (0, h)),
                pl.BlockSpec((LANES, wide), lambda b, h, t: (0, h))]
    out_specs = [pl.BlockSpec((1, tb, wide), lambda b, h, t: (b, t, h)),
                 pl.BlockSpec((1, 2 * ns, HEAD_DIM, HEAD_DIM), lambda b, h, t: (b, h, 0, 0))]
    scratch = ([pltpu.VMEM((4, 1, wide), F32), pltpu.VMEM((2, 1, LANES), F32), pltpu.VMEM((ns, LANES, LANES), F32)]
               + [pltpu.VMEM((tb, wide), F32)] * 7)
    return pl.pallas_call(
        functools.partial(_rwkv_prompt_kernel, tb=tb, ns=ns),
        grid=(B, n_grp, T // tb),
        in_specs=in_specs, out_specs=out_specs,
        out_shape=[jax.ShapeDtypeStruct((B, T, n_heads * HEAD_DIM), BF16),
                   jax.ShapeDtypeStruct((B, n_heads, HEAD_DIM, HEAD_DIM), F32)],
        scratch_shapes=scratch,
        compiler_params=_params(("parallel", "parallel", "arbitrary")),
        name="rwkv_prompt",
    )(proj, proj, proj, proj, proj, proj, mu_pad, mu_pad, mu_pad, mu_pad, mu_pad, mu_pad, hparams, w2p, a2p)


def _rwkv_params(mu, w0, w2, a0, a2, k_k, k_a, r_k, lnx_w, lnx_b):
    width = w0.shape[0]
    lora = w2.shape[0]
    zpad = jnp.zeros((LANES - lora,), F32)
    mu_pad = jnp.concatenate([mu[:4 * width], mu[4 * width:4 * width + lora], zpad, mu[4 * width + lora:], zpad])[None, :]
    hparams = jnp.stack([w0, a0, k_k, k_a, r_k.reshape(-1), lnx_w, lnx_b, jnp.zeros_like(w0)]).astype(F32)
    pad_rows = lambda m: jnp.concatenate([m, jnp.zeros((LANES - lora, width), m.dtype)], axis=0).astype(BF16)
    return mu_pad.astype(F32), hparams, pad_rows(w2), pad_rows(a2)


def _rwkv_sample_prep_kernel(pr_ref, pk_ref, pv_ref, pg_ref, pwd_ref, pad_ref,
                             qr_ref, qk_ref, qv_ref, qg_ref, qwd_ref, qad_ref,
                             mr_ref, mk_ref, mv_ref, mg_ref, mwd_ref, mad_ref,
                             hp_ref, w2_ref, a2_ref,
                             r_out, k_out, v_out, g_out, kk_out, b_out, ld_out, *, steps):
    def shifted(p_ref, q_ref, mu_ref):
        p = p_ref[...]
        return p + (q_ref[...] - p) * mu_ref[...]

    outs = _rwkv_prepare(
        shifted(pr_ref, qr_ref, mr_ref), shifted(pk_ref, qk_ref, mk_ref), shifted(pv_ref, qv_ref, mv_ref),
        shifted(pg_ref, qg_ref, mg_ref), shifted(pwd_ref, qwd_ref, mwd_ref), shifted(pad_ref, qad_ref, mad_ref),
        hp_ref[...], w2_ref[...], a2_ref[...], _gsum_wide)
    db = r_out.shape[2]
    for o_ref, val in zip((r_out, k_out, v_out, g_out, kk_out, b_out, ld_out), outs):
        for t in range(steps):
            o_ref[t] = val[t * db:(t + 1) * db, :].T


def _rwkv_sample_prep(proj, prev, mu_pad, hparams, w2p, a2p, *, n_heads, rw0, lora0, steps):
    n = proj.shape[0]
    db = n // steps
    n_hp = n_heads // 2
    blk = lambda off, fixed=False: pl.BlockSpec((n, LANES), (lambda h: (0, off)) if fixed else (lambda h: (0, off + h)))
    mblk = lambda off, fixed=False: pl.BlockSpec((1, LANES), (lambda h: (0, off)) if fixed else (lambda h: (0, off + h)))
    offs = [0, n_hp, 2 * n_hp, 3 * n_hp]
    in_specs = ([blk(rw0 + o) for o in offs] + [blk(lora0, True), blk(lora0 + 1, True)]
                + [blk(o) for o in offs] + [blk(4 * n_hp, True), blk(4 * n_hp + 1, True)]
                + [mblk(o) for o in offs] + [mblk(4 * n_hp, True), mblk(4 * n_hp + 1, True)]
                + [pl.BlockSpec((8, LANES), lambda h: (0, h)),
                   pl.BlockSpec((LANES, LANES), lambda h: (0, h)),
                   pl.BlockSpec((LANES, LANES), lambda h: (0, h))])
    width = n_heads * HEAD_DIM
    return pl.pallas_call(
        functools.partial(_rwkv_sample_prep_kernel, steps=steps),
        grid=(n_hp,),
        in_specs=in_specs,
        out_specs=[pl.BlockSpec((steps, LANES, db), lambda h: (0, h, 0))] * 7,
        out_shape=[jax.ShapeDtypeStruct((steps, width, db), F32)] * 7,
        compiler_params=_params(("parallel",)),
        name="rwkv_sample_prep",
    )(*([proj] * 6 + [prev] * 6 + [mu_pad] * 6 + [hparams, w2p, a2p]))


def _rwkv_sample_kernel(r_ref, k_ref, v_ref, g_ref, kk_ref, b_ref, ld_ref, s_ref, hpt_ref, out_ref, so_ref, y_s, *,
                        steps):
    db = out_ref.shape[0] // steps

    for hh in range(2):
        r0 = hh * HEAD_DIM
        rows = slice(r0, r0 + HEAD_DIM)
        kk = [kk_ref[t, rows, :] for t in range(steps)]
        bb = [b_ref[t, rows, :] for t in range(steps)]
        kv = [k_ref[t, rows, :] for t in range(steps)]
        rv = [r_ref[t, rows, :] for t in range(steps)]
        decay = [jnp.exp(ld_ref[t, rows, :]) for t in range(steps)]

        def row_block(v, carry):
            s_v = s_ref[hh, v]
            for t in range(steps):
                s_kk = jnp.sum(s_v * kk[t], axis=0, keepdims=True)
                s_v = s_v * decay[t] - s_kk * bb[t] + v_ref[t, pl.ds(r0 + v, 1), :] * kv[t]
                y_s[t, pl.ds(r0 + v, 1), :] = jnp.sum(s_v * rv[t], axis=0, keepdims=True)
            so_ref[hh, v] = s_v
            return carry

        lax.fori_loop(0, HEAD_DIM, row_block, 0, unroll=2)

    hpt = hpt_ref[...]
    r_k, lnx_w, lnx_b = hpt[:, 4:5], hpt[:, 5:6], hpt[:, 6:7]
    for t in range(steps):
        outs = []
        for hh in range(2):
            rows = slice(hh * HEAD_DIM, (hh + 1) * HEAD_DIM)
            y = y_s[t, rows, :]
            mean = jnp.mean(y, axis=0, keepdims=True)
            d = y - mean
            var = jnp.mean(d * d, axis=0, keepdims=True)
            yn = d * lax.rsqrt(var + RWKV_GN_EPS) * lnx_w[rows] + lnx_b[rows]
            bonus = jnp.sum(r_ref[t, rows, :] * k_ref[t, rows, :] * r_k[rows], axis=0, keepdims=True) * v_ref[t, rows, :]
            outs.append((yn + bonus) * _silu(g_ref[t, rows, :]))
        out_ref[t * db:(t + 1) * db, :] = jnp.concatenate(outs, axis=0).T


def _rwkv_sample(prepped, state, hparams_t, *, n_heads):
    steps, width, db = prepped[0].shape
    tok = pl.BlockSpec((steps, LANES, db), lambda j: (0, j, 0))
    st = pl.BlockSpec((2, HEAD_DIM, HEAD_DIM, db), lambda j: (j, 0, 0, 0))
    return pl.pallas_call(
        functools.partial(_rwkv_sample_kernel, steps=steps),
        grid=(n_heads // 2,),
        in_specs=[tok] * 7 + [st, pl.BlockSpec((LANES, 8), lambda j: (j, 0))],
        out_specs=[pl.BlockSpec((steps * db, LANES), lambda j: (0, j)), st],
        out_shape=[jax.ShapeDtypeStruct((steps * db, width), F32),
                   jax.ShapeDtypeStruct(state.shape, F32)],
        scratch_shapes=[pltpu.VMEM((steps, LANES, db), F32)],
        compiler_params=_params(("parallel",)),
        name="rwkv_sample",
    )(*prepped, state, hparams_t)


def _diff_lambda(lam_ref, lam_init):
    lp = lam_ref[...]
    s1 = jnp.sum(lp[0:1] * lp[1:2], axis=1, keepdims=True)
    s2 = jnp.sum(lp[2:3] * lp[3:4], axis=1, keepdims=True)
    return jnp.exp(s1) - jnp.exp(s2) + lam_init


def _diff_finish(o1, o2, lam, subw, ag, lam_init):
    o = o1 - lam * o2
    o = o * lax.rsqrt(jnp.mean(o * o, axis=-1, keepdims=True) + NORM_EPS)
    return o * subw * (1.0 - lam_init) * _silu(ag)


def _stack_maps(q):
    m0, m1 = _head_masks(q.shape)
    return jnp.concatenate([q * m0, q * m1], axis=0)


def _attn_prompt_kernel(q_ref, k_ref, v_ref, ag_ref, lam_ref, subw_ref, o_ref, ko_ref, vo_ref,
                        kb_s, vt_s, qt_s, m_s, l_s, acc_s, *, tq, tk, sub, lam_init):
    qi = pl.program_id(2)

    @pl.when(qi == 0)
    def _():
        ko_ref[...] = k_ref[...]
        vo_ref[...] = v_ref[...]
        kb_s[...] = k_ref[0].astype(BF16)
        vt_s[...] = v_ref[0].T.astype(BF16)

    qt_s[...] = _stack_maps(q_ref[0] * ATTN_SCALE).T.astype(BF16)
    m_s[...] = jnp.full_like(m_s, NEG_INF)
    l_s[...] = jnp.zeros_like(l_s)
    acc_s[...] = jnp.zeros_like(acc_s)
    n_sub = 2 * tq // sub

    def block(k0, diag_off):
        def width(j):
            if diag_off is None:
                return tk
            return max(0, min(tk, (j * sub) % tq + sub - diag_off))

        def scores(j):
            kw = width(j)
            if kw == 0:
                return None
            s = _dot(kb_s[pl.ds(k0, kw), :], qt_s[:, j * sub:(j + 1) * sub])
            if diag_off is not None:
                q_pos = (_iota2(s.shape, 1) + j * sub) & (tq - 1)
                s = jnp.where(_iota2(s.shape, 0) + diag_off <= q_pos, s, NEG_INF)
            return s

        s_next = scores(0)
        for j in range(n_sub):
            s = s_next
            if j + 1 < n_sub:
                s_next = scores(j + 1)
            if s is None:
                continue
            cols = slice(j * sub, (j + 1) * sub)
            m_old = m_s[:, cols]
            m_new = jnp.maximum(m_old, jnp.max(s, axis=0, keepdims=True))
            alpha = jnp.exp(m_old - m_new)
            p = jnp.exp(s - m_new)
            l_s[:, cols] = alpha * l_s[:, cols] + jnp.sum(p, axis=0, keepdims=True)
            acc_s[:, cols] = alpha * acc_s[:, cols] + _dot(vt_s[:, pl.ds(k0, width(j))], p.astype(BF16))
            m_s[:, cols] = m_new

    def body(ki, carry):
        block(pl.multiple_of(ki * tk, tk), None)
        return carry

    lax.fori_loop(0, qi * (tq // tk), body, 0)
    for d in range(tq // tk):
        block(pl.multiple_of(qi * tq + d * tk, tk), d * tk)
    o = acc_s[...] / l_s[...]
    lam = _diff_lambda(lam_ref, lam_init)
    o_ref[0] = _diff_finish(o[:, :tq].T, o[:, tq:].T, lam, subw_ref[...], ag_ref[0], lam_init).astype(o_ref.dtype)


def _attn_prompt(proj, lam_params, subw, *, n_heads, at0, tq, lam_init):
    B, T, _ = proj.shape
    qblk = lambda off: pl.BlockSpec((1, tq, LANES), lambda b, h, i: (b, i, off + h))
    kblk = lambda off: pl.BlockSpec((1, T, LANES), lambda b, h, i: (b, 0, off + h))
    return pl.pallas_call(
        functools.partial(_attn_prompt_kernel, tq=tq, tk=tq, sub=min(256, 2 * tq), lam_init=lam_init),
        grid=(B, n_heads, T // tq),
        in_specs=[qblk(at0), kblk(at0 + n_heads), kblk(at0 + 2 * n_heads), qblk(at0 + 3 * n_heads),
                  pl.BlockSpec(lam_params.shape, lambda b, h, i: (0, 0)),
                  pl.BlockSpec((1, LANES), lambda b, h, i: (0, 0))],
        out_specs=[pl.BlockSpec((1, tq, LANES), lambda b, h, i: (b, i, h)),
                   pl.BlockSpec((1, T, LANES), lambda b, h, i: (b, 0, h)),
                   pl.BlockSpec((1, T, LANES), lambda b, h, i: (b, 0, h))],
        out_shape=[jax.ShapeDtypeStruct((B, T, n_heads * LANES), BF16),
                   jax.ShapeDtypeStruct((B, T, n_heads * LANES), F32),
                   jax.ShapeDtypeStruct((B, T, n_heads * LANES), F32)],
        scratch_shapes=[pltpu.VMEM((T, LANES), BF16), pltpu.VMEM((LANES, T), BF16), pltpu.VMEM((LANES, 2 * tq), BF16),
                        pltpu.VMEM((1, 2 * tq), F32), pltpu.VMEM((1, 2 * tq), F32), pltpu.VMEM((LANES, 2 * tq), F32)],
        compiler_params=_params(("parallel", "parallel", "arbitrary")),
        name="attn_prompt",
    )(proj, proj, proj, proj, lam_params, subw)


def _attn_sample_kernel(pt_ref, q_ref, kn_ref, vn_ref, ag_ref, *refs, n_heads, steps, pages, lam_init):
    del pt_ref
    k_refs, v_refs = refs[:pages], refs[pages:2 * pages]
    lam_ref, subw_ref, o_ref, m_s, l_s, acc_s = refs[2 * pages:]
    gi = pl.program_id(1)
    n_groups = pl.num_programs(1)
    rows_h = 2 * steps
    rows = n_heads * rows_h
    kdim, page = k_refs[0].shape[1], k_refs[0].shape[2]
    assert steps & (steps - 1) == 0

    @pl.when(gi == 0)
    def _():
        m_s[...] = jnp.full_like(m_s, NEG_INF)
        l_s[...] = jnp.zeros_like(l_s)
        acc_s[...] = jnp.zeros_like(acc_s)

    q_all = q_ref[0] * ATTN_SCALE
    hs = lambda h: slice(h * LANES, (h + 1) * LANES)
    q_rep = jnp.concatenate([q_all] * (rows // steps), axis=0)
    col_grp = lax.shift_right_logical(_iota2((rows, kdim), 1), int(math.log2(HEAD_DIM)))
    row_grp = lax.shift_right_logical(_iota2((rows, kdim), 0), int(math.log2(steps)))
    q_bd = jnp.where(col_grp == row_grp, q_rep, 0.0).astype(BF16)

    s = jnp.concatenate([_dot(q_bd, k_refs[i][0].astype(BF16)) for i in range(pages)], axis=1)
    m_old = m_s[...]
    m_run = jnp.maximum(m_old, jnp.max(s, axis=1, keepdims=True))
    alpha = jnp.exp(m_old - m_run)
    p = jnp.exp(s - m_run)
    l_run = alpha * l_s[...] + jnp.sum(p, axis=1, keepdims=True)
    pv = []
    for h in range(n_heads):
        v_h = jnp.concatenate([v_refs[i][0, pl.ds(h, page, stride=n_heads), :].astype(BF16) for i in range(pages)],
                              axis=0)
        pv.append(_dot(p[h * rows_h:(h + 1) * rows_h].astype(BF16), v_h))
    acc = alpha * acc_s[...] + jnp.concatenate(pv, axis=0)
    m_s[...] = m_run
    l_s[...] = l_run
    acc_s[...] = acc

    @pl.when(gi == n_groups - 1)
    def _():
        lam = _diff_lambda(lam_ref, lam_init)
        tok = _iota2((rows_h, LANES), 0) & (steps - 1)
        lane = _iota2((rows_h, LANES), 1)
        for h in range(n_heads):
            rs = slice(h * rows_h, (h + 1) * rows_h)
            q2 = _stack_maps(q_all[:, hs(h)])
            kn = kn_ref[0][:, hs(h)]
            vn = vn_ref[0][:, hs(h)]
            s = jnp.full((rows_h, LANES), NEG_INF, F32)
            for j in range(steps):
                sj = jnp.sum(q2 * kn[j:j + 1, :], axis=1, keepdims=True)
                s = jnp.where((lane == j) & (tok >= j), sj, s)
            m_old = m_run[rs]
            m_new = jnp.maximum(m_old, jnp.max(s, axis=1, keepdims=True))
            alpha = jnp.exp(m_old - m_new)
            p = jnp.exp(s - m_new)
            l_fin = alpha * l_run[rs] + jnp.sum(p, axis=1, keepdims=True)
            a_fin = alpha * acc[rs]
            for j in range(steps):
                a_fin = a_fin + p[:, j:j + 1] * vn[j:j + 1, :]
            o = a_fin / l_fin
            o_ref[0, :, hs(h)] = _diff_finish(o[:steps], o[steps:], lam, subw_ref[...], ag_ref[0][:, hs(h)], lam_init)


def _attn_sample(qkvg, cache_k, cache_v, page_table, lam_params, subw, *, n_heads, pages, lam_init):
    db, steps, _ = qkvg.shape
    n_pages = page_table.shape[0] // db
    n_groups = n_pages // pages
    width = n_heads * LANES
    rows = n_heads * 2 * steps
    tokblk = lambda off: pl.BlockSpec((1, steps, width), lambda b, g, pt: (b, 0, off))
    pageblk = lambda cache, i: pl.BlockSpec((1,) + cache.shape[1:],
                                            lambda b, g, pt: (pt[(b * n_groups + g) * pages + i], 0, 0))
    grid_spec = pltpu.PrefetchScalarGridSpec(
        num_scalar_prefetch=1,
        grid=(db, n_groups),
        in_specs=([tokblk(0), tokblk(1), tokblk(2), tokblk(3)]
                  + [pageblk(cache_k, i) for i in range(pages)] + [pageblk(cache_v, i) for i in range(pages)]
                  + [pl.BlockSpec(lam_params.shape, lambda b, g, pt: (0, 0)),
                     pl.BlockSpec((1, LANES), lambda b, g, pt: (0, 0))]),
        out_specs=pl.BlockSpec((1, steps, width), lambda b, g, pt: (b, 0, 0)),
        scratch_shapes=[pltpu.VMEM((rows, 1), F32), pltpu.VMEM((rows, 1), F32), pltpu.VMEM((rows, LANES), F32)])
    return pl.pallas_call(
        functools.partial(_attn_sample_kernel, n_heads=n_heads, steps=steps, pages=pages, lam_init=lam_init),
        grid_spec=grid_spec,
        out_shape=jax.ShapeDtypeStruct((db, steps, width), F32),
        compiler_params=_params(("parallel", "arbitrary")),
        name="attn_sample",
    )(page_table, qkvg, qkvg, qkvg, qkvg, *([cache_k] * pages), *([cache_v] * pages), lam_params, subw)


def _outproj_kernel(x_ref, rw_ref, at_ref, grw_ref, gat_ref, wbr_ref, wba_ref, wo_ref, nf_ref, o_ref):
    merged = (_sigmoid(grw_ref[...]) * _dot(rw_ref[...], wbr_ref[...])
              + _sigmoid(gat_ref[...]) * _dot(at_ref[...], wba_ref[...]))
    y = x_ref[...] + _dot(merged.astype(BF16), wo_ref[...])
    o_ref[...] = y * lax.rsqrt(jnp.mean(y * y, axis=-1, keepdims=True) + NORM_EPS) * nf_ref[...]


def _outproj(x, rw, at, proj, wbr, wba, wo, nf, *, tm):
    n, d = x.shape
    row = lambda w: pl.BlockSpec((tm, w), lambda i: (i, 0))
    const = lambda a: pl.BlockSpec(a.shape, lambda i: (0, 0), pipeline_mode=pl.Buffered(1))
    return pl.pallas_call(
        _outproj_kernel,
        grid=(n // tm,),
        in_specs=[row(d), row(rw.shape[1]), row(at.shape[1]), row(d),
                  pl.BlockSpec((tm, d), lambda i: (i, 1)),
                  const(wbr), const(wba), const(wo), const(nf)],
        out_specs=row(d),
        out_shape=jax.ShapeDtypeStruct((n, d), F32),
        compiler_params=_params(("parallel",)),
        name="outproj",
    )(x, rw, at, proj, proj, wbr, wba, wo, nf)


def _largest_tile(n, cap):
    t = cap
    while n % t:
        t //= 2
    return t


def kernel(x_prompt, x_sample, cache_k, cache_v, state_wkv, state_shift, page_table, norm_in, w_in, mu_shift, w0, w2, a0, a2, k_k, k_a, r_k, lnx_w, lnx_b, lambda_q1, lambda_k1, lambda_q2, lambda_k2, subln_w, w_br_rwkv, w_br_attn, w_out, norm_f):
    depth = w_in.shape[0]
    assert depth == 1, "single-layer step"
    B, T, D = x_prompt.shape
    DB, TS, _ = x_sample.shape
    r_heads = r_k.shape[1]
    r_width = r_heads * HEAD_DIM
    lora = w2.shape[1]
    a_heads = cache_k.shape[3]
    a_width = a_heads * LANES
    rw_in = 4 * r_width + 2 * lora
    lam_init = 0.8 - 0.6 * math.exp(-0.3 * 0)

    segments = ((rw_in + 4 * a_width, 2 * D, 0), (0, 4 * r_width, 0), (rw_in, 4 * a_width, 0),
                (4 * r_width, lora, LANES - lora), (4 * r_width + lora, lora, LANES - lora))
    w_pad = _regroup_weights(jnp.transpose(w_in[0]), segments, tc=_largest_tile(D, 128))
    rw0 = 2 * D // LANES
    at0 = rw0 + 4 * r_width // LANES
    lora0 = at0 + 4 * a_width // LANES
    pw = w_pad.shape[0]

    mu_pad, hparams, w2p, a2p = _rwkv_params(mu_shift[0], w0[0], w2[0], a0[0], a2[0], k_k[0], k_a[0], r_k[0],
                                             lnx_w[0], lnx_b[0])
    lam_params = jnp.stack([lambda_q1[0], lambda_k1[0], lambda_q2[0], lambda_k2[0]]).astype(F32)
    subw = subln_w[0][None, :].astype(F32)
    g_in = norm_in[0][None, :].astype(F32)
    wbr = w_br_rwkv[0].astype(BF16)
    wba = w_br_attn[0].astype(BF16)
    wo = w_out[0].astype(BF16)
    nf = norm_f[None, :].astype(F32)
    tn = 14 * LANES
    assert pw % tn == 0

    n_p = B * T
    xp = x_prompt.reshape(n_p, D)
    proj_p = _inproj(xp, g_in, w_pad, tm=_largest_tile(n_p, 1024), tn=tn)
    proj_p3 = proj_p.reshape(B, T, pw)
    rw_p, wkv_p = _rwkv_prompt(proj_p3, mu_pad, hparams, w2p, a2p, n_heads=r_heads, rw0=rw0, lora0=lora0,
                               tb=_largest_tile(T, 512), ns=_largest_tile(r_heads // 2, 8))
    at_p, k_p, v_p = _attn_prompt(proj_p3, lam_params, subw, n_heads=a_heads, at0=at0, tq=_largest_tile(T, 512),
                                  lam_init=lam_init)
    y_p = _outproj(xp, rw_p.reshape(n_p, r_width), at_p.reshape(n_p, a_width), proj_p, wbr, wba, wo, nf,
                   tm=_largest_tile(n_p, 256))

    n_s = DB * TS
    xs = jnp.transpose(x_sample, (1, 0, 2)).reshape(n_s, D)
    proj_s = _inproj(xs, g_in, w_pad, tm=_largest_tile(n_s, 512), tn=tn)
    c0 = rw0 * LANES
    zs = jnp.zeros((DB, LANES - lora), F32)
    sh = state_shift[0]
    shift_pad = jnp.concatenate([sh[:, :4 * r_width], sh[:, 4 * r_width:4 * r_width + lora], zs,
                                 sh[:, 4 * r_width + lora:], zs], axis=-1)
    rwkv_cols = jnp.concatenate([proj_s[:, c0:c0 + 4 * r_width], proj_s[:, lora0 * LANES:]], axis=-1)
    prev = jnp.concatenate([shift_pad, rwkv_cols[:-DB]], axis=0)
    prepped = _rwkv_sample_prep(proj_s, prev, mu_pad, hparams, w2p, a2p, n_heads=r_heads, rw0=rw0, lora0=lora0,
                                steps=TS)
    rw_s, wkv_s = _rwkv_sample(prepped, jnp.transpose(state_wkv[0], (1, 2, 3, 0)), hparams.T, n_heads=r_heads)
    wkv_s = jnp.transpose(wkv_s, (3, 0, 1, 2))
    n_pool, page = cache_k.shape[1], cache_k.shape[2]
    n_pages = page_table.shape[1]
    q0 = at0 * LANES
    qkvg_s = jnp.transpose(proj_s[:, q0:q0 + 4 * a_width].reshape(TS, DB, 4 * a_width), (1, 0, 2))
    kc = jnp.transpose(cache_k[0], (0, 2, 3, 4, 1)).reshape(n_pool, a_width, page)
    at_s = _attn_sample(qkvg_s, kc, cache_v[0].reshape(n_pool, page * a_heads, LANES), page_table.reshape(-1),
                        lam_params, subw, n_heads=a_heads, pages=_largest_tile(n_pages, 16), lam_init=lam_init)
    at_s = jnp.transpose(at_s, (1, 0, 2)).reshape(n_s, a_width)
    y_s = _outproj(xs, rw_s.astype(BF16), at_s.astype(BF16), proj_s, wbr, wba, wo, nf, tm=_largest_tile(n_s, 256))
    y_s = jnp.transpose(y_s.reshape(TS, DB, D), (1, 0, 2))

    def shift_row(last):
        return jnp.concatenate([last[:, c0:c0 + 4 * r_width], last[:, lora0 * LANES:lora0 * LANES + lora],
                                last[:, (lora0 + 1) * LANES:(lora0 + 1) * LANES + lora]], axis=-1)[None]

    return (y_p.reshape(B, T, D), y_s,
            k_p.reshape(1, B, T, a_heads, 2, HEAD_DIM), v_p.reshape(1, B, T, a_heads, LANES),
            qkvg_s[:, :, a_width:2 * a_width].reshape(1, DB, TS, a_heads, 2, HEAD_DIM),
            qkvg_s[:, :, 2 * a_width:3 * a_width].reshape(1, DB, TS, a_heads, LANES),
            wkv_p[None], wkv_s[None], shift_row(proj_p3[:, -1]), shift_row(proj_s[-DB:]))
```

```python
import functools
import math

import jax
import jax.numpy as jnp
from jax import lax
from jax.experimental import pallas as pl
from jax.experimental.pallas import tpu as pltpu

F32 = jnp.float32
BF16 = jnp.bfloat16

LANES = 128
HEAD_DIM = 64
CHUNK = 64
NORM_EPS = 1e-6
RWKV_GN_EPS = 64e-5
NEG_INF = -1e30
ATTN_SCALE = HEAD_DIM ** -0.5
VMEM_LIMIT = 56 * 1024 * 1024

HI = lax.Precision.HIGHEST


def _dot(a, b, prec=None):
    return lax.dot_general(a, b, (((1,), (0,)), ((), ())), precision=prec, preferred_element_type=F32)


def _dot_nt(a, b, prec=None):
    return lax.dot_general(a, b, (((1,), (1,)), ((), ())), precision=prec, preferred_element_type=F32)


def _dot_tn(a, b, prec=None):
    return lax.dot_general(a, b, (((0,), (0,)), ((), ())), precision=prec, preferred_element_type=F32)


def _iota2(shape, dim):
    return lax.broadcasted_iota(jnp.int32, shape, dim)


def _sigmoid(x):
    return 1.0 / (1.0 + jnp.exp(-x))


def _silu(x):
    return x * _sigmoid(x)


def _params(semantics):
    return pltpu.CompilerParams(dimension_semantics=semantics, vmem_limit_bytes=VMEM_LIMIT)


def _regroup_kernel(w_ref, o_ref, *, segments):
    row = 0
    for start, width, pad in segments:
        o_ref[row:row + width, :] = w_ref[start:start + width, :].astype(o_ref.dtype)
        row += width
        if pad:
            o_ref[row:row + pad, :] = jnp.zeros((pad, o_ref.shape[1]), o_ref.dtype)
            row += pad


def _regroup_weights(w_t, segments, *, tc):
    cols, d = w_t.shape
    out_cols = sum(width + pad for _, width, pad in segments)
    return pl.pallas_call(
        functools.partial(_regroup_kernel, segments=segments),
        grid=(d // tc,),
        in_specs=[pl.BlockSpec((cols, tc), lambda i: (0, i))],
        out_specs=pl.BlockSpec((out_cols, tc), lambda i: (0, i)),
        out_shape=jax.ShapeDtypeStruct((out_cols, d), BF16),
        compiler_params=_params(("parallel",)),
        name="regroup_weights",
    )(w_t)


def _inproj_kernel(x_ref, g_ref, w_ref, o_ref, xn_ref):
    @pl.when(pl.program_id(1) == 0)
    def _():
        x = x_ref[...]
        xn = x * lax.rsqrt(jnp.mean(x * x, axis=-1, keepdims=True) + NORM_EPS) * g_ref[...]
        xn_ref[...] = xn.astype(BF16)

    o_ref[...] = _dot_nt(xn_ref[...], w_ref[...])


def _inproj(x, g, w_t, *, tm, tn):
    n, d = x.shape
    pw = w_t.shape[0]
    return pl.pallas_call(
        _inproj_kernel,
        grid=(n // tm, pw // tn),
        in_specs=[pl.BlockSpec((tm, d), lambda i, j: (i, 0)),
                  pl.BlockSpec((1, d), lambda i, j: (0, 0)),
                  pl.BlockSpec((tn, d), lambda i, j: (j, 0))],
        out_specs=pl.BlockSpec((tm, tn), lambda i, j: (i, j)),
        out_shape=jax.ShapeDtypeStruct((n, pw), F32),
        scratch_shapes=[pltpu.VMEM((tm, d), BF16)],
        compiler_params=_params(("parallel", "arbitrary")),
        name="inproj",
    )(x, g, w_t)


def _rwkv_prepare(ps_r, ps_k, ps_v, ps_g, ps_wd, ps_ad, hp, w2, a2, gsum):
    w0, a0, k_k, k_a = hp[0:1], hp[1:2], hp[2:3], hp[3:4]
    w_pre = w0 + _dot(jnp.tanh(ps_wd).astype(BF16), w2)
    log_decay = -math.exp(-0.5) * _sigmoid(w_pre)
    a = _sigmoid(a0 + _dot(ps_ad.astype(BF16), a2))
    kk = ps_k * k_k
    kk = kk * lax.rsqrt(jnp.maximum(gsum(kk * kk), 1e-24))
    k = ps_k * (1.0 + (a - 1.0) * k_a)
    return ps_r, k, ps_v, ps_g, kk, kk * a, log_decay


def _rwkv_finish(y, r, k, v, g, hp, gsum):
    r_k, lnx_w, lnx_b = hp[4:5], hp[5:6], hp[6:7]
    mean = gsum(y) * (1.0 / HEAD_DIM)
    d = y - mean
    var = gsum(d * d) * (1.0 / HEAD_DIM)
    yn = d * lax.rsqrt(var + RWKV_GN_EPS) * lnx_w + lnx_b
    bonus = gsum(r * k * r_k) * v
    return (yn + bonus) * _silu(g)


def _head_masks(shape):
    lane = _iota2(shape, 1)
    m0 = (lane < HEAD_DIM).astype(F32)
    return m0, 1.0 - m0


def _group_ones():
    i = _iota2((LANES, LANES), 0) // HEAD_DIM
    j = _iota2((LANES, LANES), 1) // HEAD_DIM
    return (i == j).astype(BF16)


def _gsum_wide(x):
    ones = _group_ones()
    hi = x.astype(BF16)
    lo = (x - hi.astype(F32)).astype(BF16)
    parts = []
    for j in range(x.shape[1] // LANES):
        ls = slice(j * LANES, (j + 1) * LANES)
        parts.append(_dot(hi[:, ls], ones) + _dot(lo[:, ls], ones))
    return parts[0] if len(parts) == 1 else jnp.concatenate(parts, axis=1)


def _rwkv_prompt_kernel(pr_ref, pk_ref, pv_ref, pg_ref, pwd_ref, pad_ref,
                        mr_ref, mk_ref, mv_ref, mg_ref, mwd_ref, mad_ref,
                        hp_ref, w2_ref, a2_ref,
                        out_ref, state_ref,
                        carry_ref, cw_ref, m_ref, r_s, k_s, v_s, g_s, kk_s, b_s, ld_s, *, tb, ns):
    t_idx = pl.program_id(2)
    n_t = pl.num_programs(2)
    C = CHUNK

    @pl.when(t_idx == 0)
    def _():
        carry_ref[...] = jnp.zeros_like(carry_ref)
        cw_ref[...] = jnp.zeros_like(cw_ref)
        m_ref[...] = jnp.zeros_like(m_ref)

    def shifted(p_ref, mu_ref, c_ref, idx):
        p = p_ref[0]
        row = _iota2(p.shape, 0)
        prev = pltpu.roll(p, 1, axis=0)
        prev = jnp.where(row == 0, c_ref[idx], prev)
        c_ref[idx] = p[tb - 1:tb, :]
        return p + (prev - p) * mu_ref[...]

    hp = hp_ref[...]
    r, k, v, g, kk, b, ld = _rwkv_prepare(
        shifted(pr_ref, mr_ref, carry_ref, 0), shifted(pk_ref, mk_ref, carry_ref, 1),
        shifted(pv_ref, mv_ref, carry_ref, 2), shifted(pg_ref, mg_ref, carry_ref, 3),
        shifted(pwd_ref, mwd_ref, cw_ref, 0), shifted(pad_ref, mad_ref, cw_ref, 1),
        hp, w2_ref[...], a2_ref[...], _gsum_wide)
    r_s[...] = r
    k_s[...] = k
    v_s[...] = v
    g_s[...] = g
    kk_s[...] = kk
    b_s[...] = b
    ld_s[...] = ld

    m0, m1 = _head_masks((C, LANES))
    ii = _iota2((2 * C, 2 * C), 0)
    jj = _iota2((2 * C, 2 * C), 1)
    strict = ii > jj
    incl = ii >= jj
    eye = ii == jj
    row_c = _iota2((C, LANES), 0)
    first = m0 > 0.5

    def cumsum_rows(x):
        shift = 1
        while shift < C:
            x = x + jnp.where(row_c >= shift, pltpu.roll(x, shift, axis=0), 0.0)
            shift *= 2
        return x

    def head_sums(x):
        s0 = jnp.sum(x * m0, axis=1, keepdims=True)
        s1 = jnp.sum(x * m1, axis=1, keepdims=True)
        return jnp.where(first, s0, s1)

    stack2 = lambda x: jnp.concatenate([x * m0, x * m1], axis=0).astype(BF16)
    bf = lambda x: x.astype(BF16)

    lanes = [slice(s * LANES, (s + 1) * LANES) for s in range(ns)]
    each = lambda f, *cols: [f(*args) for args in zip(*cols)]

    def chunk(ci, ms):
        sl = pl.ds(pl.multiple_of(ci * C, C), C)
        ld = [ld_s[sl, ls] for ls in lanes]
        c = each(cumsum_rows, ld)
        c_end = each(lambda x: x[C - 1:C, :], c)
        p_inv = each(lambda x: jnp.exp(-x), c)
        p_end = each(lambda e, x: jnp.exp(e - x), c_end, c)
        r2 = each(lambda ls, x: stack2(r_s[sl, ls] * jnp.exp(x)), lanes, c)
        kk2 = each(lambda ls, x, d: stack2(kk_s[sl, ls] * jnp.exp(x - d)), lanes, c, ld)
        kh2 = each(lambda ls, p: stack2(k_s[sl, ls] * p), lanes, p_inv)
        bh2 = each(lambda ls, p: stack2(b_s[sl, ls] * p), lanes, p_inv)
        ke2 = each(lambda ls, p: stack2(k_s[sl, ls] * p), lanes, p_end)
        be2 = each(lambda ls, p: stack2(b_s[sl, ls] * p), lanes, p_end)
        v2 = each(lambda ls: stack2(v_s[sl, ls]), lanes)
        a = each(lambda w, x, y, z: _dot_nt(jnp.concatenate([w, x], axis=0), jnp.concatenate([y, z], axis=0)),
                 kk2, r2, kh2, bh2)
        l_k = each(lambda x: bf(jnp.where(strict, x[:2 * C, :2 * C], 0.0)), a)
        l_b = each(lambda x: bf(jnp.where(strict, x[:2 * C, 2 * C:], 0.0)), a)
        a_rk = each(lambda x: bf(jnp.where(incl, x[2 * C:, :2 * C], 0.0)), a)
        a_rb = each(lambda x: bf(jnp.where(incl, -x[2 * C:, 2 * C:], 0.0)), a)
        mb = each(bf, ms)
        x = each(lambda w, lk, m, v: _dot(jnp.concatenate([w, lk], axis=1), jnp.concatenate([m, v], axis=0)),
                 kk2, l_k, mb, v2)
        pw = l_b
        u = each(lambda x_, p: x_ - _dot(p, bf(x_)), x, pw)
        for _ in range(5):
            pw = each(lambda p: bf(_dot(p, p)), pw)
            u = each(lambda u_, p: u_ + _dot(p, bf(u_)), u, pw)
        ub = each(bf, u)
        y2 = each(lambda r_, ak, ab, m, v, u_: _dot(jnp.concatenate([r_, ak, ab], axis=1),
                                                     jnp.concatenate([m, v, u_], axis=0)),
                  r2, a_rk, a_rb, mb, v2, ub)
        p_c_col = each(lambda e: jnp.sum(jnp.where(eye, jnp.broadcast_to(jnp.exp(e), (2 * C, LANES)), 0.0),
                                         axis=1, keepdims=True), c_end)
        m_new = each(lambda m, pc, ke, be, v, u_: m * pc + _dot_tn(jnp.concatenate([ke, -be], axis=0),
                                                                   jnp.concatenate([v, u_], axis=0)),
                     ms, p_c_col, ke2, be2, v2, ub)
        for ls, y in zip(lanes, y2):
            out = _rwkv_finish(y[:C] + y[C:], r_s[sl, ls], k_s[sl, ls], v_s[sl, ls], g_s[sl, ls], hp[:, ls],
                               head_sums)
            out_ref[0, sl, ls] = out.astype(out_ref.dtype)
        return tuple(m_new)

    ms = lax.fori_loop(0, tb // C, chunk, tuple(m_ref[s] for s in range(ns)))
    for s in range(ns):
        m_ref[s] = ms[s]

    @pl.when(t_idx == n_t - 1)
    def _():
        for s in range(ns):
            mt = ms[s].T
            state_ref[0, 2 * s] = mt[:HEAD_DIM, :HEAD_DIM]
            state_ref[0, 2 * s + 1] = mt[HEAD_DIM:, HEAD_DIM:]


def _rwkv_prompt(proj, mu_pad, hparams, w2p, a2p, *, n_heads, rw0, lora0, tb, ns):
    B, T, _ = proj.shape
    n_hp = n_heads // 2
    assert rw0 % ns == 0 and n_hp % ns == 0
    n_grp = n_hp // ns
    wide = ns * LANES
    col = lambda off: (lambda b, h, t: (b, t, off // ns + h))
    colc = lambda off: (lambda b, h, t: (b, t, off))
    mcol = lambda off: (lambda b, h, t: (0, off // ns + h))
    mcolc = lambda off: (lambda b, h, t: (0, off))
    pblk = lambda im: pl.BlockSpec((1, tb, wide), im)
    mblk = lambda im: pl.BlockSpec((1, wide), im)
    in_specs = [pblk(col(rw0)), pblk(col(rw0 + n_hp)), pblk(col(rw0 + 2 * n_hp)), pblk(col(rw0 + 3 * n_hp)),
                pl.BlockSpec((1, tb, LANES), colc(lora0)), pl.BlockSpec((1, tb, LANES), colc(lora0 + 1)),
                mblk(mcol(0)), mblk(mcol(n_hp)), mblk(mcol(2 * n_hp)), mblk(mcol(3 * n_hp)),
                pl.BlockSpec((1, LANES), mcolc(4 * n_hp)), pl.BlockSpec((1, LANES), mcolc(4 * n_hp + 1)),
                pl.BlockSpec((8, wide), lambda b, h, t: (0, h)),
                pl.BlockSpec((LANES, wide), lambda b, h, t: (0, h)),
                pl.BlockSpec((LANES, wide), lambda b, h, t: (0, h))]
    out_specs = [pl.BlockSpec((1, tb, wide), lambda b, h, t: (b, t, h)),
                 pl.BlockSpec((1, 2 * ns, HEAD_DIM, HEAD_DIM), lambda b, h, t: (b, h, 0, 0))]
    scratch = ([pltpu.VMEM((4, 1, wide), F32), pltpu.VMEM((2, 1, LANES), F32), pltpu.VMEM((ns, LANES, LANES), F32)]
               + [pltpu.VMEM((tb, wide), F32)] * 7)
    return pl.pallas_call(
        functools.partial(_rwkv_prompt_kernel, tb=tb, ns=ns),
        grid=(B, n_grp, T // tb),
        in_specs=in_specs, out_specs=out_specs,
        out_shape=[jax.ShapeDtypeStruct((B, T, n_heads * HEAD_DIM), BF16),
                   jax.ShapeDtypeStruct((B, n_heads, HEAD_DIM, HEAD_DIM), F32)],
        scratch_shapes=scratch,
        compiler_params=_params(("parallel", "parallel", "arbitrary")),
        name="rwkv_prompt",
    )(proj, proj, proj, proj, proj, proj, mu_pad, mu_pad, mu_pad, mu_pad, mu_pad, mu_pad, hparams, w2p, a2p)


def _rwkv_params(mu, w0, w2, a0, a2, k_k, k_a, r_k, lnx_w, lnx_b):
    width = w0.shape[0]
    lora = w2.shape[0]
    zpad = jnp.zeros((LANES - lora,), F32)
    mu_pad = jnp.concatenate([mu[:4 * width], mu[4 * width:4 * width + lora], zpad, mu[4 * width + lora:], zpad])[None, :]
    hparams = jnp.stack([w0, a0, k_k, k_a, r_k.reshape(-1), lnx_w, lnx_b, jnp.zeros_like(w0)]).astype(F32)
    pad_rows = lambda m: jnp.concatenate([m, jnp.zeros((LANES - lora, width), m.dtype)], axis=0).astype(BF16)
    return mu_pad.astype(F32), hparams, pad_rows(w2), pad_rows(a2)


def _rwkv_sample_prep_kernel(pr_ref, pk_ref, pv_ref, pg_ref, pwd_ref, pad_ref,
                             qr_ref, qk_ref, qv_ref, qg_ref, qwd_ref, qad_ref,
                             mr_ref, mk_ref, mv_ref, mg_ref, mwd_ref, mad_ref,
                             hp_ref, w2_ref, a2_ref,
                             r_out, k_out, v_out, g_out, kk_out, b_out, ld_out, *, steps):
    def shifted(p_ref, q_ref, mu_ref):
        p = p_ref[...]
        return p + (q_ref[...] - p) * mu_ref[...]

    outs = _rwkv_prepare(
        shifted(pr_ref, qr_ref, mr_ref), shifted(pk_ref, qk_ref, mk_ref), shifted(pv_ref, qv_ref, mv_ref),
        shifted(pg_ref, qg_ref, mg_ref), shifted(pwd_ref, qwd_ref, mwd_ref), shifted(pad_ref, qad_ref, mad_ref),
        hp_ref[...], w2_ref[...], a2_ref[...], _gsum_wide)
    db = r_out.shape[2]
    for o_ref, val in zip((r_out, k_out, v_out, g_out, kk_out, b_out, ld_out), outs):
        for t in range(steps):
            o_ref[t] = val[t * db:(t + 1) * db, :].T


def _rwkv_sample_prep(proj, prev, mu_pad, hparams, w2p, a2p, *, n_heads, rw0, lora0, steps):
    n = proj.shape[0]
    db = n // steps
    n_hp = n_heads // 2
    blk = lambda off, fixed=False: pl.BlockSpec((n, LANES), (lambda h: (0, off)) if fixed else (lambda h: (0, off + h)))
    mblk = lambda off, fixed=False: pl.BlockSpec((1, LANES), (lambda h: (0, off)) if fixed else (lambda h: (0, off + h)))
    offs = [0, n_hp, 2 * n_hp, 3 * n_hp]
    in_specs = ([blk(rw0 + o) for o in offs] + [blk(lora0, True), blk(lora0 + 1, True)]
                + [blk(o) for o in offs] + [blk(4 * n_hp, True), blk(4 * n_hp + 1, True)]
                + [mblk(o) for o in offs] + [mblk(4 * n_hp, True), mblk(4 * n_hp + 1, True)]
                + [pl.BlockSpec((8, LANES), lambda h: (0, h)),
                   pl.BlockSpec((LANES, LANES), lambda h: (0, h)),
                   pl.BlockSpec((LANES, LANES), lambda h: (0, h))])
    width = n_heads * HEAD_DIM
    return pl.pallas_call(
        functools.partial(_rwkv_sample_prep_kernel, steps=steps),
        grid=(n_hp,),
        in_specs=in_specs,
        out_specs=[pl.BlockSpec((steps, LANES, db), lambda h: (0, h, 0))] * 7,
        out_shape=[jax.ShapeDtypeStruct((steps, width, db), F32)] * 7,
        compiler_params=_params(("parallel",)),
        name="rwkv_sample_prep",
    )(*([proj] * 6 + [prev] * 6 + [mu_pad] * 6 + [hparams, w2p, a2p]))


def _rwkv_sample_kernel(r_ref, k_ref, v_ref, g_ref, kk_ref, b_ref, ld_ref, s_ref, hpt_ref, out_ref, so_ref, y_s, *,
                        steps):
    db = out_ref.shape[0] // steps

    for hh in range(2):
        r0 = hh * HEAD_DIM
        rows = slice(r0, r0 + HEAD_DIM)
        kk = [kk_ref[t, rows, :] for t in range(steps)]
        bb = [b_ref[t, rows, :] for t in range(steps)]
        kv = [k_ref[t, rows, :] for t in range(steps)]
        rv = [r_ref[t, rows, :] for t in range(steps)]
        decay = [jnp.exp(ld_ref[t, rows, :]) for t in range(steps)]

        def row_block(v, carry):
            s_v = s_ref[hh, v]
            for t in range(steps):
                s_kk = jnp.sum(s_v * kk[t], axis=0, keepdims=True)
                s_v = s_v * decay[t] - s_kk * bb[t] + v_ref[t, pl.ds(r0 + v, 1), :] * kv[t]
                y_s[t, pl.ds(r0 + v, 1), :] = jnp.sum(s_v * rv[t], axis=0, keepdims=True)
            so_ref[hh, v] = s_v
            return carry

        lax.fori_loop(0, HEAD_DIM, row_block, 0, unroll=2)

    hpt = hpt_ref[...]
    r_k, lnx_w, lnx_b = hpt[:, 4:5], hpt[:, 5:6], hpt[:, 6:7]
    for t in range(steps):
        outs = []
        for hh in range(2):
            rows = slice(hh * HEAD_DIM, (hh + 1) * HEAD_DIM)
            y = y_s[t, rows, :]
            mean = jnp.mean(y, axis=0, keepdims=True)
            d = y - mean
            var = jnp.mean(d * d, axis=0, keepdims=True)
            yn = d * lax.rsqrt(var + RWKV_GN_EPS) * lnx_w[rows] + lnx_b[rows]
            bonus = jnp.sum(r_ref[t, rows, :] * k_ref[t, rows, :] * r_k[rows], axis=0, keepdims=True) * v_ref[t, rows, :]
            outs.append((yn + bonus) * _silu(g_ref[t, rows, :]))
        out_ref[t * db:(t + 1) * db, :] = jnp.concatenate(outs, axis=0).T


def _rwkv_sample(prepped, state, hparams_t, *, n_heads):
    steps, width, db = prepped[0].shape
    tok = pl.BlockSpec((steps, LANES, db), lambda j: (0, j, 0))
    st = pl.BlockSpec((2, HEAD_DIM, HEAD_DIM, db), lambda j: (j, 0, 0, 0))
    return pl.pallas_call(
        functools.partial(_rwkv_sample_kernel, steps=steps),
        grid=(n_heads // 2,),
        in_specs=[tok] * 7 + [st, pl.BlockSpec((LANES, 8), lambda j: (j, 0))],
        out_specs=[pl.BlockSpec((steps * db, LANES), lambda j: (0, j)), st],
        out_shape=[jax.ShapeDtypeStruct((steps * db, width), F32),
                   jax.ShapeDtypeStruct(state.shape, F32)],
        scratch_shapes=[pltpu.VMEM((steps, LANES, db), F32)],
        compiler_params=_params(("parallel",)),
        name="rwkv_sample",
    )(*prepped, state, hparams_t)


def _diff_lambda(lam_ref, lam_init):
    lp = lam_ref[...]
    s1 = jnp.sum(lp[0:1] * lp[1:2], axis=1, keepdims=True)
    s2 = jnp.sum(lp[2:3] * lp[3:4], axis=1, keepdims=True)
    return jnp.exp(s1) - jnp.exp(s2) + lam_init


def _diff_finish(o1, o2, lam, subw, ag, lam_init):
    o = o1 - lam * o2
    o = o * lax.rsqrt(jnp.mean(o * o, axis=-1, keepdims=True) + NORM_EPS)
    return o * subw * (1.0 - lam_init) * _silu(ag)


def _stack_maps(q):
    m0, m1 = _head_masks(q.shape)
    return jnp.concatenate([q * m0, q * m1], axis=0)


def _attn_prompt_kernel(q_ref, k_ref, v_ref, ag_ref, lam_ref, subw_ref, o_ref, ko_ref, vo_ref,
                        kb_s, vt_s, qt_s, m_s, l_s, acc_s, *, tq, tk, sub, lam_init):
    qi = pl.program_id(2)

    @pl.when(qi == 0)
    def _():
        ko_ref[...] = k_ref[...]
        vo_ref[...] = v_ref[...]
        kb_s[...] = k_ref[0].astype(BF16)
        vt_s[...] = v_ref[0].T.astype(BF16)

    qt_s[...] = _stack_maps(q_ref[0] * ATTN_SCALE).T.astype(BF16)
    m_s[...] = jnp.full_like(m_s, NEG_INF)
    l_s[...] = jnp.zeros_like(l_s)
    acc_s[...] = jnp.zeros_like(acc_s)
    n_sub = 2 * tq // sub

    def block(k0, diag_off):
        def width(j):
            if diag_off is None:
                return tk
            return max(0, min(tk, (j * sub) % tq + sub - diag_off))

        def scores(j):
            kw = width(j)
            if kw == 0:
                return None
            s = _dot(kb_s[pl.ds(k0, kw), :], qt_s[:, j * sub:(j + 1) * sub])
            if diag_off is not None:
                q_pos = (_iota2(s.shape, 1) + j * sub) & (tq - 1)
                s = jnp.where(_iota2(s.shape, 0) + diag_off <= q_pos, s, NEG_INF)
            return s

        s_next = scores(0)
        for j in range(n_sub):
            s = s_next
            if j + 1 < n_sub:
                s_next = scores(j + 1)
            if s is None:
                continue
            cols = slice(j * sub, (j + 1) * sub)
            m_old = m_s[:, cols]
            m_new = jnp.maximum(m_old, jnp.max(s, axis=0, keepdims=True))
            alpha = jnp.exp(m_old - m_new)
            p = jnp.exp(s - m_new)
            l_s[:, cols] = alpha * l_s[:, cols] + jnp.sum(p, axis=0, keepdims=True)
            acc_s[:, cols] = alpha * acc_s[:, cols] + _dot(vt_s[:, pl.ds(k0, width(j))], p.astype(BF16))
            m_s[:, cols] = m_new

    def body(ki, carry):
        block(pl.multiple_of(ki * tk, tk), None)
        return carry

    lax.fori_loop(0, qi * (tq // tk), body, 0)
    for d in range(tq // tk):
        block(pl.multiple_of(qi * tq + d * tk, tk), d * tk)
    o = acc_s[...] / l_s[...]
    lam = _diff_lambda(lam_ref, lam_init)
    o_ref[0] = _diff_finish(o[:, :tq].T, o[:, tq:].T, lam, subw_ref[...], ag_ref[0], lam_init).astype(o_ref.dtype)


def _attn_prompt(proj, lam_params, subw, *, n_heads, at0, tq, lam_init):
    B, T, _ = proj.shape
    qblk = lambda off: pl.BlockSpec((1, tq, LANES), lambda b, h, i: (b, i, off + h))
    kblk = lambda off: pl.BlockSpec((1, T, LANES), lambda b, h, i: (b, 0, off + h))
    return pl.pallas_call(
        functools.partial(_attn_prompt_kernel, tq=tq, tk=min(512, tq), sub=min(256, 2 * tq), lam_init=lam_init),
        grid=(B, n_heads, T // tq),
        in_specs=[qblk(at0), kblk(at0 + n_heads), kblk(at0 + 2 * n_heads), qblk(at0 + 3 * n_heads),
                  pl.BlockSpec(lam_params.shape, lambda b, h, i: (0, 0)),
                  pl.BlockSpec((1, LANES), lambda b, h, i: (0, 0))],
        out_specs=[pl.BlockSpec((1, tq, LANES), lambda b, h, i: (b, i, h)),
                   pl.BlockSpec((1, T, LANES), lambda b, h, i: (b, 0, h)),
                   pl.BlockSpec((1, T, LANES), lambda b, h, i: (b, 0, h))],
        out_shape=[jax.ShapeDtypeStruct((B, T, n_heads * LANES), BF16),
                   jax.ShapeDtypeStruct((B, T, n_heads * LANES), F32),
                   jax.ShapeDtypeStruct((B, T, n_heads * LANES), F32)],
        scratch_shapes=[pltpu.VMEM((T, LANES), BF16), pltpu.VMEM((LANES, T), BF16), pltpu.VMEM((LANES, 2 * tq), BF16),
                        pltpu.VMEM((1, 2 * tq), F32), pltpu.VMEM((1, 2 * tq), F32), pltpu.VMEM((LANES, 2 * tq), F32)],
        compiler_params=_params(("parallel", "parallel", "arbitrary")),
        name="attn_prompt",
    )(proj, proj, proj, proj, lam_params, subw)


def _attn_sample_kernel(pt_ref, q_ref, kn_ref, vn_ref, ag_ref, *refs, n_heads, steps, pages, lam_init):
    del pt_ref
    k_refs, v_refs = refs[:pages], refs[pages:2 * pages]
    lam_ref, subw_ref, o_ref, m_s, l_s, acc_s = refs[2 * pages:]
    gi = pl.program_id(1)
    n_groups = pl.num_programs(1)
    rows_h = 2 * steps
    rows = n_heads * rows_h
    kdim, page = k_refs[0].shape[1], k_refs[0].shape[2]
    assert steps & (steps - 1) == 0

    @pl.when(gi == 0)
    def _():
        m_s[...] = jnp.full_like(m_s, NEG_INF)
        l_s[...] = jnp.zeros_like(l_s)
        acc_s[...] = jnp.zeros_like(acc_s)

    q_all = q_ref[0] * ATTN_SCALE
    hs = lambda h: slice(h * LANES, (h + 1) * LANES)
    q_rep = jnp.concatenate([q_all] * (rows // steps), axis=0)
    col_grp = lax.shift_right_logical(_iota2((rows, kdim), 1), int(math.log2(HEAD_DIM)))
    row_grp = lax.shift_right_logical(_iota2((rows, kdim), 0), int(math.log2(steps)))
    q_bd = jnp.where(col_grp == row_grp, q_rep, 0.0).astype(BF16)

    s = jnp.concatenate([_dot(q_bd, k_refs[i][0].astype(BF16)) for i in range(pages)], axis=1)
    m_old = m_s[...]
    m_run = jnp.maximum(m_old, jnp.max(s, axis=1, keepdims=True))
    alpha = jnp.exp(m_old - m_run)
    p = jnp.exp(s - m_run)
    l_run = alpha * l_s[...] + jnp.sum(p, axis=1, keepdims=True)
    pv = []
    for h in range(n_heads):
        v_h = jnp.concatenate([v_refs[i][0, pl.ds(h, page, stride=n_heads), :].astype(BF16) for i in range(pages)],
                              axis=0)
        pv.append(_dot(p[h * rows_h:(h + 1) * rows_h].astype(BF16), v_h))
    acc = alpha * acc_s[...] + jnp.concatenate(pv, axis=0)
    m_s[...] = m_run
    l_s[...] = l_run
    acc_s[...] = acc

    @pl.when(gi == n_groups - 1)
    def _():
        lam = _diff_lambda(lam_ref, lam_init)
        tok = _iota2((rows_h, LANES), 0) & (steps - 1)
        lane = _iota2((rows_h, LANES), 1)
        for h in range(n_heads):
            rs = slice(h * rows_h, (h + 1) * rows_h)
            q2 = _stack_maps(q_all[:, hs(h)])
            kn = kn_ref[0][:, hs(h)]
            vn = vn_ref[0][:, hs(h)]
            s = jnp.full((rows_h, LANES), NEG_INF, F32)
            for j in range(steps):
                sj = jnp.sum(q2 * kn[j:j + 1, :], axis=1, keepdims=True)
                s = jnp.where((lane == j) & (tok >= j), sj, s)
            m_old = m_run[rs]
            m_new = jnp.maximum(m_old, jnp.max(s, axis=1, keepdims=True))
            alpha = jnp.exp(m_old - m_new)
            p = jnp.exp(s - m_new)
            l_fin = alpha * l_run[rs] + jnp.sum(p, axis=1, keepdims=True)
            a_fin = alpha * acc[rs]
            for j in range(steps):
                a_fin = a_fin + p[:, j:j + 1] * vn[j:j + 1, :]
            o = a_fin / l_fin
            o_ref[0, :, hs(h)] = _diff_finish(o[:steps], o[steps:], lam, subw_ref[...], ag_ref[0][:, hs(h)], lam_init)


def _attn_sample(qkvg, cache_k, cache_v, page_table, lam_params, subw, *, n_heads, pages, lam_init):
    db, steps, _ = qkvg.shape
    n_pages = page_table.shape[0] // db
    n_groups = n_pages // pages
    width = n_heads * LANES
    rows = n_heads * 2 * steps
    tokblk = lambda off: pl.BlockSpec((1, steps, width), lambda b, g, pt: (b, 0, off))
    pageblk = lambda cache, i: pl.BlockSpec((1,) + cache.shape[1:],
                                            lambda b, g, pt: (pt[(b * n_groups + g) * pages + i], 0, 0))
    grid_spec = pltpu.PrefetchScalarGridSpec(
        num_scalar_prefetch=1,
        grid=(db, n_groups),
        in_specs=([tokblk(0), tokblk(1), tokblk(2), tokblk(3)]
                  + [pageblk(cache_k, i) for i in range(pages)] + [pageblk(cache_v, i) for i in range(pages)]
                  + [pl.BlockSpec(lam_params.shape, lambda b, g, pt: (0, 0)),
                     pl.BlockSpec((1, LANES), lambda b, g, pt: (0, 0))]),
        out_specs=pl.BlockSpec((1, steps, width), lambda b, g, pt: (b, 0, 0)),
        scratch_shapes=[pltpu.VMEM((rows, 1), F32), pltpu.VMEM((rows, 1), F32), pltpu.VMEM((rows, LANES), F32)])
    return pl.pallas_call(
        functools.partial(_attn_sample_kernel, n_heads=n_heads, steps=steps, pages=pages, lam_init=lam_init),
        grid_spec=grid_spec,
        out_shape=jax.ShapeDtypeStruct((db, steps, width), F32),
        compiler_params=_params(("parallel", "arbitrary")),
        name="attn_sample",
    )(page_table, qkvg, qkvg, qkvg, qkvg, *([cache_k] * pages), *([cache_v] * pages), lam_params, subw)


def _outproj_kernel(x_ref, rw_ref, at_ref, grw_ref, gat_ref, wbr_ref, wba_ref, wo_ref, nf_ref, o_ref):
    merged = (_sigmoid(grw_ref[...]) * _dot(rw_ref[...], wbr_ref[...])
              + _sigmoid(gat_ref[...]) * _dot(at_ref[...], wba_ref[...]))
    y = x_ref[...] + _dot(merged.astype(BF16), wo_ref[...])
    o_ref[...] = y * lax.rsqrt(jnp.mean(y * y, axis=-1, keepdims=True) + NORM_EPS) * nf_ref[...]


def _outproj(x, rw, at, proj, wbr, wba, wo, nf, *, tm):
    n, d = x.shape
    row = lambda w: pl.BlockSpec((tm, w), lambda i: (i, 0))
    const = lambda a: pl.BlockSpec(a.shape, lambda i: (0, 0), pipeline_mode=pl.Buffered(1))
    return pl.pallas_call(
        _outproj_kernel,
        grid=(n // tm,),
        in_specs=[row(d), row(rw.shape[1]), row(at.shape[1]), row(d),
                  pl.BlockSpec((tm, d), lambda i: (i, 1)),
                  const(wbr), const(wba), const(wo), const(nf)],
        out_specs=row(d),
        out_shape=jax.ShapeDtypeStruct((n, d), F32),
        compiler_params=_params(("parallel",)),
        name="outproj",
    )(x, rw, at, proj, proj, wbr, wba, wo, nf)


def _largest_tile(n, cap):
    t = cap
    while n % t:
        t //= 2
    return t


def kernel(x_prompt, x_sample, cache_k, cache_v, state_wkv, state_shift, page_table, norm_in, w_in, mu_shift, w0, w2, a0, a2, k_k, k_a, r_k, lnx_w, lnx_b, lambda_q1, lambda_k1, lambda_q2, lambda_k2, subln_w, w_br_rwkv, w_br_attn, w_out, norm_f):
    depth = w_in.shape[0]
    assert depth == 1, "single-layer step"
    B, T, D = x_prompt.shape
    DB, TS, _ = x_sample.shape
    r_heads = r_k.shape[1]
    r_width = r_heads * HEAD_DIM
    lora = w2.shape[1]
    a_heads = cache_k.shape[3]
    a_width = a_heads * LANES
    rw_in = 4 * r_width + 2 * lora
    lam_init = 0.8 - 0.6 * math.exp(-0.3 * 0)

    segments = ((rw_in + 4 * a_width, 2 * D, 0), (0, 4 * r_width, 0), (rw_in, 4 * a_width, 0),
                (4 * r_width, lora, LANES - lora), (4 * r_width + lora, lora, LANES - lora))
    w_pad = _regroup_weights(jnp.transpose(w_in[0]), segments, tc=_largest_tile(D, 128))
    rw0 = 2 * D // LANES
    at0 = rw0 + 4 * r_width // LANES
    lora0 = at0 + 4 * a_width // LANES
    pw = w_pad.shape[0]

    mu_pad, hparams, w2p, a2p = _rwkv_params(mu_shift[0], w0[0], w2[0], a0[0], a2[0], k_k[0], k_a[0], r_k[0],
                                             lnx_w[0], lnx_b[0])
    lam_params = jnp.stack([lambda_q1[0], lambda_k1[0], lambda_q2[0], lambda_k2[0]]).astype(F32)
    subw = subln_w[0][None, :].astype(F32)
    g_in = norm_in[0][None, :].astype(F32)
    wbr = w_br_rwkv[0].astype(BF16)
    wba = w_br_attn[0].astype(BF16)
    wo = w_out[0].astype(BF16)
    nf = norm_f[None, :].astype(F32)
    tn = 14 * LANES
    assert pw % tn == 0

    n_p = B * T
    xp = x_prompt.reshape(n_p, D)
    proj_p = _inproj(xp, g_in, w_pad, tm=_largest_tile(n_p, 1024), tn=tn)
    proj_p3 = proj_p.reshape(B, T, pw)
    rw_p, wkv_p = _rwkv_prompt(proj_p3, mu_pad, hparams, w2p, a2p, n_heads=r_heads, rw0=rw0, lora0=lora0,
                               tb=_largest_tile(T, 512), ns=_largest_tile(r_heads // 2, 8))
    at_p, k_p, v_p = _attn_prompt(proj_p3, lam_params, subw, n_heads=a_heads, at0=at0, tq=_largest_tile(T, 2048),
                                  lam_init=lam_init)
    y_p = _outproj(xp, rw_p.reshape(n_p, r_width), at_p.reshape(n_p, a_width), proj_p, wbr, wba, wo, nf,
                   tm=_largest_tile(n_p, 256))

    n_s = DB * TS
    xs = jnp.transpose(x_sample, (1, 0, 2)).reshape(n_s, D)
    proj_s = _inproj(xs, g_in, w_pad, tm=_largest_tile(n_s, 512), tn=tn)
    c0 = rw0 * LANES
    zs = jnp.zeros((DB, LANES - lora), F32)
    sh = state_shift[0]
    shift_pad = jnp.concatenate([sh[:, :4 * r_width], sh[:, 4 * r_width:4 * r_width + lora], zs,
                                 sh[:, 4 * r_width + lora:], zs], axis=-1)
    rwkv_cols = jnp.concatenate([proj_s[:, c0:c0 + 4 * r_width], proj_s[:, lora0 * LANES:]], axis=-1)
    prev = jnp.concatenate([shift_pad, rwkv_cols[:-DB]], axis=0)
    prepped = _rwkv_sample_prep(proj_s, prev, mu_pad, hparams, w2p, a2p, n_heads=r_heads, rw0=rw0, lora0=lora0,
                                steps=TS)
    rw_s, wkv_s = _rwkv_sample(prepped, jnp.transpose(state_wkv[0], (1, 2, 3, 0)), hparams.T, n_heads=r_heads)
    wkv_s = jnp.transpose(wkv_s, (3, 0, 1, 2))
    n_pool, page = cache_k.shape[1], cache_k.shape[2]
    n_pages = page_table.shape[1]
    q0 = at0 * LANES
    qkvg_s = jnp.transpose(proj_s[:, q0:q0 + 4 * a_width].reshape(TS, DB, 4 * a_width), (1, 0, 2))
    kc = jnp.transpose(cache_k[0], (0, 2, 3, 4, 1)).reshape(n_pool, a_width, page)
    at_s = _attn_sample(qkvg_s, kc, cache_v[0].reshape(n_pool, page * a_heads, LANES), page_table.reshape(-1),
                        lam_params, subw, n_heads=a_heads, pages=_largest_tile(n_pages, 16), lam_init=lam_init)
    at_s = jnp.transpose(at_s, (1, 0, 2)).reshape(n_s, a_width)
    y_s = _outproj(xs, rw_s.astype(BF16), at_s.astype(BF16), proj_s, wbr, wba, wo, nf, tm=_largest_tile(n_s, 256))
    y_s = jnp.transpose(y_s.reshape(TS, DB, D), (1, 0, 2))

    def shift_row(last):
        return jnp.concatenate([last[:, c0:c0 + 4 * r_width], last[:, lora0 * LANES:lora0 * LANES + lora],
                                last[:, (lora0 + 1) * LANES:(lora0 + 1) * LANES + lora]], axis=-1)[None]

    return (y_p.reshape(B, T, D), y_s,
            k_p.reshape(1, B, T, a_heads, 2, HEAD_DIM), v_p.reshape(1, B, T, a_heads, LANES),
            qkvg_s[:, :, a_width:2 * a_width].reshape(1, DB, TS, a_heads, 2, HEAD_DIM),
            qkvg_s[:, :, 2 * a_width:3 * a_width].reshape(1, DB, TS, a_heads, LANES),
            wkv_p[None], wkv_s[None], shift_row(proj_p3[:, -1]), shift_row(proj_s[-DB:]))
```

```python
import functools
import math

import jax
import jax.numpy as jnp
from jax import lax
from jax.experimental import pallas as pl
from jax.experimental.pallas import tpu as pltpu

F32 = jnp.float32
BF16 = jnp.bfloat16

LANES = 128
HEAD_DIM = 64
CHUNK = 64
NORM_EPS = 1e-6
RWKV_GN_EPS = 64e-5
NEG_INF = -1e30
ATTN_SCALE = HEAD_DIM ** -0.5
VMEM_LIMIT = 56 * 1024 * 1024

HI = lax.Precision.HIGHEST


def _dot(a, b, prec=None):
    return lax.dot_general(a, b, (((1,), (0,)), ((), ())), precision=prec, preferred_element_type=F32)


def _dot_nt(a, b, prec=None):
    return lax.dot_general(a, b, (((1,), (1,)), ((), ())), precision=prec, preferred_element_type=F32)


def _dot_tn(a, b, prec=None):
    return lax.dot_general(a, b, (((0,), (0,)), ((), ())), precision=prec, preferred_element_type=F32)


def _iota2(shape, dim):
    return lax.broadcasted_iota(jnp.int32, shape, dim)


def _sigmoid(x):
    return 1.0 / (1.0 + jnp.exp(-x))


def _silu(x):
    return x * _sigmoid(x)


def _params(semantics):
    return pltpu.CompilerParams(dimension_semantics=semantics, vmem_limit_bytes=VMEM_LIMIT)


def _regroup_kernel(w_ref, o_ref, *, segments):
    row = 0
    for start, width, pad in segments:
        o_ref[row:row + width, :] = w_ref[start:start + width, :].astype(o_ref.dtype)
        row += width
        if pad:
            o_ref[row:row + pad, :] = jnp.zeros((pad, o_ref.shape[1]), o_ref.dtype)
            row += pad


def _regroup_weights(w_t, segments, *, tc):
    cols, d = w_t.shape
    out_cols = sum(width + pad for _, width, pad in segments)
    return pl.pallas_call(
        functools.partial(_regroup_kernel, segments=segments),
        grid=(d // tc,),
        in_specs=[pl.BlockSpec((cols, tc), lambda i: (0, i))],
        out_specs=pl.BlockSpec((out_cols, tc), lambda i: (0, i)),
        out_shape=jax.ShapeDtypeStruct((out_cols, d), BF16),
        compiler_params=_params(("parallel",)),
        name="regroup_weights",
    )(w_t)


def _inproj_kernel(x_ref, g_ref, w_ref, o_ref, xn_ref):
    @pl.when(pl.program_id(1) == 0)
    def _():
        x = x_ref[...]
        xn = x * lax.rsqrt(jnp.mean(x * x, axis=-1, keepdims=True) + NORM_EPS) * g_ref[...]
        xn_ref[...] = xn.astype(BF16)

    o_ref[...] = _dot_nt(xn_ref[...], w_ref[...])


def _inproj(x, g, w_t, *, tm, tn):
    n, d = x.shape
    pw = w_t.shape[0]
    return pl.pallas_call(
        _inproj_kernel,
        grid=(n // tm, pw // tn),
        in_specs=[pl.BlockSpec((tm, d), lambda i, j: (i, 0)),
                  pl.BlockSpec((1, d), lambda i, j: (0, 0)),
                  pl.BlockSpec((tn, d), lambda i, j: (j, 0))],
        out_specs=pl.BlockSpec((tm, tn), lambda i, j: (i, j)),
        out_shape=jax.ShapeDtypeStruct((n, pw), F32),
        scratch_shapes=[pltpu.VMEM((tm, d), BF16)],
        compiler_params=_params(("parallel", "arbitrary")),
        name="inproj",
    )(x, g, w_t)


def _rwkv_prepare(ps_r, ps_k, ps_v, ps_g, ps_wd, ps_ad, hp, w2, a2, gsum):
    w0, a0, k_k, k_a = hp[0:1], hp[1:2], hp[2:3], hp[3:4]
    w_pre = w0 + _dot(jnp.tanh(ps_wd).astype(BF16), w2)
    log_decay = -math.exp(-0.5) * _sigmoid(w_pre)
    a = _sigmoid(a0 + _dot(ps_ad.astype(BF16), a2))
    kk = ps_k * k_k
    kk = kk * lax.rsqrt(jnp.maximum(gsum(kk * kk), 1e-24))
    k = ps_k * (1.0 + (a - 1.0) * k_a)
    return ps_r, k, ps_v, ps_g, kk, kk * a, log_decay


def _rwkv_finish(y, r, k, v, g, hp, gsum):
    r_k, lnx_w, lnx_b = hp[4:5], hp[5:6], hp[6:7]
    mean = gsum(y) * (1.0 / HEAD_DIM)
    d = y - mean
    var = gsum(d * d) * (1.0 / HEAD_DIM)
    yn = d * lax.rsqrt(var + RWKV_GN_EPS) * lnx_w + lnx_b
    bonus = gsum(r * k * r_k) * v
    return (yn + bonus) * _silu(g)


def _head_masks(shape):
    lane = _iota2(shape, 1)
    m0 = (lane < HEAD_DIM).astype(F32)
    return m0, 1.0 - m0


def _group_ones():
    i = _iota2((LANES, LANES), 0) // HEAD_DIM
    j = _iota2((LANES, LANES), 1) // HEAD_DIM
    return (i == j).astype(BF16)


def _gsum_wide(x):
    ones = _group_ones()
    hi = x.astype(BF16)
    lo = (x - hi.astype(F32)).astype(BF16)
    parts = []
    for j in range(x.shape[1] // LANES):
        ls = slice(j * LANES, (j + 1) * LANES)
        parts.append(_dot(hi[:, ls], ones) + _dot(lo[:, ls], ones))
    return parts[0] if len(parts) == 1 else jnp.concatenate(parts, axis=1)


def _rwkv_prompt_kernel(pr_ref, pk_ref, pv_ref, pg_ref, pwd_ref, pad_ref,
                        mr_ref, mk_ref, mv_ref, mg_ref, mwd_ref, mad_ref,
                        hp_ref, w2_ref, a2_ref,
                        out_ref, state_ref,
                        carry_ref, cw_ref, m_ref, r_s, k_s, v_s, g_s, kk_s, b_s, ld_s, *, tb, ns):
    t_idx = pl.program_id(2)
    n_t = pl.num_programs(2)
    C = CHUNK

    @pl.when(t_idx == 0)
    def _():
        carry_ref[...] = jnp.zeros_like(carry_ref)
        cw_ref[...] = jnp.zeros_like(cw_ref)
        m_ref[...] = jnp.zeros_like(m_ref)

    def shifted(p_ref, mu_ref, c_ref, idx):
        p = p_ref[0]
        row = _iota2(p.shape, 0)
        prev = pltpu.roll(p, 1, axis=0)
        prev = jnp.where(row == 0, c_ref[idx], prev)
        c_ref[idx] = p[tb - 1:tb, :]
        return p + (prev - p) * mu_ref[...]

    hp = hp_ref[...]
    r, k, v, g, kk, b, ld = _rwkv_prepare(
        shifted(pr_ref, mr_ref, carry_ref, 0), shifted(pk_ref, mk_ref, carry_ref, 1),
        shifted(pv_ref, mv_ref, carry_ref, 2), shifted(pg_ref, mg_ref, carry_ref, 3),
        shifted(pwd_ref, mwd_ref, cw_ref, 0), shifted(pad_ref, mad_ref, cw_ref, 1),
        hp, w2_ref[...], a2_ref[...], _gsum_wide)
    r_s[...] = r
    k_s[...] = k
    v_s[...] = v
    g_s[...] = g
    kk_s[...] = kk
    b_s[...] = b
    ld_s[...] = ld

    m0, m1 = _head_masks((C, LANES))
    ii = _iota2((2 * C, 2 * C), 0)
    jj = _iota2((2 * C, 2 * C), 1)
    strict = ii > jj
    incl = ii >= jj
    eye = ii == jj
    row_c = _iota2((C, LANES), 0)
    first = m0 > 0.5

    def cumsum_rows(x):
        shift = 1
        while shift < C:
            x = x + jnp.where(row_c >= shift, pltpu.roll(x, shift, axis=0), 0.0)
            shift *= 2
        return x

    def head_sums(x):
        s0 = jnp.sum(x * m0, axis=1, keepdims=True)
        s1 = jnp.sum(x * m1, axis=1, keepdims=True)
        return jnp.where(first, s0, s1)

    stack2 = lambda x: jnp.concatenate([x * m0, x * m1], axis=0).astype(BF16)
    bf = lambda x: x.astype(BF16)

    lanes = [slice(s * LANES, (s + 1) * LANES) for s in range(ns)]
    each = lambda f, *cols: [f(*args) for args in zip(*cols)]

    def chunk(ci, ms):
        sl = pl.ds(pl.multiple_of(ci * C, C), C)
        ld = [ld_s[sl, ls] for ls in lanes]
        c = each(cumsum_rows, ld)
        c_end = each(lambda x: x[C - 1:C, :], c)
        p_inv = each(lambda x: jnp.exp(-x), c)
        p_end = each(lambda e, x: jnp.exp(e - x), c_end, c)
        r2 = each(lambda ls, x: stack2(r_s[sl, ls] * jnp.exp(x)), lanes, c)
        kk2 = each(lambda ls, x, d: stack2(kk_s[sl, ls] * jnp.exp(x - d)), lanes, c, ld)
        kh2 = each(lambda ls, p: stack2(k_s[sl, ls] * p), lanes, p_inv)
        bh2 = each(lambda ls, p: stack2(b_s[sl, ls] * p), lanes, p_inv)
        ke2 = each(lambda ls, p: stack2(k_s[sl, ls] * p), lanes, p_end)
        be2 = each(lambda ls, p: stack2(b_s[sl, ls] * p), lanes, p_end)
        v2 = each(lambda ls: stack2(v_s[sl, ls]), lanes)
        a = each(lambda w, x, y, z: _dot_nt(jnp.concatenate([w, x], axis=0), jnp.concatenate([y, z], axis=0)),
                 kk2, r2, kh2, bh2)
        l_k = each(lambda x: bf(jnp.where(strict, x[:2 * C, :2 * C], 0.0)), a)
        l_b = each(lambda x: bf(jnp.where(strict, x[:2 * C, 2 * C:], 0.0)), a)
        a_rk = each(lambda x: bf(jnp.where(incl, x[2 * C:, :2 * C], 0.0)), a)
        a_rb = each(lambda x: bf(jnp.where(incl, -x[2 * C:, 2 * C:], 0.0)), a)
        mb = each(bf, ms)
        x = each(lambda w, lk, m, v: _dot(jnp.concatenate([w, lk], axis=1), jnp.concatenate([m, v], axis=0)),
                 kk2, l_k, mb, v2)
        pw = l_b
        u = each(lambda x_, p: x_ - _dot(p, bf(x_)), x, pw)
        for _ in range(5):
            pw = each(lambda p: bf(_dot(p, p)), pw)
            u = each(lambda u_, p: u_ + _dot(p, bf(u_)), u, pw)
        ub = each(bf, u)
        y2 = each(lambda r_, ak, ab, m, v, u_: _dot(jnp.concatenate([r_, ak, ab], axis=1),
                                                     jnp.concatenate([m, v, u_], axis=0)),
                  r2, a_rk, a_rb, mb, v2, ub)
        p_c_col = each(lambda e: jnp.sum(jnp.where(eye, jnp.broadcast_to(jnp.exp(e), (2 * C, LANES)), 0.0),
                                         axis=1, keepdims=True), c_end)
        m_new = each(lambda m, pc, ke, be, v, u_: m * pc + _dot_tn(jnp.concatenate([ke, -be], axis=0),
                                                                   jnp.concatenate([v, u_], axis=0)),
                     ms, p_c_col, ke2, be2, v2, ub)
        for ls, y in zip(lanes, y2):
            out = _rwkv_finish(y[:C] + y[C:], r_s[sl, ls], k_s[sl, ls], v_s[sl, ls], g_s[sl, ls], hp[:, ls],
                               head_sums)
            out_ref[0, sl, ls] = out.astype(out_ref.dtype)
        return tuple(m_new)

    ms = lax.fori_loop(0, tb // C, chunk, tuple(m_ref[s] for s in range(ns)))
    for s in range(ns):
        m_ref[s] = ms[s]

    @pl.when(t_idx == n_t - 1)
    def _():
        for s in range(ns):
            mt = ms[s].T
            state_ref[0, 2 * s] = mt[:HEAD_DIM, :HEAD_DIM]
            state_ref[0, 2 * s + 1] = mt[HEAD_DIM:, HEAD_DIM:]


def _rwkv_prompt(proj, mu_pad, hparams, w2p, a2p, *, n_heads, rw0, lora0, tb, ns):
    B, T, _ = proj.shape
    n_hp = n_heads // 2
    assert rw0 % ns == 0 and n_hp % ns == 0
    n_grp = n_hp // ns
    wide = ns * LANES
    col = lambda off: (lambda b, h, t: (b, t, off // ns + h))
    colc = lambda off: (lambda b, h, t: (b, t, off))
    mcol = lambda off: (lambda b, h, t: (0, off // ns + h))
    mcolc = lambda off: (lambda b, h, t: (0, off))
    pblk = lambda im: pl.BlockSpec((1, tb, wide), im)
    mblk = lambda im: pl.BlockSpec((1, wide), im)
    in_specs = [pblk(col(rw0)), pblk(col(rw0 + n_hp)), pblk(col(rw0 + 2 * n_hp)), pblk(col(rw0 + 3 * n_hp)),
                pl.BlockSpec((1, tb, LANES), colc(lora0)), pl.BlockSpec((1, tb, LANES), colc(lora0 + 1)),
                mblk(mcol(0)), mblk(mcol(n_hp)), mblk(mcol(2 * n_hp)), mblk(mcol(3 * n_hp)),
                pl.BlockSpec((1, LANES), mcolc(4 * n_hp)), pl.BlockSpec((1, LANES), mcolc(4 * n_hp + 1)),
                pl.BlockSpec((8, wide), lambda b, h, t: (0, h)),
                pl.BlockSpec((LANES, wide), lambda b, h, t: (0, h)),
                pl.BlockSpec((LANES, wide), lambda b, h, t: (0, h))]
    out_specs = [pl.BlockSpec((1, tb, wide), lambda b, h, t: (b, t, h)),
                 pl.BlockSpec((1, 2 * ns, HEAD_DIM, HEAD_DIM), lambda b, h, t: (b, h, 0, 0))]
    scratch = ([pltpu.VMEM((4, 1, wide), F32), pltpu.VMEM((2, 1, LANES), F32), pltpu.VMEM((ns, LANES, LANES), F32)]
               + [pltpu.VMEM((tb, wide), F32)] * 7)
    return pl.pallas_call(
        functools.partial(_rwkv_prompt_kernel, tb=tb, ns=ns),
        grid=(B, n_grp, T // tb),
        in_specs=in_specs, out_specs=out_specs,
        out_shape=[jax.ShapeDtypeStruct((B, T, n_heads * HEAD_DIM), BF16),
                   jax.ShapeDtypeStruct((B, n_heads, HEAD_DIM, HEAD_DIM), F32)],
        scratch_shapes=scratch,
        compiler_params=_params(("parallel", "parallel", "arbitrary")),
        name="rwkv_prompt",
    )(proj, proj, proj, proj, proj, proj, mu_pad, mu_pad, mu_pad, mu_pad, mu_pad, mu_pad, hparams, w2p, a2p)


def _rwkv_params(mu, w0, w2, a0, a2, k_k, k_a, r_k, lnx_w, lnx_b):
    width = w0.shape[0]
    lora = w2.shape[0]
    zpad = jnp.zeros((LANES - lora,), F32)
    mu_pad = jnp.concatenate([mu[:4 * width], mu[4 * width:4 * width + lora], zpad, mu[4 * width + lora:], zpad])[None, :]
    hparams = jnp.stack([w0, a0, k_k, k_a, r_k.reshape(-1), lnx_w, lnx_b, jnp.zeros_like(w0)]).astype(F32)
    pad_rows = lambda m: jnp.concatenate([m, jnp.zeros((LANES - lora, width), m.dtype)], axis=0).astype(BF16)
    return mu_pad.astype(F32), hparams, pad_rows(w2), pad_rows(a2)


def _rwkv_sample_prep_kernel(pr_ref, pk_ref, pv_ref, pg_ref, pwd_ref, pad_ref,
                             qr_ref, qk_ref, qv_ref, qg_ref, qwd_ref, qad_ref,
                             mr_ref, mk_ref, mv_ref, mg_ref, mwd_ref, mad_ref,
                             hp_ref, w2_ref, a2_ref,
                             r_out, k_out, v_out, g_out, kk_out, b_out, ld_out, *, steps):
    def shifted(p_ref, q_ref, mu_ref):
        p = p_ref[...]
        return p + (q_ref[...] - p) * mu_ref[...]

    outs = _rwkv_prepare(
        shifted(pr_ref, qr_ref, mr_ref), shifted(pk_ref, qk_ref, mk_ref), shifted(pv_ref, qv_ref, mv_ref),
        shifted(pg_ref, qg_ref, mg_ref), shifted(pwd_ref, qwd_ref, mwd_ref), shifted(pad_ref, qad_ref, mad_ref),
        hp_ref[...], w2_ref[...], a2_ref[...], _gsum_wide)
    db = r_out.shape[2]
    for o_ref, val in zip((r_out, k_out, v_out, g_out, kk_out, b_out, ld_out), outs):
        for t in range(steps):
            o_ref[t] = val[t * db:(t + 1) * db, :].T


def _rwkv_sample_prep(proj, prev, mu_pad, hparams, w2p, a2p, *, n_heads, rw0, lora0, steps):
    n = proj.shape[0]
    db = n // steps
    n_hp = n_heads // 2
    blk = lambda off, fixed=False: pl.BlockSpec((n, LANES), (lambda h: (0, off)) if fixed else (lambda h: (0, off + h)))
    mblk = lambda off, fixed=False: pl.BlockSpec((1, LANES), (lambda h: (0, off)) if fixed else (lambda h: (0, off + h)))
    offs = [0, n_hp, 2 * n_hp, 3 * n_hp]
    in_specs = ([blk(rw0 + o) for o in offs] + [blk(lora0, True), blk(lora0 + 1, True)]
                + [blk(o) for o in offs] + [blk(4 * n_hp, True), blk(4 * n_hp + 1, True)]
                + [mblk(o) for o in offs] + [mblk(4 * n_hp, True), mblk(4 * n_hp + 1, True)]
                + [pl.BlockSpec((8, LANES), lambda h: (0, h)),
                   pl.BlockSpec((LANES, LANES), lambda h: (0, h)),
                   pl.BlockSpec((LANES, LANES), lambda h: (0, h))])
    width = n_heads * HEAD_DIM
    return pl.pallas_call(
        functools.partial(_rwkv_sample_prep_kernel, steps=steps),
        grid=(n_hp,),
        in_specs=in_specs,
        out_specs=[pl.BlockSpec((steps, LANES, db), lambda h: (0, h, 0))] * 7,
        out_shape=[jax.ShapeDtypeStruct((steps, width, db), F32)] * 7,
        compiler_params=_params(("parallel",)),
        name="rwkv_sample_prep",
    )(*([proj] * 6 + [prev] * 6 + [mu_pad] * 6 + [hparams, w2p, a2p]))


def _rwkv_sample_kernel(r_ref, k_ref, v_ref, g_ref, kk_ref, b_ref, ld_ref, s_ref, hpt_ref, out_ref, so_ref, y_s, *,
                        steps):
    db = out_ref.shape[0] // steps

    for hh in range(2):
        r0 = hh * HEAD_DIM
        rows = slice(r0, r0 + HEAD_DIM)
        kk = [kk_ref[t, rows, :] for t in range(steps)]
        bb = [b_ref[t, rows, :] for t in range(steps)]
        kv = [k_ref[t, rows, :] for t in range(steps)]
        rv = [r_ref[t, rows, :] for t in range(steps)]
        decay = [jnp.exp(ld_ref[t, rows, :]) for t in range(steps)]

        def row_block(v, carry):
            s_v = s_ref[hh, v]
            for t in range(steps):
                s_kk = jnp.sum(s_v * kk[t], axis=0, keepdims=True)
                s_v = s_v * decay[t] - s_kk * bb[t] + v_ref[t, pl.ds(r0 + v, 1), :] * kv[t]
                y_s[t, pl.ds(r0 + v, 1), :] = jnp.sum(s_v * rv[t], axis=0, keepdims=True)
            so_ref[hh, v] = s_v
            return carry

        lax.fori_loop(0, HEAD_DIM, row_block, 0, unroll=2)

    hpt = hpt_ref[...]
    r_k, lnx_w, lnx_b = hpt[:, 4:5], hpt[:, 5:6], hpt[:, 6:7]
    for t in range(steps):
        outs = []
        for hh in range(2):
            rows = slice(hh * HEAD_DIM, (hh + 1) * HEAD_DIM)
            y = y_s[t, rows, :]
            mean = jnp.mean(y, axis=0, keepdims=True)
            d = y - mean
            var = jnp.mean(d * d, axis=0, keepdims=True)
            yn = d * lax.rsqrt(var + RWKV_GN_EPS) * lnx_w[rows] + lnx_b[rows]
            bonus = jnp.sum(r_ref[t, rows, :] * k_ref[t, rows, :] * r_k[rows], axis=0, keepdims=True) * v_ref[t, rows, :]
            outs.append((yn + bonus) * _silu(g_ref[t, rows, :]))
        out_ref[t * db:(t + 1) * db, :] = jnp.concatenate(outs, axis=0).T


def _rwkv_sample(prepped, state, hparams_t, *, n_heads):
    steps, width, db = prepped[0].shape
    tok = pl.BlockSpec((steps, LANES, db), lambda j: (0, j, 0))
    st = pl.BlockSpec((2, HEAD_DIM, HEAD_DIM, db), lambda j: (j, 0, 0, 0))
    return pl.pallas_call(
        functools.partial(_rwkv_sample_kernel, steps=steps),
        grid=(n_heads // 2,),
        in_specs=[tok] * 7 + [st, pl.BlockSpec((LANES, 8), lambda j: (j, 0))],
        out_specs=[pl.BlockSpec((steps * db, LANES), lambda j: (0, j)), st],
        out_shape=[jax.ShapeDtypeStruct((steps * db, width), F32),
                   jax.ShapeDtypeStruct(state.shape, F32)],
        scratch_shapes=[pltpu.VMEM((steps, LANES, db), F32)],
        compiler_params=_params(("parallel",)),
        name="rwkv_sample",
    )(*prepped, state, hparams_t)


def _diff_lambda(lam_ref, lam_init):
    lp = lam_ref[...]
    s1 = jnp.sum(lp[0:1] * lp[1:2], axis=1, keepdims=True)
    s2 = jnp.sum(lp[2:3] * lp[3:4], axis=1, keepdims=True)
    return jnp.exp(s1) - jnp.exp(s2) + lam_init


def _diff_finish(o1, o2, lam, subw, ag, lam_init):
    o = o1 - lam * o2
    o = o * lax.rsqrt(jnp.mean(o * o, axis=-1, keepdims=True) + NORM_EPS)
    return o * subw * (1.0 - lam_init) * _silu(ag)


def _stack_maps(q):
    m0, m1 = _head_masks(q.shape)
    return jnp.concatenate([q * m0, q * m1], axis=0)


def _attn_prompt_kernel(q_ref, k_ref, v_ref, ag_ref, lam_ref, subw_ref, o_ref, ko_ref, vo_ref,
                        kb_s, vt_s, qt_s, m_s, l_s, acc_s, *, tq, tk, sub, lam_init):
    qi = pl.program_id(2)

    @pl.when(qi == 0)
    def _():
        ko_ref[...] = k_ref[...]
        vo_ref[...] = v_ref[...]
        kb_s[...] = k_ref[0].astype(BF16)
        vt_s[...] = v_ref[0].T.astype(BF16)

    qt_s[...] = _stack_maps(q_ref[0] * (ATTN_SCALE * math.log2(math.e))).T.astype(BF16)
    m_s[...] = jnp.full_like(m_s, NEG_INF)
    l_s[...] = jnp.zeros_like(l_s)
    acc_s[...] = jnp.zeros_like(acc_s)
    n_sub = 2 * tq // sub

    def block(k0, diag_off):
        def width(j):
            if diag_off is None:
                return tk
            return max(0, min(tk, (j * sub) % tq + sub - diag_off))

        def scores(j):
            kw = width(j)
            if kw == 0:
                return None
            s = _dot(kb_s[pl.ds(k0, kw), :], qt_s[:, j * sub:(j + 1) * sub])
            if diag_off is not None:
                q_pos = (_iota2(s.shape, 1) + j * sub) & (tq - 1)
                s = jnp.where(_iota2(s.shape, 0) + diag_off <= q_pos, s, NEG_INF)
            return s

        s_next = scores(0)
        for j in range(n_sub):
            s = s_next
            if j + 1 < n_sub:
                s_next = scores(j + 1)
            if s is None:
                continue
            cols = slice(j * sub, (j + 1) * sub)
            m_old = m_s[:, cols]
            m_new = jnp.maximum(m_old, jnp.max(s, axis=0, keepdims=True))
            alpha = jnp.exp2(m_old - m_new)
            p = jnp.exp2(s - m_new)
            l_s[:, cols] = alpha * l_s[:, cols] + jnp.sum(p, axis=0, keepdims=True)
            acc_s[:, cols] = alpha * acc_s[:, cols] + _dot(vt_s[:, pl.ds(k0, width(j))], p.astype(BF16))
            m_s[:, cols] = m_new

    def body(ki, carry):
        block(pl.multiple_of(ki * tk, tk), None)
        return carry

    lax.fori_loop(0, qi * (tq // tk), body, 0)
    for d in range(tq // tk):
        block(pl.multiple_of(qi * tq + d * tk, tk), d * tk)
    o = acc_s[...] / l_s[...]
    lam = _diff_lambda(lam_ref, lam_init)
    o_ref[0] = _diff_finish(o[:, :tq].T, o[:, tq:].T, lam, subw_ref[...], ag_ref[0], lam_init).astype(o_ref.dtype)


def _attn_prompt(proj, lam_params, subw, *, n_heads, at0, tq, lam_init):
    B, T, _ = proj.shape
    qblk = lambda off: pl.BlockSpec((1, tq, LANES), lambda b, h, i: (b, i, off + h))
    kblk = lambda off: pl.BlockSpec((1, T, LANES), lambda b, h, i: (b, 0, off + h))
    return pl.pallas_call(
        functools.partial(_attn_prompt_kernel, tq=tq, tk=min(512, tq), sub=min(256, 2 * tq), lam_init=lam_init),
        grid=(B, n_heads, T // tq),
        in_specs=[qblk(at0), kblk(at0 + n_heads), kblk(at0 + 2 * n_heads), qblk(at0 + 3 * n_heads),
                  pl.BlockSpec(lam_params.shape, lambda b, h, i: (0, 0)),
                  pl.BlockSpec((1, LANES), lambda b, h, i: (0, 0))],
        out_specs=[pl.BlockSpec((1, tq, LANES), lambda b, h, i: (b, i, h)),
                   pl.BlockSpec((1, T, LANES), lambda b, h, i: (b, 0, h)),
                   pl.BlockSpec((1, T, LANES), lambda b, h, i: (b, 0, h))],
        out_shape=[jax.ShapeDtypeStruct((B, T, n_heads * LANES), BF16),
                   jax.ShapeDtypeStruct((B, T, n_heads * LANES), F32),
                   jax.ShapeDtypeStruct((B, T, n_heads * LANES), F32)],
        scratch_shapes=[pltpu.VMEM((T, LANES), BF16), pltpu.VMEM((LANES, T), BF16), pltpu.VMEM((LANES, 2 * tq), BF16),
                        pltpu.VMEM((1, 2 * tq), F32), pltpu.VMEM((1, 2 * tq), F32), pltpu.VMEM((LANES, 2 * tq), F32)],
        compiler_params=_params(("parallel", "parallel", "arbitrary")),
        name="attn_prompt",
    )(proj, proj, proj, proj, lam_params, subw)


def _attn_sample_kernel(pt_ref, q_ref, kn_ref, vn_ref, ag_ref, *refs, n_heads, steps, pages, lam_init):
    del pt_ref
    k_refs, v_refs = refs[:pages], refs[pages:2 * pages]
    lam_ref, subw_ref, o_ref, m_s, l_s, acc_s = refs[2 * pages:]
    gi = pl.program_id(1)
    n_groups = pl.num_programs(1)
    rows_h = 2 * steps
    rows = n_heads * rows_h
    kdim, page = k_refs[0].shape[1], k_refs[0].shape[2]
    assert steps & (steps - 1) == 0

    @pl.when(gi == 0)
    def _():
        m_s[...] = jnp.full_like(m_s, NEG_INF)
        l_s[...] = jnp.zeros_like(l_s)
        acc_s[...] = jnp.zeros_like(acc_s)

    q_all = q_ref[0] * ATTN_SCALE
    hs = lambda h: slice(h * LANES, (h + 1) * LANES)
    q_rep = jnp.concatenate([q_all] * (rows // steps), axis=0)
    col_grp = lax.shift_right_logical(_iota2((rows, kdim), 1), int(math.log2(HEAD_DIM)))
    row_grp = lax.shift_right_logical(_iota2((rows, kdim), 0), int(math.log2(steps)))
    q_bd = jnp.where(col_grp == row_grp, q_rep, 0.0).astype(BF16)

    s = jnp.concatenate([_dot(q_bd, k_refs[i][0].astype(BF16)) for i in range(pages)], axis=1)
    m_old = m_s[...]
    m_run = jnp.maximum(m_old, jnp.max(s, axis=1, keepdims=True))
    alpha = jnp.exp(m_old - m_run)
    p = jnp.exp(s - m_run)
    l_run = alpha * l_s[...] + jnp.sum(p, axis=1, keepdims=True)
    pv = []
    for h in range(n_heads):
        v_h = jnp.concatenate([v_refs[i][0, pl.ds(h, page, stride=n_heads), :].astype(BF16) for i in range(pages)],
                              axis=0)
        pv.append(_dot(p[h * rows_h:(h + 1) * rows_h].astype(BF16), v_h))
    acc = alpha * acc_s[...] + jnp.concatenate(pv, axis=0)
    m_s[...] = m_run
    l_s[...] = l_run
    acc_s[...] = acc

    @pl.when(gi == n_groups - 1)
    def _():
        lam = _diff_lambda(lam_ref, lam_init)
        tok = _iota2((rows_h, LANES), 0) & (steps - 1)
        lane = _iota2((rows_h, LANES), 1)
        for h in range(n_heads):
            rs = slice(h * rows_h, (h + 1) * rows_h)
            q2 = _stack_maps(q_all[:, hs(h)])
            kn = kn_ref[0][:, hs(h)]
            vn = vn_ref[0][:, hs(h)]
            s = jnp.full((rows_h, LANES), NEG_INF, F32)
            for j in range(steps):
                sj = jnp.sum(q2 * kn[j:j + 1, :], axis=1, keepdims=True)
                s = jnp.where((lane == j) & (tok >= j), sj, s)
            m_old = m_run[rs]
            m_new = jnp.maximum(m_old, jnp.max(s, axis=1, keepdims=True))
            alpha = jnp.exp(m_old - m_new)
            p = jnp.exp(s - m_new)
            l_fin = alpha * l_run[rs] + jnp.sum(p, axis=1, keepdims=True)
            a_fin = alpha * acc[rs]
            for j in range(steps):
                a_fin = a_fin + p[:, j:j + 1] * vn[j:j + 1, :]
            o = a_fin / l_fin
            o_ref[0, :, hs(h)] = _diff_finish(o[:steps], o[steps:], lam, subw_ref[...], ag_ref[0][:, hs(h)], lam_init)


def _attn_sample(qkvg, cache_k, cache_v, page_table, lam_params, subw, *, n_heads, pages, lam_init):
    db, steps, _ = qkvg.shape
    n_pages = page_table.shape[0] // db
    n_groups = n_pages // pages
    width = n_heads * LANES
    rows = n_heads * 2 * steps
    tokblk = lambda off: pl.BlockSpec((1, steps, width), lambda b, g, pt: (b, 0, off))
    pageblk = lambda cache, i: pl.BlockSpec((1,) + cache.shape[1:],
                                            lambda b, g, pt: (pt[(b * n_groups + g) * pages + i], 0, 0))
    grid_spec = pltpu.PrefetchScalarGridSpec(
        num_scalar_prefetch=1,
        grid=(db, n_groups),
        in_specs=([tokblk(0), tokblk(1), tokblk(2), tokblk(3)]
                  + [pageblk(cache_k, i) for i in range(pages)] + [pageblk(cache_v, i) for i in range(pages)]
                  + [pl.BlockSpec(lam_params.shape, lambda b, g, pt: (0, 0)),
                     pl.BlockSpec((1, LANES), lambda b, g, pt: (0, 0))]),
        out_specs=pl.BlockSpec((1, steps, width), lambda b, g, pt: (b, 0, 0)),
        scratch_shapes=[pltpu.VMEM((rows, 1), F32), pltpu.VMEM((rows, 1), F32), pltpu.VMEM((rows, LANES), F32)])
    return pl.pallas_call(
        functools.partial(_attn_sample_kernel, n_heads=n_heads, steps=steps, pages=pages, lam_init=lam_init),
        grid_spec=grid_spec,
        out_shape=jax.ShapeDtypeStruct((db, steps, width), F32),
        compiler_params=_params(("parallel", "arbitrary")),
        name="attn_sample",
    )(page_table, qkvg, qkvg, qkvg, qkvg, *([cache_k] * pages), *([cache_v] * pages), lam_params, subw)


def _outproj_kernel(x_ref, rw_ref, at_ref, grw_ref, gat_ref, wbr_ref, wba_ref, wo_ref, nf_ref, o_ref):
    merged = (_sigmoid(grw_ref[...]) * _dot(rw_ref[...], wbr_ref[...])
              + _sigmoid(gat_ref[...]) * _dot(at_ref[...], wba_ref[...]))
    y = x_ref[...] + _dot(merged.astype(BF16), wo_ref[...])
    o_ref[...] = y * lax.rsqrt(jnp.mean(y * y, axis=-1, keepdims=True) + NORM_EPS) * nf_ref[...]


def _outproj(x, rw, at, proj, wbr, wba, wo, nf, *, tm):
    n, d = x.shape
    row = lambda w: pl.BlockSpec((tm, w), lambda i: (i, 0))
    const = lambda a: pl.BlockSpec(a.shape, lambda i: (0, 0), pipeline_mode=pl.Buffered(1))
    return pl.pallas_call(
        _outproj_kernel,
        grid=(n // tm,),
        in_specs=[row(d), row(rw.shape[1]), row(at.shape[1]), row(d),
                  pl.BlockSpec((tm, d), lambda i: (i, 1)),
                  const(wbr), const(wba), const(wo), const(nf)],
        out_specs=row(d),
        out_shape=jax.ShapeDtypeStruct((n, d), F32),
        compiler_params=_params(("parallel",)),
        name="outproj",
    )(x, rw, at, proj, proj, wbr, wba, wo, nf)


def _largest_tile(n, cap):
    t = cap
    while n % t:
        t //= 2
    return t


def kernel(x_prompt, x_sample, cache_k, cache_v, state_wkv, state_shift, page_table, norm_in, w_in, mu_shift, w0, w2, a0, a2, k_k, k_a, r_k, lnx_w, lnx_b, lambda_q1, lambda_k1, lambda_q2, lambda_k2, subln_w, w_br_rwkv, w_br_attn, w_out, norm_f):
    depth = w_in.shape[0]
    assert depth == 1, "single-layer step"
    B, T, D = x_prompt.shape
    DB, TS, _ = x_sample.shape
    r_heads = r_k.shape[1]
    r_width = r_heads * HEAD_DIM
    lora = w2.shape[1]
    a_heads = cache_k.shape[3]
    a_width = a_heads * LANES
    rw_in = 4 * r_width + 2 * lora
    lam_init = 0.8 - 0.6 * math.exp(-0.3 * 0)

    segments = ((rw_in + 4 * a_width, 2 * D, 0), (0, 4 * r_width, 0), (rw_in, 4 * a_width, 0),
                (4 * r_width, lora, LANES - lora), (4 * r_width + lora, lora, LANES - lora))
    w_pad = _regroup_weights(jnp.transpose(w_in[0]), segments, tc=_largest_tile(D, 128))
    rw0 = 2 * D // LANES
    at0 = rw0 + 4 * r_width // LANES
    lora0 = at0 + 4 * a_width // LANES
    pw = w_pad.shape[0]

    mu_pad, hparams, w2p, a2p = _rwkv_params(mu_shift[0], w0[0], w2[0], a0[0], a2[0], k_k[0], k_a[0], r_k[0],
                                             lnx_w[0], lnx_b[0])
    lam_params = jnp.stack([lambda_q1[0], lambda_k1[0], lambda_q2[0], lambda_k2[0]]).astype(F32)
    subw = subln_w[0][None, :].astype(F32)
    g_in = norm_in[0][None, :].astype(F32)
    wbr = w_br_rwkv[0].astype(BF16)
    wba = w_br_attn[0].astype(BF16)
    wo = w_out[0].astype(BF16)
    nf = norm_f[None, :].astype(F32)
    tn = 14 * LANES
    assert pw % tn == 0

    n_p = B * T
    xp = x_prompt.reshape(n_p, D)
    proj_p = _inproj(xp, g_in, w_pad, tm=_largest_tile(n_p, 1024), tn=tn)
    proj_p3 = proj_p.reshape(B, T, pw)
    rw_p, wkv_p = _rwkv_prompt(proj_p3, mu_pad, hparams, w2p, a2p, n_heads=r_heads, rw0=rw0, lora0=lora0,
                               tb=_largest_tile(T, 512), ns=_largest_tile(r_heads // 2, 8))
    at_p, k_p, v_p = _attn_prompt(proj_p3, lam_params, subw, n_heads=a_heads, at0=at0, tq=_largest_tile(T, 2048),
                                  lam_init=lam_init)
    y_p = _outproj(xp, rw_p.reshape(n_p, r_width), at_p.reshape(n_p, a_width), proj_p, wbr, wba, wo, nf,
                   tm=_largest_tile(n_p, 256))

    n_s = DB * TS
    xs = jnp.transpose(x_sample, (1, 0, 2)).reshape(n_s, D)
    proj_s = _inproj(xs, g_in, w_pad, tm=_largest_tile(n_s, 512), tn=tn)
    c0 = rw0 * LANES
    zs = jnp.zeros((DB, LANES - lora), F32)
    sh = state_shift[0]
    shift_pad = jnp.concatenate([sh[:, :4 * r_width], sh[:, 4 * r_width:4 * r_width + lora], zs,
                                 sh[:, 4 * r_width + lora:], zs], axis=-1)
    rwkv_cols = jnp.concatenate([proj_s[:, c0:c0 + 4 * r_width], proj_s[:, lora0 * LANES:]], axis=-1)
    prev = jnp.concatenate([shift_pad, rwkv_cols[:-DB]], axis=0)
    prepped = _rwkv_sample_prep(proj_s, prev, mu_pad, hparams, w2p, a2p, n_heads=r_heads, rw0=rw0, lora0=lora0,
                                steps=TS)
    rw_s, wkv_s = _rwkv_sample(prepped, jnp.transpose(state_wkv[0], (1, 2, 3, 0)), hparams.T, n_heads=r_heads)
    wkv_s = jnp.transpose(wkv_s, (3, 0, 1, 2))
    n_pool, page = cache_k.shape[1], cache_k.shape[2]
    n_pages = page_table.shape[1]
    q0 = at0 * LANES
    qkvg_s = jnp.transpose(proj_s[:, q0:q0 + 4 * a_width].reshape(TS, DB, 4 * a_width), (1, 0, 2))
    kc = jnp.transpose(cache_k[0], (0, 2, 3, 4, 1)).reshape(n_pool, a_width, page)
    at_s = _attn_sample(qkvg_s, kc, cache_v[0].reshape(n_pool, page * a_heads, LANES), page_table.reshape(-1),
                        lam_params, subw, n_heads=a_heads, pages=_largest_tile(n_pages, 16), lam_init=lam_init)
    at_s = jnp.transpose(at_s, (1, 0, 2)).reshape(n_s, a_width)
    y_s = _outproj(xs, rw_s.astype(BF16), at_s.astype(BF16), proj_s, wbr, wba, wo, nf, tm=_largest_tile(n_s, 256))
    y_s = jnp.transpose(y_s.reshape(TS, DB, D), (1, 0, 2))

    def shift_row(last):
        return jnp.concatenate([last[:, c0:c0 + 4 * r_width], last[:, lora0 * LANES:lora0 * LANES + lora],
                                last[:, (lora0 + 1) * LANES:(lora0 + 1) * LANES + lora]], axis=-1)[None]

    return (y_p.reshape(B, T, D), y_s,
            k_p.reshape(1, B, T, a_heads, 2, HEAD_DIM), v_p.reshape(1, B, T, a_heads, LANES),
            qkvg_s[:, :, a_width:2 * a_width].reshape(1, DB, TS, a_heads, 2, HEAD_DIM),
            qkvg_s[:, :, 2 * a_width:3 * a_width].reshape(1, DB, TS, a_heads, LANES),
            wkv_p[None], wkv_s[None], shift_row(proj_p3[:, -1]), shift_row(proj_s[-DB:]))
```

```python
import functools
import math

import jax
import jax.numpy as jnp
from jax import lax
from jax.experimental import pallas as pl
from jax.experimental.pallas import tpu as pltpu

F32 = jnp.float32
BF16 = jnp.bfloat16

LANES = 128
HEAD_DIM = 64
CHUNK = 64
NORM_EPS = 1e-6
RWKV_GN_EPS = 64e-5
NEG_INF = -1e30
ATTN_SCALE = HEAD_DIM ** -0.5
VMEM_LIMIT = 56 * 1024 * 1024

HI = lax.Precision.HIGHEST


def _dot(a, b, prec=None):
    return lax.dot_general(a, b, (((1,), (0,)), ((), ())), precision=prec, preferred_element_type=F32)


def _dot_nt(a, b, prec=None):
    return lax.dot_general(a, b, (((1,), (1,)), ((), ())), precision=prec, preferred_element_type=F32)


def _dot_tn(a, b, prec=None):
    return lax.dot_general(a, b, (((0,), (0,)), ((), ())), precision=prec, preferred_element_type=F32)


def _iota2(shape, dim):
    return lax.broadcasted_iota(jnp.int32, shape, dim)


def _sigmoid(x):
    return 1.0 / (1.0 + jnp.exp(-x))


def _silu(x):
    return x * _sigmoid(x)


def _params(semantics):
    return pltpu.CompilerParams(dimension_semantics=semantics, vmem_limit_bytes=VMEM_LIMIT)


def _regroup_kernel(w_ref, o_ref, *, segments):
    row = 0
    for start, width, pad in segments:
        o_ref[row:row + width, :] = w_ref[start:start + width, :].astype(o_ref.dtype)
        row += width
        if pad:
            o_ref[row:row + pad, :] = jnp.zeros((pad, o_ref.shape[1]), o_ref.dtype)
            row += pad


def _regroup_weights(w_t, segments, *, tc):
    cols, d = w_t.shape
    out_cols = sum(width + pad for _, width, pad in segments)
    return pl.pallas_call(
        functools.partial(_regroup_kernel, segments=segments),
        grid=(d // tc,),
        in_specs=[pl.BlockSpec((cols, tc), lambda i: (0, i))],
        out_specs=pl.BlockSpec((out_cols, tc), lambda i: (0, i)),
        out_shape=jax.ShapeDtypeStruct((out_cols, d), BF16),
        compiler_params=_params(("parallel",)),
        name="regroup_weights",
    )(w_t)


def _inproj_kernel(x_ref, g_ref, w_ref, o_ref, xn_ref):
    @pl.when(pl.program_id(1) == 0)
    def _():
        x = x_ref[...]
        xn = x * lax.rsqrt(jnp.mean(x * x, axis=-1, keepdims=True) + NORM_EPS) * g_ref[...]
        xn_ref[...] = xn.astype(BF16)

    o_ref[...] = _dot_nt(xn_ref[...], w_ref[...])


def _inproj(x, g, w_t, *, tm, tn):
    n, d = x.shape
    pw = w_t.shape[0]
    return pl.pallas_call(
        _inproj_kernel,
        grid=(n // tm, pw // tn),
        in_specs=[pl.BlockSpec((tm, d), lambda i, j: (i, 0)),
                  pl.BlockSpec((1, d), lambda i, j: (0, 0)),
                  pl.BlockSpec((tn, d), lambda i, j: (j, 0))],
        out_specs=pl.BlockSpec((tm, tn), lambda i, j: (i, j)),
        out_shape=jax.ShapeDtypeStruct((n, pw), F32),
        scratch_shapes=[pltpu.VMEM((tm, d), BF16)],
        compiler_params=_params(("parallel", "arbitrary")),
        name="inproj",
    )(x, g, w_t)


def _rwkv_prepare(ps_r, ps_k, ps_v, ps_g, ps_wd, ps_ad, hp, w2, a2, gsum):
    w0, a0, k_k, k_a = hp[0:1], hp[1:2], hp[2:3], hp[3:4]
    w_pre = w0 + _dot(jnp.tanh(ps_wd).astype(BF16), w2)
    log_decay = -math.exp(-0.5) * _sigmoid(w_pre)
    a = _sigmoid(a0 + _dot(ps_ad.astype(BF16), a2))
    kk = ps_k * k_k
    kk = kk * lax.rsqrt(jnp.maximum(gsum(kk * kk), 1e-24))
    k = ps_k * (1.0 + (a - 1.0) * k_a)
    return ps_r, k, ps_v, ps_g, kk, kk * a, log_decay


def _rwkv_finish(y, r, k, v, g, hp, gsum):
    r_k, lnx_w, lnx_b = hp[4:5], hp[5:6], hp[6:7]
    mean = gsum(y) * (1.0 / HEAD_DIM)
    d = y - mean
    var = gsum(d * d) * (1.0 / HEAD_DIM)
    yn = d * lax.rsqrt(var + RWKV_GN_EPS) * lnx_w + lnx_b
    bonus = gsum(r * k * r_k) * v
    return (yn + bonus) * _silu(g)


def _head_masks(shape):
    lane = _iota2(shape, 1)
    m0 = (lane < HEAD_DIM).astype(F32)
    return m0, 1.0 - m0


def _group_ones():
    i = _iota2((LANES, LANES), 0) // HEAD_DIM
    j = _iota2((LANES, LANES), 1) // HEAD_DIM
    return (i == j).astype(BF16)


def _gsum_wide(x):
    ones = _group_ones()
    hi = x.astype(BF16)
    lo = (x - hi.astype(F32)).astype(BF16)
    parts = []
    for j in range(x.shape[1] // LANES):
        ls = slice(j * LANES, (j + 1) * LANES)
        parts.append(_dot(hi[:, ls], ones) + _dot(lo[:, ls], ones))
    return parts[0] if len(parts) == 1 else jnp.concatenate(parts, axis=1)


def _rwkv_prompt_kernel(pr_ref, pk_ref, pv_ref, pg_ref, pwd_ref, pad_ref,
                        mr_ref, mk_ref, mv_ref, mg_ref, mwd_ref, mad_ref,
                        hp_ref, w2_ref, a2_ref,
                        out_ref, state_ref,
                        carry_ref, cw_ref, m_ref, r_s, k_s, v_s, g_s, kk_s, b_s, ld_s, *, tb, ns):
    t_idx = pl.program_id(2)
    n_t = pl.num_programs(2)
    C = CHUNK

    @pl.when(t_idx == 0)
    def _():
        carry_ref[...] = jnp.zeros_like(carry_ref)
        cw_ref[...] = jnp.zeros_like(cw_ref)
        m_ref[...] = jnp.zeros_like(m_ref)

    def shifted(p_ref, mu_ref, c_ref, idx):
        p = p_ref[0]
        row = _iota2(p.shape, 0)
        prev = pltpu.roll(p, 1, axis=0)
        prev = jnp.where(row == 0, c_ref[idx], prev)
        c_ref[idx] = p[tb - 1:tb, :]
        return p + (prev - p) * mu_ref[...]

    hp = hp_ref[...]
    r, k, v, g, kk, b, ld = _rwkv_prepare(
        shifted(pr_ref, mr_ref, carry_ref, 0), shifted(pk_ref, mk_ref, carry_ref, 1),
        shifted(pv_ref, mv_ref, carry_ref, 2), shifted(pg_ref, mg_ref, carry_ref, 3),
        shifted(pwd_ref, mwd_ref, cw_ref, 0), shifted(pad_ref, mad_ref, cw_ref, 1),
        hp, w2_ref[...], a2_ref[...], _gsum_wide)
    r_s[...] = r
    k_s[...] = k
    v_s[...] = v
    g_s[...] = g
    kk_s[...] = kk
    b_s[...] = b
    ld_s[...] = ld

    m0, m1 = _head_masks((C, LANES))
    ii = _iota2((2 * C, 2 * C), 0)
    jj = _iota2((2 * C, 2 * C), 1)
    strict = ii > jj
    incl = ii >= jj
    eye = ii == jj
    row_c = _iota2((C, LANES), 0)
    first = m0 > 0.5

    def cumsum_rows(x):
        shift = 1
        while shift < C:
            x = x + jnp.where(row_c >= shift, pltpu.roll(x, shift, axis=0), 0.0)
            shift *= 2
        return x

    def head_sums(x):
        s0 = jnp.sum(x * m0, axis=1, keepdims=True)
        s1 = jnp.sum(x * m1, axis=1, keepdims=True)
        return jnp.where(first, s0, s1)

    stack2 = lambda x: jnp.concatenate([x * m0, x * m1], axis=0).astype(BF16)
    bf = lambda x: x.astype(BF16)

    lanes = [slice(s * LANES, (s + 1) * LANES) for s in range(ns)]
    each = lambda f, *cols: [f(*args) for args in zip(*cols)]

    def chunk(ci, ms):
        sl = pl.ds(pl.multiple_of(ci * C, C), C)
        ld = [ld_s[sl, ls] for ls in lanes]
        c = each(cumsum_rows, ld)
        c_end = each(lambda x: x[C - 1:C, :], c)
        p_inv = each(lambda x: jnp.exp(-x), c)
        p_end = each(lambda e, x: jnp.exp(e - x), c_end, c)
        r2 = each(lambda ls, x: stack2(r_s[sl, ls] * jnp.exp(x)), lanes, c)
        kk2 = each(lambda ls, x, d: stack2(kk_s[sl, ls] * jnp.exp(x - d)), lanes, c, ld)
        kh2 = each(lambda ls, p: stack2(k_s[sl, ls] * p), lanes, p_inv)
        bh2 = each(lambda ls, p: stack2(b_s[sl, ls] * p), lanes, p_inv)
        ke2 = each(lambda ls, p: stack2(k_s[sl, ls] * p), lanes, p_end)
        be2 = each(lambda ls, p: stack2(b_s[sl, ls] * p), lanes, p_end)
        v2 = each(lambda ls: stack2(v_s[sl, ls]), lanes)
        a = each(lambda w, x, y, z: _dot_nt(jnp.concatenate([w, x], axis=0), jnp.concatenate([y, z], axis=0)),
                 kk2, r2, kh2, bh2)
        l_k = each(lambda x: bf(jnp.where(strict, x[:2 * C, :2 * C], 0.0)), a)
        l_b = each(lambda x: bf(jnp.where(strict, x[:2 * C, 2 * C:], 0.0)), a)
        a_rk = each(lambda x: bf(jnp.where(incl, x[2 * C:, :2 * C], 0.0)), a)
        a_rb = each(lambda x: bf(jnp.where(incl, -x[2 * C:, 2 * C:], 0.0)), a)
        mb = each(bf, ms)
        x = each(lambda w, lk, m, v: _dot(jnp.concatenate([w, lk], axis=1), jnp.concatenate([m, v], axis=0)),
                 kk2, l_k, mb, v2)
        pw = l_b
        u = each(lambda x_, p: x_ - _dot(p, bf(x_)), x, pw)
        for _ in range(5):
            pw = each(lambda p: bf(_dot(p, p)), pw)
            u = each(lambda u_, p: u_ + _dot(p, bf(u_)), u, pw)
        ub = each(bf, u)
        y2 = each(lambda r_, ak, ab, m, v, u_: _dot(jnp.concatenate([r_, ak, ab], axis=1),
                                                     jnp.concatenate([m, v, u_], axis=0)),
                  r2, a_rk, a_rb, mb, v2, ub)
        p_c_col = each(lambda e: jnp.sum(jnp.where(eye, jnp.broadcast_to(jnp.exp(e), (2 * C, LANES)), 0.0),
                                         axis=1, keepdims=True), c_end)
        m_new = each(lambda m, pc, ke, be, v, u_: m * pc + _dot_tn(jnp.concatenate([ke, -be], axis=0),
                                                                   jnp.concatenate([v, u_], axis=0)),
                     ms, p_c_col, ke2, be2, v2, ub)
        for ls, y in zip(lanes, y2):
            out = _rwkv_finish(y[:C] + y[C:], r_s[sl, ls], k_s[sl, ls], v_s[sl, ls], g_s[sl, ls], hp[:, ls],
                               head_sums)
            out_ref[0, sl, ls] = out.astype(out_ref.dtype)
        return tuple(m_new)

    ms = lax.fori_loop(0, tb // C, chunk, tuple(m_ref[s] for s in range(ns)), unroll=2)
    for s in range(ns):
        m_ref[s] = ms[s]

    @pl.when(t_idx == n_t - 1)
    def _():
        for s in range(ns):
            mt = ms[s].T
            state_ref[0, 2 * s] = mt[:HEAD_DIM, :HEAD_DIM]
            state_ref[0, 2 * s + 1] = mt[HEAD_DIM:, HEAD_DIM:]


def _rwkv_prompt(proj, mu_pad, hparams, w2p, a2p, *, n_heads, rw0, lora0, tb, ns):
    B, T, _ = proj.shape
    n_hp = n_heads // 2
    assert rw0 % ns == 0 and n_hp % ns == 0
    n_grp = n_hp // ns
    wide = ns * LANES
    col = lambda off: (lambda b, h, t: (b, t, off // ns + h))
    colc = lambda off: (lambda b, h, t: (b, t, off))
    mcol = lambda off: (lambda b, h, t: (0, off // ns + h))
    mcolc = lambda off: (lambda b, h, t: (0, off))
    pblk = lambda im: pl.BlockSpec((1, tb, wide), im)
    mblk = lambda im: pl.BlockSpec((1, wide), im)
    in_specs = [pblk(col(rw0)), pblk(col(rw0 + n_hp)), pblk(col(rw0 + 2 * n_hp)), pblk(col(rw0 + 3 * n_hp)),
                pl.BlockSpec((1, tb, LANES), colc(lora0)), pl.BlockSpec((1, tb, LANES), colc(lora0 + 1)),
                mblk(mcol(0)), mblk(mcol(n_hp)), mblk(mcol(2 * n_hp)), mblk(mcol(3 * n_hp)),
                pl.BlockSpec((1, LANES), mcolc(4 * n_hp)), pl.BlockSpec((1, LANES), mcolc(4 * n_hp + 1)),
                pl.BlockSpec((8, wide), lambda b, h, t: (0, h)),
                pl.BlockSpec((LANES, wide), lambda b, h, t: (0, h)),
                pl.BlockSpec((LANES, wide), lambda b, h, t: (0, h))]
    out_specs = [pl.BlockSpec((1, tb, wide), lambda b, h, t: (b, t, h)),
                 pl.BlockSpec((1, 2 * ns, HEAD_DIM, HEAD_DIM), lambda b, h, t: (b, h, 0, 0))]
    scratch = ([pltpu.VMEM((4, 1, wide), F32), pltpu.VMEM((2, 1, LANES), F32), pltpu.VMEM((ns, LANES, LANES), F32)]
               + [pltpu.VMEM((tb, wide), F32)] * 7)
    return pl.pallas_call(
        functools.partial(_rwkv_prompt_kernel, tb=tb, ns=ns),
        grid=(B, n_grp, T // tb),
        in_specs=in_specs, out_specs=out_specs,
        out_shape=[jax.ShapeDtypeStruct((B, T, n_heads * HEAD_DIM), BF16),
                   jax.ShapeDtypeStruct((B, n_heads, HEAD_DIM, HEAD_DIM), F32)],
        scratch_shapes=scratch,
        compiler_params=_params(("parallel", "parallel", "arbitrary")),
        name="rwkv_prompt",
    )(proj, proj, proj, proj, proj, proj, mu_pad, mu_pad, mu_pad, mu_pad, mu_pad, mu_pad, hparams, w2p, a2p)


def _rwkv_params(mu, w0, w2, a0, a2, k_k, k_a, r_k, lnx_w, lnx_b):
    width = w0.shape[0]
    lora = w2.shape[0]
    zpad = jnp.zeros((LANES - lora,), F32)
    mu_pad = jnp.concatenate([mu[:4 * width], mu[4 * width:4 * width + lora], zpad, mu[4 * width + lora:], zpad])[None, :]
    hparams = jnp.stack([w0, a0, k_k, k_a, r_k.reshape(-1), lnx_w, lnx_b, jnp.zeros_like(w0)]).astype(F32)
    pad_rows = lambda m: jnp.concatenate([m, jnp.zeros((LANES - lora, width), m.dtype)], axis=0).astype(BF16)
    return mu_pad.astype(F32), hparams, pad_rows(w2), pad_rows(a2)


def _rwkv_sample_prep_kernel(pr_ref, pk_ref, pv_ref, pg_ref, pwd_ref, pad_ref,
                             qr_ref, qk_ref, qv_ref, qg_ref, qwd_ref, qad_ref,
                             mr_ref, mk_ref, mv_ref, mg_ref, mwd_ref, mad_ref,
                             hp_ref, w2_ref, a2_ref,
                             r_out, k_out, v_out, g_out, kk_out, b_out, ld_out, *, steps):
    def shifted(p_ref, q_ref, mu_ref):
        p = p_ref[...]
        return p + (q_ref[...] - p) * mu_ref[...]

    outs = _rwkv_prepare(
        shifted(pr_ref, qr_ref, mr_ref), shifted(pk_ref, qk_ref, mk_ref), shifted(pv_ref, qv_ref, mv_ref),
        shifted(pg_ref, qg_ref, mg_ref), shifted(pwd_ref, qwd_ref, mwd_ref), shifted(pad_ref, qad_ref, mad_ref),
        hp_ref[...], w2_ref[...], a2_ref[...], _gsum_wide)
    db = r_out.shape[2]
    for o_ref, val in zip((r_out, k_out, v_out, g_out, kk_out, b_out, ld_out), outs):
        for t in range(steps):
            o_ref[t] = val[t * db:(t + 1) * db, :].T


def _rwkv_sample_prep(proj, prev, mu_pad, hparams, w2p, a2p, *, n_heads, rw0, lora0, steps):
    n = proj.shape[0]
    db = n // steps
    n_hp = n_heads // 2
    blk = lambda off, fixed=False: pl.BlockSpec((n, LANES), (lambda h: (0, off)) if fixed else (lambda h: (0, off + h)))
    mblk = lambda off, fixed=False: pl.BlockSpec((1, LANES), (lambda h: (0, off)) if fixed else (lambda h: (0, off + h)))
    offs = [0, n_hp, 2 * n_hp, 3 * n_hp]
    in_specs = ([blk(rw0 + o) for o in offs] + [blk(lora0, True), blk(lora0 + 1, True)]
                + [blk(o) for o in offs] + [blk(4 * n_hp, True), blk(4 * n_hp + 1, True)]
                + [mblk(o) for o in offs] + [mblk(4 * n_hp, True), mblk(4 * n_hp + 1, True)]
                + [pl.BlockSpec((8, LANES), lambda h: (0, h)),
                   pl.BlockSpec((LANES, LANES), lambda h: (0, h)),
                   pl.BlockSpec((LANES, LANES), lambda h: (0, h))])
    width = n_heads * HEAD_DIM
    return pl.pallas_call(
        functools.partial(_rwkv_sample_prep_kernel, steps=steps),
        grid=(n_hp,),
        in_specs=in_specs,
        out_specs=[pl.BlockSpec((steps, LANES, db), lambda h: (0, h, 0))] * 7,
        out_shape=[jax.ShapeDtypeStruct((steps, width, db), F32)] * 7,
        compiler_params=_params(("parallel",)),
        name="rwkv_sample_prep",
    )(*([proj] * 6 + [prev] * 6 + [mu_pad] * 6 + [hparams, w2p, a2p]))


def _rwkv_sample_kernel(r_ref, k_ref, v_ref, g_ref, kk_ref, b_ref, ld_ref, s_ref, hpt_ref, out_ref, so_ref, y_s, *,
                        steps):
    db = out_ref.shape[0] // steps

    for hh in range(2):
        r0 = hh * HEAD_DIM
        rows = slice(r0, r0 + HEAD_DIM)
        kk = [kk_ref[t, rows, :] for t in range(steps)]
        bb = [b_ref[t, rows, :] for t in range(steps)]
        kv = [k_ref[t, rows, :] for t in range(steps)]
        rv = [r_ref[t, rows, :] for t in range(steps)]
        decay = [jnp.exp(ld_ref[t, rows, :]) for t in range(steps)]

        def row_block(v, carry):
            s_v = s_ref[hh, v]
            for t in range(steps):
                s_kk = jnp.sum(s_v * kk[t], axis=0, keepdims=True)
                s_v = s_v * decay[t] - s_kk * bb[t] + v_ref[t, pl.ds(r0 + v, 1), :] * kv[t]
                y_s[t, pl.ds(r0 + v, 1), :] = jnp.sum(s_v * rv[t], axis=0, keepdims=True)
            so_ref[hh, v] = s_v
            return carry

        lax.fori_loop(0, HEAD_DIM, row_block, 0, unroll=2)

    hpt = hpt_ref[...]
    r_k, lnx_w, lnx_b = hpt[:, 4:5], hpt[:, 5:6], hpt[:, 6:7]
    for t in range(steps):
        outs = []
        for hh in range(2):
            rows = slice(hh * HEAD_DIM, (hh + 1) * HEAD_DIM)
            y = y_s[t, rows, :]
            mean = jnp.mean(y, axis=0, keepdims=True)
            d = y - mean
            var = jnp.mean(d * d, axis=0, keepdims=True)
            yn = d * lax.rsqrt(var + RWKV_GN_EPS) * lnx_w[rows] + lnx_b[rows]
            bonus = jnp.sum(r_ref[t, rows, :] * k_ref[t, rows, :] * r_k[rows], axis=0, keepdims=True) * v_ref[t, rows, :]
            outs.append((yn + bonus) * _silu(g_ref[t, rows, :]))
        out_ref[t * db:(t + 1) * db, :] = jnp.concatenate(outs, axis=0).T


def _rwkv_sample(prepped, state, hparams_t, *, n_heads):
    steps, width, db = prepped[0].shape
    tok = pl.BlockSpec((steps, LANES, db), lambda j: (0, j, 0))
    st = pl.BlockSpec((2, HEAD_DIM, HEAD_DIM, db), lambda j: (j, 0, 0, 0))
    return pl.pallas_call(
        functools.partial(_rwkv_sample_kernel, steps=steps),
        grid=(n_heads // 2,),
        in_specs=[tok] * 7 + [st, pl.BlockSpec((LANES, 8), lambda j: (j, 0))],
        out_specs=[pl.BlockSpec((steps * db, LANES), lambda j: (0, j)), st],
        out_shape=[jax.ShapeDtypeStruct((steps * db, width), F32),
                   jax.ShapeDtypeStruct(state.shape, F32)],
        scratch_shapes=[pltpu.VMEM((steps, LANES, db), F32)],
        compiler_params=_params(("parallel",)),
        name="rwkv_sample",
    )(*prepped, state, hparams_t)


def _diff_lambda(lam_ref, lam_init):
    lp = lam_ref[...]
    s1 = jnp.sum(lp[0:1] * lp[1:2], axis=1, keepdims=True)
    s2 = jnp.sum(lp[2:3] * lp[3:4], axis=1, keepdims=True)
    return jnp.exp(s1) - jnp.exp(s2) + lam_init


def _diff_finish(o1, o2, lam, subw, ag, lam_init):
    o = o1 - lam * o2
    o = o * lax.rsqrt(jnp.mean(o * o, axis=-1, keepdims=True) + NORM_EPS)
    return o * subw * (1.0 - lam_init) * _silu(ag)


def _stack_maps(q):
    m0, m1 = _head_masks(q.shape)
    return jnp.concatenate([q * m0, q * m1], axis=0)


def _attn_prompt_kernel(q_ref, k_ref, v_ref, ag_ref, lam_ref, subw_ref, o_ref, ko_ref, vo_ref,
                        kb_s, vt_s, qt_s, m_s, l_s, acc_s, *, tq, tk, sub, lam_init):
    qi = pl.program_id(2)

    @pl.when(qi == 0)
    def _():
        ko_ref[...] = k_ref[...]
        vo_ref[...] = v_ref[...]
        kb_s[...] = k_ref[0].astype(BF16)
        vt_s[...] = v_ref[0].T.astype(BF16)

    qt_s[...] = _stack_maps(q_ref[0] * (ATTN_SCALE * math.log2(math.e))).T.astype(BF16)
    m_s[...] = jnp.full_like(m_s, NEG_INF)
    l_s[...] = jnp.zeros_like(l_s)
    acc_s[...] = jnp.zeros_like(acc_s)
    n_sub = 2 * tq // sub

    def block(k0, diag_off):
        def width(j):
            if diag_off is None:
                return tk
            return max(0, min(tk, (j * sub) % tq + sub - diag_off))

        def scores(j):
            kw = width(j)
            if kw == 0:
                return None
            s = _dot(kb_s[pl.ds(k0, kw), :], qt_s[:, j * sub:(j + 1) * sub])
            if diag_off is not None:
                q_pos = (_iota2(s.shape, 1) + j * sub) & (tq - 1)
                s = jnp.where(_iota2(s.shape, 0) + diag_off <= q_pos, s, NEG_INF)
            return s

        s_next = scores(0)
        for j in range(n_sub):
            s = s_next
            if j + 1 < n_sub:
                s_next = scores(j + 1)
            if s is None:
                continue
            cols = slice(j * sub, (j + 1) * sub)
            m_old = m_s[:, cols]
            m_new = jnp.maximum(m_old, jnp.max(s, axis=0, keepdims=True))
            alpha = jnp.exp2(m_old - m_new)
            p = jnp.exp2(s - m_new)
            l_s[:, cols] = alpha * l_s[:, cols] + jnp.sum(p, axis=0, keepdims=True)
            acc_s[:, cols] = alpha * acc_s[:, cols] + _dot(vt_s[:, pl.ds(k0, width(j))], p.astype(BF16))
            m_s[:, cols] = m_new

    def body(ki, carry):
        block(pl.multiple_of(ki * tk, tk), None)
        return carry

    lax.fori_loop(0, qi * (tq // tk), body, 0)
    for d in range(tq // tk):
        block(pl.multiple_of(qi * tq + d * tk, tk), d * tk)
    o = acc_s[...] / l_s[...]
    lam = _diff_lambda(lam_ref, lam_init)
    o_ref[0] = _diff_finish(o[:, :tq].T, o[:, tq:].T, lam, subw_ref[...], ag_ref[0], lam_init).astype(o_ref.dtype)


def _attn_prompt(proj, lam_params, subw, *, n_heads, at0, tq, lam_init):
    B, T, _ = proj.shape
    qblk = lambda off: pl.BlockSpec((1, tq, LANES), lambda b, h, i: (b, i, off + h))
    kblk = lambda off: pl.BlockSpec((1, T, LANES), lambda b, h, i: (b, 0, off + h))
    return pl.pallas_call(
        functools.partial(_attn_prompt_kernel, tq=tq, tk=min(512, tq), sub=min(256, 2 * tq), lam_init=lam_init),
        grid=(B, n_heads, T // tq),
        in_specs=[qblk(at0), kblk(at0 + n_heads), kblk(at0 + 2 * n_heads), qblk(at0 + 3 * n_heads),
                  pl.BlockSpec(lam_params.shape, lambda b, h, i: (0, 0)),
                  pl.BlockSpec((1, LANES), lambda b, h, i: (0, 0))],
        out_specs=[pl.BlockSpec((1, tq, LANES), lambda b, h, i: (b, i, h)),
                   pl.BlockSpec((1, T, LANES), lambda b, h, i: (b, 0, h)),
                   pl.BlockSpec((1, T, LANES), lambda b, h, i: (b, 0, h))],
        out_shape=[jax.ShapeDtypeStruct((B, T, n_heads * LANES), BF16),
                   jax.ShapeDtypeStruct((B, T, n_heads * LANES), F32),
                   jax.ShapeDtypeStruct((B, T, n_heads * LANES), F32)],
        scratch_shapes=[pltpu.VMEM((T, LANES), BF16), pltpu.VMEM((LANES, T), BF16), pltpu.VMEM((LANES, 2 * tq), BF16),
                        pltpu.VMEM((1, 2 * tq), F32), pltpu.VMEM((1, 2 * tq), F32), pltpu.VMEM((LANES, 2 * tq), F32)],
        compiler_params=_params(("parallel", "parallel", "arbitrary")),
        name="attn_prompt",
    )(proj, proj, proj, proj, lam_params, subw)


def _attn_sample_kernel(pt_ref, q_ref, kn_ref, vn_ref, ag_ref, *refs, n_heads, steps, pages, lam_init):
    del pt_ref
    k_refs, v_refs = refs[:pages], refs[pages:2 * pages]
    lam_ref, subw_ref, o_ref, m_s, l_s, acc_s = refs[2 * pages:]
    gi = pl.program_id(1)
    n_groups = pl.num_programs(1)
    rows_h = 2 * steps
    rows = n_heads * rows_h
    kdim, page = k_refs[0].shape[1], k_refs[0].shape[2]
    assert steps & (steps - 1) == 0

    @pl.when(gi == 0)
    def _():
        m_s[...] = jnp.full_like(m_s, NEG_INF)
        l_s[...] = jnp.zeros_like(l_s)
        acc_s[...] = jnp.zeros_like(acc_s)

    q_all = q_ref[0] * ATTN_SCALE
    hs = lambda h: slice(h * LANES, (h + 1) * LANES)
    q_rep = jnp.concatenate([q_all] * (rows // steps), axis=0)
    col_grp = lax.shift_right_logical(_iota2((rows, kdim), 1), int(math.log2(HEAD_DIM)))
    row_grp = lax.shift_right_logical(_iota2((rows, kdim), 0), int(math.log2(steps)))
    q_bd = jnp.where(col_grp == row_grp, q_rep, 0.0).astype(BF16)

    s = jnp.concatenate([_dot(q_bd, k_refs[i][0].astype(BF16)) for i in range(pages)], axis=1)
    m_old = m_s[...]
    m_run = jnp.maximum(m_old, jnp.max(s, axis=1, keepdims=True))
    alpha = jnp.exp(m_old - m_run)
    p = jnp.exp(s - m_run)
    l_run = alpha * l_s[...] + jnp.sum(p, axis=1, keepdims=True)
    pv = []
    for h in range(n_heads):
        v_h = jnp.concatenate([v_refs[i][0, pl.ds(h, page, stride=n_heads), :].astype(BF16) for i in range(pages)],
                              axis=0)
        pv.append(_dot(p[h * rows_h:(h + 1) * rows_h].astype(BF16), v_h))
    acc = alpha * acc_s[...] + jnp.concatenate(pv, axis=0)
    m_s[...] = m_run
    l_s[...] = l_run
    acc_s[...] = acc

    @pl.when(gi == n_groups - 1)
    def _():
        lam = _diff_lambda(lam_ref, lam_init)
        tok = _iota2((rows_h, LANES), 0) & (steps - 1)
        lane = _iota2((rows_h, LANES), 1)
        for h in range(n_heads):
            rs = slice(h * rows_h, (h + 1) * rows_h)
            q2 = _stack_maps(q_all[:, hs(h)])
            kn = kn_ref[0][:, hs(h)]
            vn = vn_ref[0][:, hs(h)]
            s = jnp.full((rows_h, LANES), NEG_INF, F32)
            for j in range(steps):
                sj = jnp.sum(q2 * kn[j:j + 1, :], axis=1, keepdims=True)
                s = jnp.where((lane == j) & (tok >= j), sj, s)
            m_old = m_run[rs]
            m_new = jnp.maximum(m_old, jnp.max(s, axis=1, keepdims=True))
            alpha = jnp.exp(m_old - m_new)
            p = jnp.exp(s - m_new)
            l_fin = alpha * l_run[rs] + jnp.sum(p, axis=1, keepdims=True)
            a_fin = alpha * acc[rs]
            for j in range(steps):
                a_fin = a_fin + p[:, j:j + 1] * vn[j:j + 1, :]
            o = a_fin / l_fin
            o_ref[0, :, hs(h)] = _diff_finish(o[:steps], o[steps:], lam, subw_ref[...], ag_ref[0][:, hs(h)], lam_init)


def _attn_sample(qkvg, cache_k, cache_v, page_table, lam_params, subw, *, n_heads, pages, lam_init):
    db, steps, _ = qkvg.shape
    n_pages = page_table.shape[0] // db
    n_groups = n_pages // pages
    width = n_heads * LANES
    rows = n_heads * 2 * steps
    tokblk = lambda off: pl.BlockSpec((1, steps, width), lambda b, g, pt: (b, 0, off))
    pageblk = lambda cache, i: pl.BlockSpec((1,) + cache.shape[1:],
                                            lambda b, g, pt: (pt[(b * n_groups + g) * pages + i], 0, 0))
    grid_spec = pltpu.PrefetchScalarGridSpec(
        num_scalar_prefetch=1,
        grid=(db, n_groups),
        in_specs=([tokblk(0), tokblk(1), tokblk(2), tokblk(3)]
                  + [pageblk(cache_k, i) for i in range(pages)] + [pageblk(cache_v, i) for i in range(pages)]
                  + [pl.BlockSpec(lam_params.shape, lambda b, g, pt: (0, 0)),
                     pl.BlockSpec((1, LANES), lambda b, g, pt: (0, 0))]),
        out_specs=pl.BlockSpec((1, steps, width), lambda b, g, pt: (b, 0, 0)),
        scratch_shapes=[pltpu.VMEM((rows, 1), F32), pltpu.VMEM((rows, 1), F32), pltpu.VMEM((rows, LANES), F32)])
    return pl.pallas_call(
        functools.partial(_attn_sample_kernel, n_heads=n_heads, steps=steps, pages=pages, lam_init=lam_init),
        grid_spec=grid_spec,
        out_shape=jax.ShapeDtypeStruct((db, steps, width), F32),
        compiler_params=_params(("parallel", "arbitrary")),
        name="attn_sample",
    )(page_table, qkvg, qkvg, qkvg, qkvg, *([cache_k] * pages), *([cache_v] * pages), lam_params, subw)


def _outproj_kernel(x_ref, rw_ref, at_ref, grw_ref, gat_ref, wbr_ref, wba_ref, wo_ref, nf_ref, o_ref):
    merged = (_sigmoid(grw_ref[...]) * _dot(rw_ref[...], wbr_ref[...])
              + _sigmoid(gat_ref[...]) * _dot(at_ref[...], wba_ref[...]))
    y = x_ref[...] + _dot(merged.astype(BF16), wo_ref[...])
    o_ref[...] = y * lax.rsqrt(jnp.mean(y * y, axis=-1, keepdims=True) + NORM_EPS) * nf_ref[...]


def _outproj(x, rw, at, proj, wbr, wba, wo, nf, *, tm):
    n, d = x.shape
    row = lambda w: pl.BlockSpec((tm, w), lambda i: (i, 0))
    const = lambda a: pl.BlockSpec(a.shape, lambda i: (0, 0), pipeline_mode=pl.Buffered(1))
    return pl.pallas_call(
        _outproj_kernel,
        grid=(n // tm,),
        in_specs=[row(d), row(rw.shape[1]), row(at.shape[1]), row(d),
                  pl.BlockSpec((tm, d), lambda i: (i, 1)),
                  const(wbr), const(wba), const(wo), const(nf)],
        out_specs=row(d),
        out_shape=jax.ShapeDtypeStruct((n, d), F32),
        compiler_params=_params(("parallel",)),
        name="outproj",
    )(x, rw, at, proj, proj, wbr, wba, wo, nf)


def _largest_tile(n, cap):
    t = cap
    while n % t:
        t //= 2
    return t


def kernel(x_prompt, x_sample, cache_k, cache_v, state_wkv, state_shift, page_table, norm_in, w_in, mu_shift, w0, w2, a0, a2, k_k, k_a, r_k, lnx_w, lnx_b, lambda_q1, lambda_k1, lambda_q2, lambda_k2, subln_w, w_br_rwkv, w_br_attn, w_out, norm_f):
    depth = w_in.shape[0]
    assert depth == 1, "single-layer step"
    B, T, D = x_prompt.shape
    DB, TS, _ = x_sample.shape
    r_heads = r_k.shape[1]
    r_width = r_heads * HEAD_DIM
    lora = w2.shape[1]
    a_heads = cache_k.shape[3]
    a_width = a_heads * LANES
    rw_in = 4 * r_width + 2 * lora
    lam_init = 0.8 - 0.6 * math.exp(-0.3 * 0)

    segments = ((rw_in + 4 * a_width, 2 * D, 0), (0, 4 * r_width, 0), (rw_in, 4 * a_width, 0),
                (4 * r_width, lora, LANES - lora), (4 * r_width + lora, lora, LANES - lora))
    w_pad = _regroup_weights(jnp.transpose(w_in[0]), segments, tc=_largest_tile(D, 128))
    rw0 = 2 * D // LANES
    at0 = rw0 + 4 * r_width // LANES
    lora0 = at0 + 4 * a_width // LANES
    pw = w_pad.shape[0]

    mu_pad, hparams, w2p, a2p = _rwkv_params(mu_shift[0], w0[0], w2[0], a0[0], a2[0], k_k[0], k_a[0], r_k[0],
                                             lnx_w[0], lnx_b[0])
    lam_params = jnp.stack([lambda_q1[0], lambda_k1[0], lambda_q2[0], lambda_k2[0]]).astype(F32)
    subw = subln_w[0][None, :].astype(F32)
    g_in = norm_in[0][None, :].astype(F32)
    wbr = w_br_rwkv[0].astype(BF16)
    wba = w_br_attn[0].astype(BF16)
    wo = w_out[0].astype(BF16)
    nf = norm_f[None, :].astype(F32)
    tn = 14 * LANES
    assert pw % tn == 0

    n_p = B * T
    xp = x_prompt.reshape(n_p, D)
    proj_p = _inproj(xp, g_in, w_pad, tm=_largest_tile(n_p, 1024), tn=tn)
    proj_p3 = proj_p.reshape(B, T, pw)
    rw_p, wkv_p = _rwkv_prompt(proj_p3, mu_pad, hparams, w2p, a2p, n_heads=r_heads, rw0=rw0, lora0=lora0,
                               tb=_largest_tile(T, 512), ns=_largest_tile(r_heads // 2, 8))
    at_p, k_p, v_p = _attn_prompt(proj_p3, lam_params, subw, n_heads=a_heads, at0=at0, tq=_largest_tile(T, 2048),
                                  lam_init=lam_init)
    y_p = _outproj(xp, rw_p.reshape(n_p, r_width), at_p.reshape(n_p, a_width), proj_p, wbr, wba, wo, nf,
                   tm=_largest_tile(n_p, 256))

    n_s = DB * TS
    xs = jnp.transpose(x_sample, (1, 0, 2)).reshape(n_s, D)
    proj_s = _inproj(xs, g_in, w_pad, tm=_largest_tile(n_s, 512), tn=tn)
    c0 = rw0 * LANES
    zs = jnp.zeros((DB, LANES - lora), F32)
    sh = state_shift[0]
    shift_pad = jnp.concatenate([sh[:, :4 * r_width], sh[:, 4 * r_width:4 * r_width + lora], zs,
                                 sh[:, 4 * r_width + lora:], zs], axis=-1)
    rwkv_cols = jnp.concatenate([proj_s[:, c0:c0 + 4 * r_width], proj_s[:, lora0 * LANES:]], axis=-1)
    prev = jnp.concatenate([shift_pad, rwkv_cols[:-DB]], axis=0)
    prepped = _rwkv_sample_prep(proj_s, prev, mu_pad, hparams, w2p, a2p, n_heads=r_heads, rw0=rw0, lora0=lora0,
                                steps=TS)
    rw_s, wkv_s = _rwkv_sample(prepped, jnp.transpose(state_wkv[0], (1, 2, 3, 0)), hparams.T, n_heads=r_heads)
    wkv_s = jnp.transpose(wkv_s, (3, 0, 1, 2))
    n_pool, page = cache_k.shape[1], cache_k.shape[2]
    n_pages = page_table.shape[1]
    q0 = at0 * LANES
    qkvg_s = jnp.transpose(proj_s[:, q0:q0 + 4 * a_width].reshape(TS, DB, 4 * a_width), (1, 0, 2))
    kc = jnp.transpose(cache_k[0], (0, 2, 3, 4, 1)).reshape(n_pool, a_width, page)
    at_s = _attn_sample(qkvg_s, kc, cache_v[0].reshape(n_pool, page * a_heads, LANES), page_table.reshape(-1),
                        lam_params, subw, n_heads=a_heads, pages=_largest_tile(n_pages, 16), lam_init=lam_init)
    at_s = jnp.transpose(at_s, (1, 0, 2)).reshape(n_s, a_width)
    y_s = _outproj(xs, rw_s.astype(BF16), at_s.astype(BF16), proj_s, wbr, wba, wo, nf, tm=_largest_tile(n_s, 256))
    y_s = jnp.transpose(y_s.reshape(TS, DB, D), (1, 0, 2))

    def shift_row(last):
        return jnp.concatenate([last[:, c0:c0 + 4 * r_width], last[:, lora0 * LANES:lora0 * LANES + lora],
                                last[:, (lora0 + 1) * LANES:(lora0 + 1) * LANES + lora]], axis=-1)[None]

    return (y_p.reshape(B, T, D), y_s,
            k_p.reshape(1, B, T, a_heads, 2, HEAD_DIM), v_p.reshape(1, B, T, a_heads, LANES),
            qkvg_s[:, :, a_width:2 * a_width].reshape(1, DB, TS, a_heads, 2, HEAD_DIM),
            qkvg_s[:, :, 2 * a_width:3 * a_width].reshape(1, DB, TS, a_heads, LANES),
            wkv_p[None], wkv_s[None], shift_row(proj_p3[:, -1]), shift_row(proj_s[-DB:]))
```
